```python
import math
import jax, jax.numpy as jnp
from jax import lax
import numpy as np


D_MODEL = 1024
BATCH = 8
SEQ = 8192
DEPTH = 2

CTX_LEN = 256
GRID_W = 64
D_MIX = D_MODEL
CONV_CH = D_MIX // 2
CONV_K = 31
QK_NOPE = 64
QK_ROPE = 32
V_DIM = 64
MLA_HEADS = (D_MIX - CONV_CH) // V_DIM
QK_DIM = QK_NOPE + QK_ROPE
Q_LORA = 384
KV_LORA = 256
ROPE_THETA = 10000.0
SHORT_K = 3
FFN_DIM = 2816
FFN_K = 3
Q_BLOCK = 128
EPS = 1e-6
N_MOD = 6
EVEN_IN = 2 * CONV_CH + Q_LORA + KV_LORA + QK_ROPE
ODD_IN = 3 * D_MIX
SM_SCALE = QK_DIM ** -0.5

kernel_name = "hybrid_conformer_mla_shortconv_dit"


def rms_norm(x, g):
    xf = x.astype(jnp.float32)
    y = xf * lax.rsqrt(jnp.mean(xf * xf, axis=-1, keepdims=True) + EPS)
    return (y * g.astype(jnp.float32)).astype(x.dtype)


def layer_norm(x, g, b):
    xf = x.astype(jnp.float32)
    mu = jnp.mean(xf, axis=-1, keepdims=True)
    xc = xf - mu
    y = xc * lax.rsqrt(jnp.mean(xc * xc, axis=-1, keepdims=True) + EPS)
    return (y * g.astype(jnp.float32) + b.astype(jnp.float32)).astype(x.dtype)


def modulate(h, shift, scale):
    return h * (1 + scale) + shift


def dwconv(x, w, b):
    y = lax.conv_general_dilated(
        x, w[:, None, :].astype(x.dtype), window_strides=(1,), padding='SAME',
        dimension_numbers=('NWC', 'WIO', 'NWC'), feature_group_count=x.shape[-1])
    return y + b.astype(x.dtype)


def axial_tables(n_tokens):
    rows = n_tokens // GRID_W
    row = jnp.broadcast_to(jnp.arange(rows)[:, None], (rows, GRID_W)).reshape(-1)
    col = jnp.broadcast_to(jnp.arange(GRID_W)[None, :], (rows, GRID_W)).reshape(-1)
    half = QK_ROPE // 2
    inv = ROPE_THETA ** (-jnp.arange(0, half, 2, dtype=jnp.float32) / half)

    def cos_sin(pos):
        ang = pos.astype(jnp.float32)[:, None] * inv[None, :]
        ang = jnp.concatenate([ang, ang], axis=-1)
        return jnp.cos(ang), jnp.sin(ang)

    return cos_sin(row), cos_sin(col)


def rotate_half(v):
    v1, v2 = jnp.split(v, 2, axis=-1)
    return jnp.concatenate([-v2, v1], axis=-1)


def apply_axial_rope(x, tables):
    (cr, sr), (cc, sc) = tables
    half = QK_ROPE // 2

    def rot(v, cs, sn):
        vf = v.astype(jnp.float32)
        out = vf * cs[None, :, None, :] + rotate_half(vf) * sn[None, :, None, :]
        return out.astype(v.dtype)

    return jnp.concatenate([rot(x[..., :half], cr, sr), rot(x[..., half:], cc, sc)], axis=-1)


def split_even(proj):
    o1 = 2 * CONV_CH
    o2 = o1 + Q_LORA
    o3 = o2 + KV_LORA
    return proj[..., :o1], proj[..., o1:o2], proj[..., o2:o3], proj[..., o3:]


def conformer_conv(glu_in, conv_w, conv_b, ln_g, ln_b):
    val, gate = jnp.split(glu_in, 2, axis=-1)
    u = dwconv(val * jax.nn.sigmoid(gate), conv_w, conv_b)
    return jax.nn.silu(layer_norm(u, ln_g, ln_b))


def mla_q(cq, qa_g, w_uq, q_g, tables):
    b, l = cq.shape[:2]
    q = (rms_norm(cq, qa_g) @ w_uq).reshape(b, l, MLA_HEADS, QK_DIM)
    q = rms_norm(q, q_g)
    if tables is not None:
        q = jnp.concatenate([q[..., :QK_NOPE], apply_axial_rope(q[..., QK_NOPE:], tables)], axis=-1)
    return q


def mla_kv(ckv, kr, kva_g, w_ukv, k_g, tables):
    b, l = ckv.shape[:2]
    kv = (rms_norm(ckv, kva_g) @ w_ukv).reshape(b, l, MLA_HEADS, QK_NOPE + V_DIM)
    k_nope, v = kv[..., :QK_NOPE], kv[..., QK_NOPE:]
    k_rope = jnp.broadcast_to(kr[:, :, None, :], (b, l, MLA_HEADS, QK_ROPE))
    k = rms_norm(jnp.concatenate([k_nope, k_rope], axis=-1), k_g)
    if tables is not None:
        k = jnp.concatenate([k[..., :QK_NOPE], apply_axial_rope(k[..., QK_NOPE:], tables)], axis=-1)
    return k, v


def attend_blocks(q, k, v):
    b, s, h, dq = q.shape
    nb = s // Q_BLOCK
    qb = q.reshape(b, nb, Q_BLOCK, h, dq).transpose(1, 0, 2, 3, 4)

    def one_block(qi):
        sc = jnp.einsum('bqhd,bkhd->bhqk', qi, k, preferred_element_type=jnp.float32) * SM_SCALE
        p = jax.nn.softmax(sc, axis=-1).astype(v.dtype)
        return jnp.einsum('bhqk,bkhd->bqhd', p, v)

    o = lax.map(one_block, qb)
    return o.transpose(1, 0, 2, 3, 4).reshape(b, s, h * v.shape[-1])


def short_conv_mixer(h, w_in, conv_w, conv_b, w_out):
    bg, cg, u = jnp.split(h @ w_in, 3, axis=-1)
    return (bg * dwconv(cg * u, conv_w, conv_b)) @ w_out


def conv_ffn(h, w_up, conv_w, conv_b, w_down):
    gate, val = jnp.split(h @ w_up, 2, axis=-1)
    return (jax.nn.silu(dwconv(gate, conv_w, conv_b)) * val) @ w_down


def _fwd_setup_inputs(seed: int = 0) -> dict:
    key = jax.random.key(seed)
    ks = iter(jax.random.split(key, 40))
    f32 = jnp.float32
    D = D_MODEL
    ne = (DEPTH + 1) // 2
    no = DEPTH // 2

    def nrm(shape, scale):
        return scale * jax.random.normal(next(ks), shape, f32)

    def gain(shape):
        return 1.0 + 0.02 * jax.random.normal(next(ks), shape, f32)

    return {
        "x": nrm((BATCH, SEQ, D), 1.0),
        "c": nrm((BATCH, D), 1.0),
        "ctx": nrm((BATCH, CTX_LEN, D), 1.0),
        "c_ctx": nrm((D,), 1.0),
        "ada_w": nrm((DEPTH, D, N_MOD * D), 0.5 * D ** -0.5),
        "ada_b": nrm((DEPTH, N_MOD * D), 0.01),
        "norm_mix_g": gain((DEPTH, D)),
        "norm_ffn_g": gain((DEPTH, D)),
        "ffn_w_up": nrm((DEPTH, D, 2 * FFN_DIM), D ** -0.5),
        "ffn_conv_w": nrm((DEPTH, FFN_K, FFN_DIM), FFN_K ** -0.5),
        "ffn_conv_b": nrm((DEPTH, FFN_DIM), 0.01),
        "ffn_w_down": nrm((DEPTH, FFN_DIM, D), FFN_DIM ** -0.5),
        "ev_w_in": nrm((ne, D, EVEN_IN), D ** -0.5),
        "ev_conv_w": nrm((ne, CONV_K, CONV_CH), CONV_K ** -0.5),
        "ev_conv_b": nrm((ne, CONV_CH), 0.01),
        "ev_ln_g": gain((ne, CONV_CH)),
        "ev_ln_b": nrm((ne, CONV_CH), 0.01),
        "ev_qa_norm_g": gain((ne, Q_LORA)),
        "ev_w_uq": nrm((ne, Q_LORA, MLA_HEADS * QK_DIM), Q_LORA ** -0.5),
        "ev_kva_norm_g": gain((ne, KV_LORA)),
        "ev_w_ukv": nrm((ne, KV_LORA, MLA_HEADS * (QK_NOPE + V_DIM)), KV_LORA ** -0.5),
        "ev_q_norm_g": gain((ne, QK_DIM)),
        "ev_k_norm_g": gain((ne, QK_DIM)),
        "ev_w_out": nrm((ne, D_MIX, D), D_MIX ** -0.5),
        "od_w_in": nrm((no, D, ODD_IN), D ** -0.5),
        "od_conv_w": nrm((no, SHORT_K, D_MIX), SHORT_K ** -0.5),
        "od_conv_b": nrm((no, D_MIX), 0.01),
        "od_w_out": nrm((no, D_MIX, D), D_MIX ** -0.5),
    }


def _fwd_reference(x, c, ctx, c_ctx, ada_w, ada_b, norm_mix_g, norm_ffn_g, ffn_w_up, ffn_conv_w, ffn_conv_b,
              ffn_w_down, ev_w_in, ev_conv_w, ev_conv_b, ev_ln_g, ev_ln_b, ev_qa_norm_g, ev_w_uq,
              ev_kva_norm_g, ev_w_ukv, ev_q_norm_g, ev_k_norm_g, ev_w_out, od_w_in, od_conv_w, od_conv_b,
              od_w_out):
    tables = axial_tables(x.shape[1])
    silu_c = jax.nn.silu(c)
    silu_cc = jax.nn.silu(c_ctx)
    xc = ctx

    for layer in range(DEPTH):
        i = layer // 2
        is_even = layer % 2 == 0
        update_ctx = any(j % 2 == 0 for j in range(layer + 1, DEPTH))

        mod_lat = (silu_c @ ada_w[layer] + ada_b[layer])[:, None, :]
        sh_m, sc_m, g_m, sh_f, sc_f, g_f = jnp.split(mod_lat, N_MOD, axis=-1)
        if is_even or update_ctx:
            mod_ctx = (silu_cc @ ada_w[layer] + ada_b[layer])[None, None, :]
            csh_m, csc_m, cg_m, csh_f, csc_f, cg_f = jnp.split(mod_ctx, N_MOD, axis=-1)
            hc = modulate(rms_norm(xc, norm_mix_g[layer]), csh_m, csc_m)

        h = modulate(rms_norm(x, norm_mix_g[layer]), sh_m, sc_m)

        if is_even:
            glu_c, cq_c, ckv_c, kr_c = split_even(hc @ ev_w_in[i])
            k_c, v_c = mla_kv(ckv_c, kr_c, ev_kva_norm_g[i], ev_w_ukv[i], ev_k_norm_g[i], None)

            glu, cq, ckv, kr = split_even(h @ ev_w_in[i])
            a = conformer_conv(glu, ev_conv_w[i], ev_conv_b[i], ev_ln_g[i], ev_ln_b[i])
            q = mla_q(cq, ev_qa_norm_g[i], ev_w_uq[i], ev_q_norm_g[i], tables)
            k, v = mla_kv(ckv, kr, ev_kva_norm_g[i], ev_w_ukv[i], ev_k_norm_g[i], tables)
            k_all = jnp.concatenate([k_c, k], axis=1)
            v_all = jnp.concatenate([v_c, v], axis=1)
            att = attend_blocks(q, k_all, v_all)
            x = x + g_m * (jnp.concatenate([a, att], axis=-1) @ ev_w_out[i])

            if update_ctx:
                a_c = conformer_conv(glu_c, ev_conv_w[i], ev_conv_b[i], ev_ln_g[i], ev_ln_b[i])
                q_c = mla_q(cq_c, ev_qa_norm_g[i], ev_w_uq[i], ev_q_norm_g[i], None)
                att_c = attend_blocks(q_c, k_c, v_c)
                xc = xc + cg_m * (jnp.concatenate([a_c, att_c], axis=-1) @ ev_w_out[i])
        else:
            x = x + g_m * short_conv_mixer(h, od_w_in[i], od_conv_w[i], od_conv_b[i], od_w_out[i])
            if update_ctx:
                xc = xc + cg_m * short_conv_mixer(hc, od_w_in[i], od_conv_w[i], od_conv_b[i], od_w_out[i])

        hf = modulate(rms_norm(x, norm_ffn_g[layer]), sh_f, sc_f)
        x = x + g_f * conv_ffn(hf, ffn_w_up[layer], ffn_conv_w[layer], ffn_conv_b[layer], ffn_w_down[layer])
        if update_ctx:
            hfc = modulate(rms_norm(xc, norm_ffn_g[layer]), csh_f, csc_f)
            xc = xc + cg_f * conv_ffn(hfc, ffn_w_up[layer], ffn_conv_w[layer], ffn_conv_b[layer], ffn_w_down[layer])

    return x


import jax as _jax
import jax.numpy as _jnp

TWIN_FORMAT = 'train_step'
FWD_PARAMS = ['x', 'c', 'ctx', 'c_ctx', 'ada_w', 'ada_b', 'norm_mix_g', 'norm_ffn_g', 'ffn_w_up', 'ffn_conv_w', 'ffn_conv_b', 'ffn_w_down', 'ev_w_in', 'ev_conv_w', 'ev_conv_b', 'ev_ln_g', 'ev_ln_b', 'ev_qa_norm_g', 'ev_w_uq', 'ev_kva_norm_g', 'ev_w_ukv', 'ev_q_norm_g', 'ev_k_norm_g', 'ev_w_out', 'od_w_in', 'od_conv_w', 'od_conv_b', 'od_w_out']
TWIN_WEIGHTS = ['c_ctx', 'ada_w', 'ada_b', 'norm_mix_g', 'norm_ffn_g', 'ffn_w_up', 'ffn_conv_w', 'ffn_conv_b', 'ffn_w_down', 'ev_w_in', 'ev_conv_w', 'ev_conv_b', 'ev_ln_g', 'ev_ln_b', 'ev_qa_norm_g', 'ev_w_uq', 'ev_kva_norm_g', 'ev_w_ukv', 'ev_q_norm_g', 'ev_k_norm_g', 'ev_w_out', 'od_w_in', 'od_conv_w', 'od_conv_b', 'od_w_out']
TWIN_DIFF_INPUT = 'x'
TWIN_INPUTS = ['x', 'c', 'ctx', 'c_ctx', 'ada_w', 'ada_b', 'norm_mix_g', 'norm_ffn_g', 'ffn_w_up', 'ffn_conv_w', 'ffn_conv_b', 'ffn_w_down', 'ev_w_in', 'ev_conv_w', 'ev_conv_b', 'ev_ln_g', 'ev_ln_b', 'ev_qa_norm_g', 'ev_w_uq', 'ev_kva_norm_g', 'ev_w_ukv', 'ev_q_norm_g', 'ev_k_norm_g', 'ev_w_out', 'od_w_in', 'od_conv_w', 'od_conv_b', 'od_w_out', 'loss_target', 'm_c_ctx', 'm_ada_w', 'm_ada_b', 'm_norm_mix_g', 'm_norm_ffn_g', 'm_ffn_w_up', 'm_ffn_conv_w', 'm_ffn_conv_b', 'm_ffn_w_down', 'm_ev_w_in', 'm_ev_conv_w', 'm_ev_conv_b', 'm_ev_ln_g', 'm_ev_ln_b', 'm_ev_qa_norm_g', 'm_ev_w_uq', 'm_ev_kva_norm_g', 'm_ev_w_ukv', 'm_ev_q_norm_g', 'm_ev_k_norm_g', 'm_ev_w_out', 'm_od_w_in', 'm_od_conv_w', 'm_od_conv_b', 'm_od_w_out', 'v_c_ctx', 'v_ada_w', 'v_ada_b', 'v_norm_mix_g', 'v_norm_ffn_g', 'v_ffn_w_up', 'v_ffn_conv_w', 'v_ffn_conv_b', 'v_ffn_w_down', 'v_ev_w_in', 'v_ev_conv_w', 'v_ev_conv_b', 'v_ev_ln_g', 'v_ev_ln_b', 'v_ev_qa_norm_g', 'v_ev_w_uq', 'v_ev_kva_norm_g', 'v_ev_w_ukv', 'v_ev_q_norm_g', 'v_ev_k_norm_g', 'v_ev_w_out', 'v_od_w_in', 'v_od_conv_w', 'v_od_conv_b', 'v_od_w_out']
TWIN_OUTPUTS = ['loss', 'grad_x', 'grad_c_ctx', 'grad_ada_w', 'grad_ada_b', 'grad_norm_mix_g', 'grad_norm_ffn_g', 'grad_ffn_w_up', 'grad_ffn_conv_w', 'grad_ffn_conv_b', 'grad_ffn_w_down', 'grad_ev_w_in', 'grad_ev_conv_w', 'grad_ev_conv_b', 'grad_ev_ln_g', 'grad_ev_ln_b', 'grad_ev_qa_norm_g', 'grad_ev_w_uq', 'grad_ev_kva_norm_g', 'grad_ev_w_ukv', 'grad_ev_q_norm_g', 'grad_ev_k_norm_g', 'grad_ev_w_out', 'grad_od_w_in', 'grad_od_conv_w', 'grad_od_conv_b', 'grad_od_w_out', 'delta_c_ctx', 'delta_ada_w', 'delta_ada_b', 'delta_norm_mix_g', 'delta_norm_ffn_g', 'delta_ffn_w_up', 'delta_ffn_conv_w', 'delta_ffn_conv_b', 'delta_ffn_w_down', 'delta_ev_w_in', 'delta_ev_conv_w', 'delta_ev_conv_b', 'delta_ev_ln_g', 'delta_ev_ln_b', 'delta_ev_qa_norm_g', 'delta_ev_w_uq', 'delta_ev_kva_norm_g', 'delta_ev_w_ukv', 'delta_ev_q_norm_g', 'delta_ev_k_norm_g', 'delta_ev_w_out', 'delta_od_w_in', 'delta_od_conv_w', 'delta_od_conv_b', 'delta_od_w_out', 'new_m_c_ctx', 'new_m_ada_w', 'new_m_ada_b', 'new_m_norm_mix_g', 'new_m_norm_ffn_g', 'new_m_ffn_w_up', 'new_m_ffn_conv_w', 'new_m_ffn_conv_b', 'new_m_ffn_w_down', 'new_m_ev_w_in', 'new_m_ev_conv_w', 'new_m_ev_conv_b', 'new_m_ev_ln_g', 'new_m_ev_ln_b', 'new_m_ev_qa_norm_g', 'new_m_ev_w_uq', 'new_m_ev_kva_norm_g', 'new_m_ev_w_ukv', 'new_m_ev_q_norm_g', 'new_m_ev_k_norm_g', 'new_m_ev_w_out', 'new_m_od_w_in', 'new_m_od_conv_w', 'new_m_od_conv_b', 'new_m_od_w_out', 'new_v_c_ctx', 'new_v_ada_w', 'new_v_ada_b', 'new_v_norm_mix_g', 'new_v_norm_ffn_g', 'new_v_ffn_w_up', 'new_v_ffn_conv_w', 'new_v_ffn_conv_b', 'new_v_ffn_w_down', 'new_v_ev_w_in', 'new_v_ev_conv_w', 'new_v_ev_conv_b', 'new_v_ev_ln_g', 'new_v_ev_ln_b', 'new_v_ev_qa_norm_g', 'new_v_ev_w_uq', 'new_v_ev_kva_norm_g', 'new_v_ev_w_ukv', 'new_v_ev_q_norm_g', 'new_v_ev_k_norm_g', 'new_v_ev_w_out', 'new_v_od_w_in', 'new_v_od_conv_w', 'new_v_od_conv_b', 'new_v_od_w_out']
TWIN_LEAF_KINDS = {'loss': 'loss', 'grad_x': 'grad_x', 'grad_c_ctx': 'grad_w', 'grad_ada_w': 'grad_w', 'grad_ada_b': 'grad_w', 'grad_norm_mix_g': 'grad_w', 'grad_norm_ffn_g': 'grad_w', 'grad_ffn_w_up': 'grad_w', 'grad_ffn_conv_w': 'grad_w', 'grad_ffn_conv_b': 'grad_w', 'grad_ffn_w_down': 'grad_w', 'grad_ev_w_in': 'grad_w', 'grad_ev_conv_w': 'grad_w', 'grad_ev_conv_b': 'grad_w', 'grad_ev_ln_g': 'grad_w', 'grad_ev_ln_b': 'grad_w', 'grad_ev_qa_norm_g': 'grad_w', 'grad_ev_w_uq': 'grad_w', 'grad_ev_kva_norm_g': 'grad_w', 'grad_ev_w_ukv': 'grad_w', 'grad_ev_q_norm_g': 'grad_w', 'grad_ev_k_norm_g': 'grad_w', 'grad_ev_w_out': 'grad_w', 'grad_od_w_in': 'grad_w', 'grad_od_conv_w': 'grad_w', 'grad_od_conv_b': 'grad_w', 'grad_od_w_out': 'grad_w', 'delta_c_ctx': 'delta_w', 'delta_ada_w': 'delta_w', 'delta_ada_b': 'delta_w', 'delta_norm_mix_g': 'delta_w', 'delta_norm_ffn_g': 'delta_w', 'delta_ffn_w_up': 'delta_w', 'delta_ffn_conv_w': 'delta_w', 'delta_ffn_conv_b': 'delta_w', 'delta_ffn_w_down': 'delta_w', 'delta_ev_w_in': 'delta_w', 'delta_ev_conv_w': 'delta_w', 'delta_ev_conv_b': 'delta_w', 'delta_ev_ln_g': 'delta_w', 'delta_ev_ln_b': 'delta_w', 'delta_ev_qa_norm_g': 'delta_w', 'delta_ev_w_uq': 'delta_w', 'delta_ev_kva_norm_g': 'delta_w', 'delta_ev_w_ukv': 'delta_w', 'delta_ev_q_norm_g': 'delta_w', 'delta_ev_k_norm_g': 'delta_w', 'delta_ev_w_out': 'delta_w', 'delta_od_w_in': 'delta_w', 'delta_od_conv_w': 'delta_w', 'delta_od_conv_b': 'delta_w', 'delta_od_w_out': 'delta_w', 'new_m_c_ctx': 'new_m', 'new_m_ada_w': 'new_m', 'new_m_ada_b': 'new_m', 'new_m_norm_mix_g': 'new_m', 'new_m_norm_ffn_g': 'new_m', 'new_m_ffn_w_up': 'new_m', 'new_m_ffn_conv_w': 'new_m', 'new_m_ffn_conv_b': 'new_m', 'new_m_ffn_w_down': 'new_m', 'new_m_ev_w_in': 'new_m', 'new_m_ev_conv_w': 'new_m', 'new_m_ev_conv_b': 'new_m', 'new_m_ev_ln_g': 'new_m', 'new_m_ev_ln_b': 'new_m', 'new_m_ev_qa_norm_g': 'new_m', 'new_m_ev_w_uq': 'new_m', 'new_m_ev_kva_norm_g': 'new_m', 'new_m_ev_w_ukv': 'new_m', 'new_m_ev_q_norm_g': 'new_m', 'new_m_ev_k_norm_g': 'new_m', 'new_m_ev_w_out': 'new_m', 'new_m_od_w_in': 'new_m', 'new_m_od_conv_w': 'new_m', 'new_m_od_conv_b': 'new_m', 'new_m_od_w_out': 'new_m', 'new_v_c_ctx': 'new_v', 'new_v_ada_w': 'new_v', 'new_v_ada_b': 'new_v', 'new_v_norm_mix_g': 'new_v', 'new_v_norm_ffn_g': 'new_v', 'new_v_ffn_w_up': 'new_v', 'new_v_ffn_conv_w': 'new_v', 'new_v_ffn_conv_b': 'new_v', 'new_v_ffn_w_down': 'new_v', 'new_v_ev_w_in': 'new_v', 'new_v_ev_conv_w': 'new_v', 'new_v_ev_conv_b': 'new_v', 'new_v_ev_ln_g': 'new_v', 'new_v_ev_ln_b': 'new_v', 'new_v_ev_qa_norm_g': 'new_v', 'new_v_ev_w_uq': 'new_v', 'new_v_ev_kva_norm_g': 'new_v', 'new_v_ev_w_ukv': 'new_v', 'new_v_ev_q_norm_g': 'new_v', 'new_v_ev_k_norm_g': 'new_v', 'new_v_ev_w_out': 'new_v', 'new_v_od_w_in': 'new_v', 'new_v_od_conv_w': 'new_v', 'new_v_od_conv_b': 'new_v', 'new_v_od_w_out': 'new_v'}


def _forward(args):
    return _fwd_reference(*[args[k] for k in FWD_PARAMS])


def _output_shape():
    out = _jax.eval_shape(lambda: _forward(_fwd_setup_inputs(0)))
    return out.shape, out.dtype

N_MICROBATCH = 1
ADAM_LR = 0.001
ADAM_B1 = 0.9
ADAM_B2 = 0.999
ADAM_EPS = 1e-08
ADAM_WD = 0.01
ADAM_STEP = 10
PER_EXAMPLE_BATCH_AXIS = {'x': 0, 'c': 0, 'ctx': 0, 'loss_target': 0}
SHARED_INPUTS = []
_WEIGHT_DTYPES = {'c_ctx': _jnp.float32, 'ada_w': _jnp.float32, 'ada_b': _jnp.float32, 'norm_mix_g': _jnp.float32, 'norm_ffn_g': _jnp.float32, 'ffn_w_up': _jnp.float32, 'ffn_conv_w': _jnp.float32, 'ffn_conv_b': _jnp.float32, 'ffn_w_down': _jnp.float32, 'ev_w_in': _jnp.float32, 'ev_conv_w': _jnp.float32, 'ev_conv_b': _jnp.float32, 'ev_ln_g': _jnp.float32, 'ev_ln_b': _jnp.float32, 'ev_qa_norm_g': _jnp.float32, 'ev_w_uq': _jnp.float32, 'ev_kva_norm_g': _jnp.float32, 'ev_w_ukv': _jnp.float32, 'ev_q_norm_g': _jnp.float32, 'ev_k_norm_g': _jnp.float32, 'ev_w_out': _jnp.float32, 'od_w_in': _jnp.float32, 'od_conv_w': _jnp.float32, 'od_conv_b': _jnp.float32, 'od_w_out': _jnp.float32}
MOMENT_SCALE = {'c_ctx': 4.965530e-02, 'ada_w': 4.623164e+00, 'ada_b': 1.053250e+01, 'norm_mix_g': 2.144780e+01, 'norm_ffn_g': 6.931164e+00, 'ffn_w_up': 1.370730e-01, 'ffn_conv_w': 7.354641e-01, 'ffn_conv_b': 8.484384e-01, 'ffn_w_down': 1.541172e-01, 'ev_w_in': 2.828430e-01, 'ev_conv_w': 2.196345e-01, 'ev_conv_b': 2.136394e+00, 'ev_ln_g': 3.165861e+00, 'ev_ln_b': 2.341739e+00, 'ev_qa_norm_g': 2.757744e-02, 'ev_w_uq': 1.727498e-02, 'ev_kva_norm_g': 1.296807e+00, 'ev_w_ukv': 4.641800e-01, 'ev_q_norm_g': 6.377248e-02, 'ev_k_norm_g': 6.559343e-02, 'ev_w_out': 4.827107e-01, 'od_w_in': 4.561888e-01, 'od_conv_w': 5.911186e+00, 'od_conv_b': 4.176685e-01, 'od_w_out': 3.132416e-01}


def _to_microbatches(a, axis):
    t = _jnp.moveaxis(a, axis, 0)
    t = t.reshape((N_MICROBATCH, t.shape[0] // N_MICROBATCH) + t.shape[1:])
    return _jnp.moveaxis(t, 1, axis + 1)


def setup_inputs(seed: int = 0) -> dict:
    inp = _fwd_setup_inputs(seed)
    key = _jax.random.fold_in(_jax.random.key(seed), 7919)
    shape, _ = _output_shape()
    out = dict(inp)
    out["loss_target"] = _jax.random.normal(_jax.random.fold_in(key, 0), shape, _jnp.float32)
    for i, name in enumerate(TWIN_WEIGHTS):
        w = inp[name].astype(_jnp.float32)
        if MOMENT_SCALE is None:
            s = _jnp.sqrt(_jnp.mean(_jnp.square(w)) + 1e-30)
        else:
            s = MOMENT_SCALE[name]
        km, kv = _jax.random.split(_jax.random.fold_in(key, i + 1))
        out[name] = w
        out["m_" + name] = s * _jax.random.normal(km, w.shape, _jnp.float32)
        out["v_" + name] = (s * s) * _jax.random.uniform(kv, w.shape, _jnp.float32, 0.5, 1.5)
    if N_MICROBATCH > 1:
        for name, axis in PER_EXAMPLE_BATCH_AXIS.items():
            out[name] = _to_microbatches(out[name], axis)
    return {'x': out['x'], 'c': out['c'], 'ctx': out['ctx'], 'c_ctx': out['c_ctx'], 'ada_w': out['ada_w'], 'ada_b': out['ada_b'], 'norm_mix_g': out['norm_mix_g'], 'norm_ffn_g': out['norm_ffn_g'], 'ffn_w_up': out['ffn_w_up'], 'ffn_conv_w': out['ffn_conv_w'], 'ffn_conv_b': out['ffn_conv_b'], 'ffn_w_down': out['ffn_w_down'], 'ev_w_in': out['ev_w_in'], 'ev_conv_w': out['ev_conv_w'], 'ev_conv_b': out['ev_conv_b'], 'ev_ln_g': out['ev_ln_g'], 'ev_ln_b': out['ev_ln_b'], 'ev_qa_norm_g': out['ev_qa_norm_g'], 'ev_w_uq': out['ev_w_uq'], 'ev_kva_norm_g': out['ev_kva_norm_g'], 'ev_w_ukv': out['ev_w_ukv'], 'ev_q_norm_g': out['ev_q_norm_g'], 'ev_k_norm_g': out['ev_k_norm_g'], 'ev_w_out': out['ev_w_out'], 'od_w_in': out['od_w_in'], 'od_conv_w': out['od_conv_w'], 'od_conv_b': out['od_conv_b'], 'od_w_out': out['od_w_out'], 'loss_target': out['loss_target'], 'm_c_ctx': out['m_c_ctx'], 'm_ada_w': out['m_ada_w'], 'm_ada_b': out['m_ada_b'], 'm_norm_mix_g': out['m_norm_mix_g'], 'm_norm_ffn_g': out['m_norm_ffn_g'], 'm_ffn_w_up': out['m_ffn_w_up'], 'm_ffn_conv_w': out['m_ffn_conv_w'], 'm_ffn_conv_b': out['m_ffn_conv_b'], 'm_ffn_w_down': out['m_ffn_w_down'], 'm_ev_w_in': out['m_ev_w_in'], 'm_ev_conv_w': out['m_ev_conv_w'], 'm_ev_conv_b': out['m_ev_conv_b'], 'm_ev_ln_g': out['m_ev_ln_g'], 'm_ev_ln_b': out['m_ev_ln_b'], 'm_ev_qa_norm_g': out['m_ev_qa_norm_g'], 'm_ev_w_uq': out['m_ev_w_uq'], 'm_ev_kva_norm_g': out['m_ev_kva_norm_g'], 'm_ev_w_ukv': out['m_ev_w_ukv'], 'm_ev_q_norm_g': out['m_ev_q_norm_g'], 'm_ev_k_norm_g': out['m_ev_k_norm_g'], 'm_ev_w_out': out['m_ev_w_out'], 'm_od_w_in': out['m_od_w_in'], 'm_od_conv_w': out['m_od_conv_w'], 'm_od_conv_b': out['m_od_conv_b'], 'm_od_w_out': out['m_od_w_out'], 'v_c_ctx': out['v_c_ctx'], 'v_ada_w': out['v_ada_w'], 'v_ada_b': out['v_ada_b'], 'v_norm_mix_g': out['v_norm_mix_g'], 'v_norm_ffn_g': out['v_norm_ffn_g'], 'v_ffn_w_up': out['v_ffn_w_up'], 'v_ffn_conv_w': out['v_ffn_conv_w'], 'v_ffn_conv_b': out['v_ffn_conv_b'], 'v_ffn_w_down': out['v_ffn_w_down'], 'v_ev_w_in': out['v_ev_w_in'], 'v_ev_conv_w': out['v_ev_conv_w'], 'v_ev_conv_b': out['v_ev_conv_b'], 'v_ev_ln_g': out['v_ev_ln_g'], 'v_ev_ln_b': out['v_ev_ln_b'], 'v_ev_qa_norm_g': out['v_ev_qa_norm_g'], 'v_ev_w_uq': out['v_ev_w_uq'], 'v_ev_kva_norm_g': out['v_ev_kva_norm_g'], 'v_ev_w_ukv': out['v_ev_w_ukv'], 'v_ev_q_norm_g': out['v_ev_q_norm_g'], 'v_ev_k_norm_g': out['v_ev_k_norm_g'], 'v_ev_w_out': out['v_ev_w_out'], 'v_od_w_in': out['v_od_w_in'], 'v_od_conv_w': out['v_od_conv_w'], 'v_od_conv_b': out['v_od_conv_b'], 'v_od_w_out': out['v_od_w_out']}


def _loss(weights, diff, rest, loss_target):
    with _jax.named_scope("forward"):
        args = {**rest, TWIN_DIFF_INPUT: diff, **{k: w.astype(_WEIGHT_DTYPES[k]) for k, w in weights.items()}}
        y = _forward(args)
    with _jax.named_scope("loss_head"):
        err = _jnp.square(y.astype(_jnp.float32) - loss_target)
        return 0.5 * _jnp.sum(_jnp.mean(err, axis=-1)) if err.ndim else 0.5 * err


def _adamw(w, g, m, v):
    m = ADAM_B1 * m + (1.0 - ADAM_B1) * g
    v = ADAM_B2 * v + (1.0 - ADAM_B2) * _jnp.square(g)
    m_hat = m / (1.0 - ADAM_B1 ** ADAM_STEP)
    v_hat = v / (1.0 - ADAM_B2 ** ADAM_STEP)
    delta = -ADAM_LR * (m_hat / (_jnp.sqrt(v_hat) + ADAM_EPS) + ADAM_WD * w)
    return delta, m, v


def reference(x, c, ctx, c_ctx, ada_w, ada_b, norm_mix_g, norm_ffn_g, ffn_w_up, ffn_conv_w, ffn_conv_b, ffn_w_down, ev_w_in, ev_conv_w, ev_conv_b, ev_ln_g, ev_ln_b, ev_qa_norm_g, ev_w_uq, ev_kva_norm_g, ev_w_ukv, ev_q_norm_g, ev_k_norm_g, ev_w_out, od_w_in, od_conv_w, od_conv_b, od_w_out, loss_target, m_c_ctx, m_ada_w, m_ada_b, m_norm_mix_g, m_norm_ffn_g, m_ffn_w_up, m_ffn_conv_w, m_ffn_conv_b, m_ffn_w_down, m_ev_w_in, m_ev_conv_w, m_ev_conv_b, m_ev_ln_g, m_ev_ln_b, m_ev_qa_norm_g, m_ev_w_uq, m_ev_kva_norm_g, m_ev_w_ukv, m_ev_q_norm_g, m_ev_k_norm_g, m_ev_w_out, m_od_w_in, m_od_conv_w, m_od_conv_b, m_od_w_out, v_c_ctx, v_ada_w, v_ada_b, v_norm_mix_g, v_norm_ffn_g, v_ffn_w_up, v_ffn_conv_w, v_ffn_conv_b, v_ffn_w_down, v_ev_w_in, v_ev_conv_w, v_ev_conv_b, v_ev_ln_g, v_ev_ln_b, v_ev_qa_norm_g, v_ev_w_uq, v_ev_kva_norm_g, v_ev_w_ukv, v_ev_q_norm_g, v_ev_k_norm_g, v_ev_w_out, v_od_w_in, v_od_conv_w, v_od_conv_b, v_od_w_out):
    given = dict(x=x, c=c, ctx=ctx, c_ctx=c_ctx, ada_w=ada_w, ada_b=ada_b, norm_mix_g=norm_mix_g, norm_ffn_g=norm_ffn_g, ffn_w_up=ffn_w_up, ffn_conv_w=ffn_conv_w, ffn_conv_b=ffn_conv_b, ffn_w_down=ffn_w_down, ev_w_in=ev_w_in, ev_conv_w=ev_conv_w, ev_conv_b=ev_conv_b, ev_ln_g=ev_ln_g, ev_ln_b=ev_ln_b, ev_qa_norm_g=ev_qa_norm_g, ev_w_uq=ev_w_uq, ev_kva_norm_g=ev_kva_norm_g, ev_w_ukv=ev_w_ukv, ev_q_norm_g=ev_q_norm_g, ev_k_norm_g=ev_k_norm_g, ev_w_out=ev_w_out, od_w_in=od_w_in, od_conv_w=od_conv_w, od_conv_b=od_conv_b, od_w_out=od_w_out, loss_target=loss_target, m_c_ctx=m_c_ctx, m_ada_w=m_ada_w, m_ada_b=m_ada_b, m_norm_mix_g=m_norm_mix_g, m_norm_ffn_g=m_norm_ffn_g, m_ffn_w_up=m_ffn_w_up, m_ffn_conv_w=m_ffn_conv_w, m_ffn_conv_b=m_ffn_conv_b, m_ffn_w_down=m_ffn_w_down, m_ev_w_in=m_ev_w_in, m_ev_conv_w=m_ev_conv_w, m_ev_conv_b=m_ev_conv_b, m_ev_ln_g=m_ev_ln_g, m_ev_ln_b=m_ev_ln_b, m_ev_qa_norm_g=m_ev_qa_norm_g, m_ev_w_uq=m_ev_w_uq, m_ev_kva_norm_g=m_ev_kva_norm_g, m_ev_w_ukv=m_ev_w_ukv, m_ev_q_norm_g=m_ev_q_norm_g, m_ev_k_norm_g=m_ev_k_norm_g, m_ev_w_out=m_ev_w_out, m_od_w_in=m_od_w_in, m_od_conv_w=m_od_conv_w, m_od_conv_b=m_od_conv_b, m_od_w_out=m_od_w_out, v_c_ctx=v_c_ctx, v_ada_w=v_ada_w, v_ada_b=v_ada_b, v_norm_mix_g=v_norm_mix_g, v_norm_ffn_g=v_norm_ffn_g, v_ffn_w_up=v_ffn_w_up, v_ffn_conv_w=v_ffn_conv_w, v_ffn_conv_b=v_ffn_conv_b, v_ffn_w_down=v_ffn_w_down, v_ev_w_in=v_ev_w_in, v_ev_conv_w=v_ev_conv_w, v_ev_conv_b=v_ev_conv_b, v_ev_ln_g=v_ev_ln_g, v_ev_ln_b=v_ev_ln_b, v_ev_qa_norm_g=v_ev_qa_norm_g, v_ev_w_uq=v_ev_w_uq, v_ev_kva_norm_g=v_ev_kva_norm_g, v_ev_w_ukv=v_ev_w_ukv, v_ev_q_norm_g=v_ev_q_norm_g, v_ev_k_norm_g=v_ev_k_norm_g, v_ev_w_out=v_ev_w_out, v_od_w_in=v_od_w_in, v_od_conv_w=v_od_conv_w, v_od_conv_b=v_od_conv_b, v_od_w_out=v_od_w_out)
    weights = {n: given[n] for n in TWIN_WEIGHTS}
    shared = {n: given[n] for n in SHARED_INPUTS}
    per_example = {n: given[n] for n in ['x', 'c', 'ctx']}
    grad_fn = _jax.value_and_grad(_loss, argnums=(0, 1))

    def one_microbatch(ex, loss_target):
        ex = dict(ex)
        diff = ex.pop(TWIN_DIFF_INPUT)
        return grad_fn(weights, diff, {**shared, **ex}, loss_target)

    if N_MICROBATCH == 1:
        loss, (grad_w, grad_x) = one_microbatch(per_example, given["loss_target"])
    else:
        def body(carry, xs):
            loss_sum, grad_sum = carry
            l_k, (gw_k, gx_k) = one_microbatch(xs[0], xs[1])
            with _jax.named_scope("update"):
                return (loss_sum + l_k, _jax.tree.map(_jnp.add, grad_sum, gw_k)), gx_k

        init = (_jnp.zeros((), _jnp.float32), _jax.tree.map(_jnp.zeros_like, weights))
        (loss, grad_w), grad_x = _jax.lax.scan(body, init, (per_example, given["loss_target"]))
    with _jax.named_scope("update"):
        delta_w, new_m, new_v = {}, {}, {}
        for n in TWIN_WEIGHTS:
            delta_w[n], new_m[n], new_v[n] = _adamw(weights[n], grad_w[n], given["m_" + n], given["v_" + n])
    return (loss, grad_x, *[grad_w[n] for n in TWIN_WEIGHTS], *[delta_w[n] for n in TWIN_WEIGHTS],
            *[new_m[n] for n in TWIN_WEIGHTS], *[new_v[n] for n in TWIN_WEIGHTS])
```

```python
import functools

import jax
import jax.numpy as jnp
from jax import lax
from jax.experimental import pallas as pl
from jax.experimental.pallas import tpu as pltpu

F32, BF = jnp.float32, jnp.bfloat16
N_DEV = 8
MESH_AXES = ("x", "y", "c")
LANES = 128
SUBLANES = 8
HALO_ROWS = 16
VMEM_LIMIT = 56 << 20
PACK_W = 1024
EPS = 1e-6
QK_NOPE, QK_ROPE, V_DIM, GRID_W = 64, 32, 64, 64
QK_DIM = QK_NOPE + QK_ROPE
ROPE_THETA = 10000.0
SM_SCALE = QK_DIM ** -0.5
N_MOD = 6
ADAM_LR, ADAM_B1, ADAM_B2, ADAM_EPS, ADAM_WD, ADAM_STEP = 0.001, 0.9, 0.999, 1e-08, 0.01, 10


def _div(n, cap):
    if n <= cap:
        return n
    best = None
    for d in range(LANES, cap + 1, LANES):
        if n % d == 0:
            best = d
    return n if best is None else best


def _cp(*sem):
    return pltpu.CompilerParams(dimension_semantics=sem, vmem_limit_bytes=VMEM_LIMIT)


def _rms(x, n=None):
    d = x.shape[-1] if n is None else n
    return x * lax.rsqrt(jnp.sum(x * x, axis=-1, keepdims=True) / d + EPS)


@functools.lru_cache(maxsize=None)
def _lane_roll(shift):
    @jax.custom_vjp
    def roll(x):
        return pltpu.roll(x, shift, 1)

    def fwd(x):
        return roll(x), None

    def bwd(_, g):
        return (pltpu.roll(g, (LANES - shift) % LANES, 1),)

    roll.defvjp(fwd, bwd)
    return roll


def _mm(a, b, *, ta=False, tb=False, out_dtype=F32, res=None, gate=None, name):
    m, k = (a.shape[1], a.shape[0]) if ta else a.shape
    n = b.shape[0] if tb else b.shape[1]
    tm, tn, tk = _div(m, 512), _div(n, 1408), _div(k, 1536)
    nk = k // tk
    a_spec = pl.BlockSpec((tk, tm), lambda i, j, kk: (kk, i)) if ta else pl.BlockSpec((tm, tk), lambda i, j, kk: (i, kk))
    b_spec = pl.BlockSpec((tn, tk), lambda i, j, kk: (j, kk)) if tb else pl.BlockSpec((tk, tn), lambda i, j, kk: (kk, j))
    o_spec = pl.BlockSpec((tm, tn), lambda i, j, kk: (i, j))
    dn = (((0 if ta else 1,), (1 if tb else 0,)), ((), ()))
    fused = res is not None

    def body(*refs):
        if fused:
            a_ref, b_ref, res_ref, gate_ref, o_ref, f_ref, acc = refs
        else:
            a_ref, b_ref, o_ref, acc = refs
        kk = pl.program_id(2)

        @pl.when(kk == 0)
        def _():
            acc[...] = jnp.zeros_like(acc)

        acc[...] += lax.dot_general(a_ref[...].astype(BF), b_ref[...].astype(BF), dn, preferred_element_type=F32)

        @pl.when(kk == nk - 1)
        def _():
            if fused:
                f_ref[...] = acc[...]
                o_ref[...] = res_ref[...] + gate_ref[...] * acc[...]
            else:
                o_ref[...] = acc[...].astype(out_dtype)

    in_specs, args = [a_spec, b_spec], [a, b]
    out_specs, out_shape = o_spec, jax.ShapeDtypeStruct((m, n), out_dtype)
    if fused:
        in_specs += [o_spec, pl.BlockSpec((1, tn), lambda i, j, kk: (0, j))]
        args += [res, gate]
        out_specs = [o_spec, o_spec]
        out_shape = [jax.ShapeDtypeStruct((m, n), F32), jax.ShapeDtypeStruct((m, n), F32)]
    return pl.pallas_call(
        body, grid=(m // tm, n // tn, nk), in_specs=in_specs, out_specs=out_specs, out_shape=out_shape,
        scratch_shapes=[pltpu.VMEM((tm, tn), F32)], compiler_params=_cp("parallel", "parallel", "arbitrary"), name=name,
    )(*args)


def _row(tm, c, off=0):
    return pl.BlockSpec((tm, c), lambda i: (i + off, 0))


def _full(shape):
    return pl.BlockSpec(shape, lambda *_: (0,) * len(shape))


def _acc(ref, val, first):
    @pl.when(first)
    def _():
        ref[...] = jnp.zeros_like(ref)

    ref[...] += val


def _modnorm_f(x, g, sc, sh):
    return (_rms(x) * g) * (1.0 + sc) + sh


def _cls_spec(ncls, nctx, d):
    if ncls == 2:
        return pl.BlockSpec((1, 1, d), lambda i: (jnp.where(i < nctx, 0, 1), 0, 0))
    return pl.BlockSpec((1, 1, d), lambda i: (0, 0, 0))


def _modnorm(x, g, sc, sh, *, nctx, tm, name):
    r, d = x.shape
    cls = _cls_spec(sc.shape[0], nctx, d)

    def body(x_ref, g_ref, sc_ref, sh_ref, o_ref):
        o_ref[...] = _modnorm_f(x_ref[...], g_ref[...], sc_ref[0], sh_ref[0]).astype(BF)

    return pl.pallas_call(
        body, grid=(r // tm,), in_specs=[_row(tm, d), _full((1, d)), cls, cls], out_specs=_row(tm, d),
        out_shape=jax.ShapeDtypeStruct((r, d), BF), compiler_params=_cp("parallel"), name=name,
    )(x, g, sc, sh)


def _modnorm_bwd(x, dh, dres, g, sc, sh, *, nctx, tm, name):
    r, d = x.shape
    ncls = sc.shape[0]
    s = r - nctx * tm
    cls = _cls_spec(ncls, nctx, d)
    lat = pl.BlockSpec((tm, d), lambda i: (jnp.maximum(i - nctx, 0), 0))

    def body(x_ref, dh_ref, dres_ref, g_ref, sc_ref, sh_ref, dx_ref, dg_ref, dsc_ref, dsh_ref):
        i = pl.program_id(0)
        _, vjp = jax.vjp(_modnorm_f, x_ref[...], g_ref[...], sc_ref[0], sh_ref[0])
        dx, dg, dsc, dsh = vjp(dh_ref[...])
        _acc(dg_ref, dg, i == 0)
        first = (i == 0) | (i == nctx) if ncls == 2 else i == 0
        _acc(dsc_ref, dsc[None], first)
        _acc(dsh_ref, dsh[None], first)

        @pl.when(i >= nctx)
        def _():
            dx_ref[...] = dx + dres_ref[...]

    return pl.pallas_call(
        body, grid=(r // tm,), in_specs=[_row(tm, d), _row(tm, d), lat, _full((1, d)), cls, cls],
        out_specs=[lat, _full((1, d)), cls, cls],
        out_shape=[jax.ShapeDtypeStruct((s, d), F32), jax.ShapeDtypeStruct((1, d), F32),
                   jax.ShapeDtypeStruct((ncls, 1, d), F32), jax.ShapeDtypeStruct((ncls, 1, d), F32)],
        compiler_params=_cp("arbitrary"), name=name,
    )(x, dh, dres, g, sc, sh)


def _even_parts(p, c, ql, kvl):
    return p[:, :c], p[:, c:2 * c], p[:, 2 * c:2 * c + ql], p[:, 2 * c + ql:2 * c + ql + kvl]


def _even_mid_f(val, gate, cq, ckv, qa_g, kva_g):
    return val * jax.nn.sigmoid(gate), _rms(cq) * qa_g, _rms(ckv) * kva_g


def _even_mid(p, qa_g, kva_g, *, c, tm, name):
    r, w = p.shape
    ql, kvl = qa_g.shape[1], kva_g.shape[1]

    def body(p_ref, qg_ref, kg_ref, gl_ref, qn_ref, kvn_ref):
        gl, qn, kvn = _even_mid_f(*_even_parts(p_ref[...], c, ql, kvl), qg_ref[...], kg_ref[...])
        gl_ref[...] = gl
        qn_ref[...] = qn.astype(BF)
        kvn_ref[...] = kvn.astype(BF)

    return pl.pallas_call(
        body, grid=(r // tm,), in_specs=[_row(tm, w), _full((1, ql)), _full((1, kvl))],
        out_specs=[_row(tm, c), _row(tm, ql), _row(tm, kvl)],
        out_shape=[jax.ShapeDtypeStruct((r, c), F32), jax.ShapeDtypeStruct((r, ql), BF), jax.ShapeDtypeStruct((r, kvl), BF)],
        compiler_params=_cp("parallel"), name=name,
    )(p, qa_g, kva_g)


def _even_mid_bwd(p, dgl, dqn, dkvn, dkr, qa_g, kva_g, *, c, tm, name):
    r, w = p.shape
    ql, kvl = qa_g.shape[1], kva_g.shape[1]
    tail = w - (2 * c + ql + kvl + LANES)

    def body(p_ref, dgl_ref, dqn_ref, dkvn_ref, dkr_ref, qg_ref, kg_ref, dp_ref, dqg_ref, dkg_ref):
        i = pl.program_id(0)
        _, vjp = jax.vjp(_even_mid_f, *_even_parts(p_ref[...], c, ql, kvl), qg_ref[...], kg_ref[...])
        dval, dgate, dcq, dckv, dqg, dkg = vjp((dgl_ref[...], dqn_ref[...], dkvn_ref[...]))
        parts = [dval, dgate, dcq, dckv, dkr_ref[...]]
        if tail:
            parts.append(jnp.zeros((tm, tail), F32))
        dp_ref[...] = jnp.concatenate(parts, axis=1).astype(BF)
        _acc(dqg_ref, dqg, i == 0)
        _acc(dkg_ref, dkg, i == 0)

    return pl.pallas_call(
        body, grid=(r // tm,),
        in_specs=[_row(tm, w), _row(tm, c), _row(tm, ql), _row(tm, kvl), _row(tm, LANES), _full((1, ql)), _full((1, kvl))],
        out_specs=[_row(tm, w), _full((1, ql)), _full((1, kvl))],
        out_shape=[jax.ShapeDtypeStruct((r, w), BF), jax.ShapeDtypeStruct((1, ql), F32), jax.ShapeDtypeStruct((1, kvl), F32)],
        compiler_params=_cp("arbitrary"), name=name,
    )(p, dgl, dqn, dkvn, dkr, qa_g, kva_g)


def _halo_specs(tm, tc, total_rows, col_off=0, row_off=0):
    hb = HALO_ROWS
    nb = total_rows // hb
    cur = pl.BlockSpec((tm, tc), lambda j, i: (i + row_off // tm, j + col_off))
    prev = pl.BlockSpec((hb, tc), lambda j, i: (jnp.maximum((i * tm + row_off) // hb - 1, 0), j + col_off))
    nxt = pl.BlockSpec((hb, tc), lambda j, i: (jnp.minimum(((i + 1) * tm + row_off) // hb, nb - 1), j + col_off))
    return [prev, cur, nxt]


def _ext(prev_ref, cur_ref, next_ref, i, n):
    prev = jnp.where(i == 0, 0.0, prev_ref[...].astype(F32))
    nxt = jnp.where(i == n - 1, 0.0, next_ref[...].astype(F32))
    return jnp.concatenate([prev, cur_ref[...].astype(F32), nxt], axis=0)


def _sroll(x, shift):
    return pltpu.roll(x, shift % x.shape[0], 0)


def _conv_taps(e, w_ref, ksize, sign):
    pad = (ksize - 1) // 2
    out = None
    for k in range(ksize):
        t = w_ref[k:k + 1, :] * _sroll(e, -sign * (k - pad))
        out = t if out is None else out + t
    return out


def _core(e, tm):
    return e[HALO_ROWS:HALO_ROWS + tm]


def _ln_silu_f(u, g, b):
    mu = jnp.mean(u, axis=-1, keepdims=True)
    xc = u - mu
    y = xc * lax.rsqrt(jnp.mean(xc * xc, axis=-1, keepdims=True) + EPS)
    return jax.nn.silu(y * g + b)


def _conv_ln_silu(gl_all, w, b, ln_g, ln_b, *, s, row_off, tm, name):
    c = gl_all.shape[1]
    ksize = w.shape[0]
    n = s // tm

    def body(p_ref, c_ref, n_ref, w_ref, b_ref, g_ref, lb_ref, u_ref, a_ref):
        i = pl.program_id(1)
        e = _ext(p_ref, c_ref, n_ref, i, n)
        u = _core(_conv_taps(e, w_ref, ksize, 1), tm) + b_ref[...]
        u_ref[...] = u
        a_ref[...] = _ln_silu_f(u, g_ref[...], lb_ref[...]).astype(BF)

    out = pl.BlockSpec((tm, c), lambda j, i: (i, 0))
    return pl.pallas_call(
        body, grid=(1, n),
        in_specs=_halo_specs(tm, c, gl_all.shape[0], row_off=row_off) + [_full((ksize, c)), _full((1, c)), _full((1, c)), _full((1, c))],
        out_specs=[out, out], out_shape=[jax.ShapeDtypeStruct((s, c), F32), jax.ShapeDtypeStruct((s, c), BF)],
        compiler_params=_cp("parallel", "parallel"), name=name,
    )(gl_all, gl_all, gl_all, w, b, ln_g, ln_b)


def _ln_silu_bwd(u, dcat, ln_g, ln_b, *, tm, name):
    s, c = u.shape
    wc = dcat.shape[1]

    def body(u_ref, dc_ref, g_ref, b_ref, du_ref, dg_ref, db_ref):
        i = pl.program_id(0)
        _, vjp = jax.vjp(_ln_silu_f, u_ref[...], g_ref[...], b_ref[...])
        du, dg, db = vjp(dc_ref[:, :c])
        du_ref[...] = du
        _acc(dg_ref, dg, i == 0)
        _acc(db_ref, db, i == 0)

    return pl.pallas_call(
        body, grid=(s // tm,), in_specs=[_row(tm, c), _row(tm, wc), _full((1, c)), _full((1, c))],
        out_specs=[_row(tm, c), _full((1, c)), _full((1, c))],
        out_shape=[jax.ShapeDtypeStruct((s, c), F32), jax.ShapeDtypeStruct((1, c), F32), jax.ShapeDtypeStruct((1, c), F32)],
        compiler_params=_cp("arbitrary"), name=name,
    )(u, dcat, ln_g, ln_b)


def _dw_taps(dw_ref, g_core, e, ksize, tm, first):
    pad = (ksize - 1) // 2

    @pl.when(first)
    def _():
        dw_ref[...] = jnp.zeros_like(dw_ref)

    for k in range(ksize):
        dw_ref[k:k + 1, :] += jnp.sum(g_core * _core(_sroll(e, -(k - pad)), tm), axis=0, keepdims=True)


def _conv_bwd(du, gl_all, w, *, row_off, tm, name):
    s, c = du.shape
    ksize = w.shape[0]
    n = s // tm

    def body(dp_ref, dc_ref, dn_ref, gp_ref, gc_ref, gn_ref, w_ref, dgl_ref, dw_ref, db_ref):
        i = pl.program_id(1)
        de = _ext(dp_ref, dc_ref, dn_ref, i, n)
        ge = _ext(gp_ref, gc_ref, gn_ref, i, n)
        dgl_ref[...] = _core(_conv_taps(de, w_ref, ksize, -1), tm)
        du_core = dc_ref[...]
        _dw_taps(dw_ref, du_core, ge, ksize, tm, i == 0)
        _acc(db_ref, jnp.sum(du_core, axis=0, keepdims=True), i == 0)

    return pl.pallas_call(
        body, grid=(1, n),
        in_specs=_halo_specs(tm, c, s) + _halo_specs(tm, c, gl_all.shape[0], row_off=row_off) + [_full((ksize, c))],
        out_specs=[pl.BlockSpec((tm, c), lambda j, i: (i, 0)), _full((ksize, c)), _full((1, c))],
        out_shape=[jax.ShapeDtypeStruct((s, c), F32), jax.ShapeDtypeStruct((ksize, c), F32), jax.ShapeDtypeStruct((1, c), F32)],
        compiler_params=_cp("arbitrary", "arbitrary"), name=name,
    )(du, du, du, gl_all, gl_all, gl_all, w)


def _ffn_mid(u, w, b, *, tm, name):
    s, f2 = u.shape
    f = f2 // 2
    ksize = w.shape[0]
    tc = _div(f, 1408)
    nc, n = f // tc, s // tm

    def body(p_ref, c_ref, n_ref, v_ref, w_ref, b_ref, z_ref):
        i = pl.program_id(1)
        e = _ext(p_ref, c_ref, n_ref, i, n)
        cg = _core(_conv_taps(e, w_ref, ksize, 1), tm) + b_ref[...]
        z_ref[...] = (jax.nn.silu(cg) * v_ref[...]).astype(BF)

    cb = lambda j, i: (0, j)
    return pl.pallas_call(
        body, grid=(nc, n),
        in_specs=_halo_specs(tm, tc, s) + [pl.BlockSpec((tm, tc), lambda j, i: (i, j + nc)), pl.BlockSpec((ksize, tc), cb), pl.BlockSpec((1, tc), cb)],
        out_specs=pl.BlockSpec((tm, tc), lambda j, i: (i, j)), out_shape=jax.ShapeDtypeStruct((s, f), BF),
        compiler_params=_cp("parallel", "parallel"), name=name,
    )(u, u, u, u, w, b)


def _ffn_mid_bwd(u, dz, w, b, *, tm, name):
    s, f2 = u.shape
    f = f2 // 2
    ksize = w.shape[0]
    tc = _div(f, 1408)
    nc, n = f // tc, s // tm

    def body(gp_ref, gc_ref, gn_ref, vp_ref, vc_ref, vn_ref, zp_ref, zc_ref, zn_ref, w_ref, b_ref, dg_ref, dv_ref, dw_ref, db_ref):
        i = pl.program_id(1)
        ge = _ext(gp_ref, gc_ref, gn_ref, i, n)
        ve = _ext(vp_ref, vc_ref, vn_ref, i, n)
        ze = _ext(zp_ref, zc_ref, zn_ref, i, n)
        cg = _conv_taps(ge, w_ref, ksize, 1) + b_ref[...]
        sg = jax.nn.sigmoid(cg)
        dcg = ze * ve * (sg * (1.0 + cg * (1.0 - sg)))
        dg_ref[...] = _core(_conv_taps(dcg, w_ref, ksize, -1), tm).astype(BF)
        dv_ref[...] = _core(ze * cg * sg, tm).astype(BF)
        dcg_core = _core(dcg, tm)
        _dw_taps(dw_ref, dcg_core, ge, ksize, tm, i == 0)
        _acc(db_ref, jnp.sum(dcg_core, axis=0, keepdims=True), i == 0)

    cb = lambda j, i: (0, j)
    return pl.pallas_call(
        body, grid=(nc, n),
        in_specs=_halo_specs(tm, tc, s) + _halo_specs(tm, tc, s, col_off=nc) + _halo_specs(tm, tc, s) + [pl.BlockSpec((ksize, tc), cb), pl.BlockSpec((1, tc), cb)],
        out_specs=[pl.BlockSpec((tm, tc), lambda j, i: (i, j)), pl.BlockSpec((tm, tc), lambda j, i: (i, j)),
                   pl.BlockSpec((ksize, tc), cb), pl.BlockSpec((1, tc), cb)],
        out_shape=[jax.ShapeDtypeStruct((s, f), BF), jax.ShapeDtypeStruct((s, f), BF),
                   jax.ShapeDtypeStruct((ksize, f), F32), jax.ShapeDtypeStruct((1, f), F32)],
        compiler_params=_cp("parallel", "arbitrary"), name=name,
    )(u, u, u, u, u, u, dz, dz, dz, w, b)


def _odd_mid(r3, w, b, *, tm, name):
    s, d3 = r3.shape
    d = d3 // 3
    ksize = w.shape[0]
    tc = _div(d, 1024)
    nc, n = d // tc, s // tm

    def body(bg_ref, cp_ref, cc_ref, cn_ref, up_ref, uc_ref, un_ref, w_ref, b_ref, m_ref):
        i = pl.program_id(1)
        pe = _ext(cp_ref, cc_ref, cn_ref, i, n) * _ext(up_ref, uc_ref, un_ref, i, n)
        sv = _core(_conv_taps(pe, w_ref, ksize, 1), tm) + b_ref[...]
        m_ref[...] = (bg_ref[...] * sv).astype(BF)

    cb = lambda j, i: (0, j)
    return pl.pallas_call(
        body, grid=(nc, n),
        in_specs=[pl.BlockSpec((tm, tc), lambda j, i: (i, j))] + _halo_specs(tm, tc, s, col_off=nc) + _halo_specs(tm, tc, s, col_off=2 * nc)
        + [pl.BlockSpec((ksize, tc), cb), pl.BlockSpec((1, tc), cb)],
        out_specs=pl.BlockSpec((tm, tc), lambda j, i: (i, j)), out_shape=jax.ShapeDtypeStruct((s, d), BF),
        compiler_params=_cp("parallel", "parallel"), name=name,
    )(r3, r3, r3, r3, r3, r3, r3, w, b)


def _odd_mid_bwd(r3, dm, w, b, *, tm, name):
    s, d3 = r3.shape
    d = d3 // 3
    ksize = w.shape[0]
    tc = _div(d, 1024)
    nc, n = d // tc, s // tm

    def body(bp_ref, bc_ref, bn_ref, cp_ref, cc_ref, cn_ref, up_ref, uc_ref, un_ref, mp_ref, mc_ref, mn_ref, w_ref, b_ref,
             dbg_ref, dcg_ref, du_ref, dw_ref, db_ref):
        i = pl.program_id(1)
        be = _ext(bp_ref, bc_ref, bn_ref, i, n)
        ce = _ext(cp_ref, cc_ref, cn_ref, i, n)
        ue = _ext(up_ref, uc_ref, un_ref, i, n)
        me = _ext(mp_ref, mc_ref, mn_ref, i, n)
        pe = ce * ue
        sv = _conv_taps(pe, w_ref, ksize, 1) + b_ref[...]
        ds = me * be
        dp = _core(_conv_taps(ds, w_ref, ksize, -1), tm)
        dbg_ref[...] = _core(me * sv, tm).astype(BF)
        dcg_ref[...] = (dp * uc_ref[...]).astype(BF)
        du_ref[...] = (dp * cc_ref[...]).astype(BF)
        ds_core = _core(ds, tm)
        _dw_taps(dw_ref, ds_core, pe, ksize, tm, i == 0)
        _acc(db_ref, jnp.sum(ds_core, axis=0, keepdims=True), i == 0)

    cb = lambda j, i: (0, j)
    col = pl.BlockSpec((tm, tc), lambda j, i: (i, j))
    return pl.pallas_call(
        body, grid=(nc, n),
        in_specs=_halo_specs(tm, tc, s) + _halo_specs(tm, tc, s, col_off=nc) + _halo_specs(tm, tc, s, col_off=2 * nc) + _halo_specs(tm, tc, s)
        + [pl.BlockSpec((ksize, tc), cb), pl.BlockSpec((1, tc), cb)],
        out_specs=[col, col, col, pl.BlockSpec((ksize, tc), cb), pl.BlockSpec((1, tc), cb)],
        out_shape=[jax.ShapeDtypeStruct((s, d), BF)] * 3 + [jax.ShapeDtypeStruct((ksize, d), F32), jax.ShapeDtypeStruct((1, d), F32)],
        compiler_params=_cp("parallel", "arbitrary"), name=name,
    )(r3, r3, r3, r3, r3, r3, r3, r3, r3, dm, dm, dm, w, b)


def _gate_bwd(dx, f, g, *, tm, name):
    s, d = dx.shape

    def body(dx_ref, f_ref, g_ref, df_ref, dg_ref):
        i = pl.program_id(0)
        dxv = dx_ref[...]
        df_ref[...] = (dxv * g_ref[...]).astype(BF)
        _acc(dg_ref, jnp.sum(dxv * f_ref[...], axis=0, keepdims=True), i == 0)

    return pl.pallas_call(
        body, grid=(s // tm,), in_specs=[_row(tm, d), _row(tm, d), _full((1, d))], out_specs=[_row(tm, d), _full((1, d))],
        out_shape=[jax.ShapeDtypeStruct((s, d), BF), jax.ShapeDtypeStruct((1, d), F32)], compiler_params=_cp("arbitrary"), name=name,
    )(dx, f, g)


def _loss_head(y, target, *, tm, name):
    s, d = y.shape

    def body(y_ref, t_ref, dy_ref, l_ref):
        i = pl.program_id(0)
        err = y_ref[...] - t_ref[...]
        dy_ref[...] = err * (1.0 / d)
        _acc(l_ref, jnp.full((1, LANES), 0.5 / d, F32) * jnp.sum(err * err), i == 0)

    return pl.pallas_call(
        body, grid=(s // tm,), in_specs=[_row(tm, d), _row(tm, d)], out_specs=[_row(tm, d), _full((1, LANES))],
        out_shape=[jax.ShapeDtypeStruct((s, d), F32), jax.ShapeDtypeStruct((1, LANES), F32)], compiler_params=_cp("arbitrary"), name=name,
    )(y, target)


def _rope(n, cos, sa, sb):
    return n * cos + _lane_roll(LANES - 8)(n) * sa + _lane_roll(8)(n) * sb


def _q_post_f(q, g, cos, sa, sb):
    outs = []
    for h in range(q.shape[1] // LANES):
        outs.append(_rope(_rms(q[:, h * LANES:(h + 1) * LANES], QK_DIM) * g, cos, sa, sb))
    return jnp.concatenate(outs, axis=1)


def _kv_post_f(kv, kr, g, cos, sa, sb):
    lane = lax.broadcasted_iota(jnp.int32, (1, LANES), 1)
    krs = jnp.where((lane >= QK_NOPE) & (lane < QK_DIM), _lane_roll(QK_NOPE)(kr), 0.0)
    ks, vs = [], []
    for h in range(kv.shape[1] // LANES):
        kvh = kv[:, h * LANES:(h + 1) * LANES]
        ks.append(_rope(_rms(jnp.where(lane < QK_NOPE, kvh, krs), QK_DIM) * g, cos, sa, sb))
        vs.append(jnp.where(lane < V_DIM, _lane_roll(LANES - QK_NOPE)(kvh), 0.0))
    return jnp.concatenate(ks, axis=1), jnp.concatenate(vs, axis=1)


def _tab_specs(tm):
    return [_row(tm, LANES)] * 3


def _q_post(q, g, tabs, *, tm, name):
    r, w = q.shape

    def body(q_ref, g_ref, c_ref, sa_ref, sb_ref, o_ref):
        o_ref[...] = _q_post_f(q_ref[...], g_ref[...], c_ref[...], sa_ref[...], sb_ref[...]).astype(BF)

    return pl.pallas_call(
        body, grid=(r // tm,), in_specs=[_row(tm, w), _full((1, LANES))] + _tab_specs(tm), out_specs=_row(tm, w),
        out_shape=jax.ShapeDtypeStruct((r, w), BF), compiler_params=_cp("parallel"), name=name,
    )(q, g, *tabs)


def _q_post_bwd(q, dqh, g, tabs, *, tm, name):
    r, w = q.shape

    def body(q_ref, d_ref, g_ref, c_ref, sa_ref, sb_ref, dq_ref, dg_ref):
        i = pl.program_id(0)
        f = lambda qv, gv: _q_post_f(qv, gv, c_ref[...], sa_ref[...], sb_ref[...])
        _, vjp = jax.vjp(f, q_ref[...], g_ref[...])
        dq, dg = vjp(d_ref[...])
        dq_ref[...] = dq.astype(BF)
        _acc(dg_ref, dg, i == 0)

    return pl.pallas_call(
        body, grid=(r // tm,), in_specs=[_row(tm, w), _row(tm, w), _full((1, LANES))] + _tab_specs(tm),
        out_specs=[_row(tm, w), _full((1, LANES))],
        out_shape=[jax.ShapeDtypeStruct((r, w), BF), jax.ShapeDtypeStruct((1, LANES), F32)], compiler_params=_cp("arbitrary"), name=name,
    )(q, dqh, g, *tabs)


def _kr_spec(tm, w):
    return pl.BlockSpec((tm, LANES), lambda i: (i, w // LANES - 1))


def _kv_post(kv, p, g, tabs, *, tm, name):
    r, w = kv.shape

    def body(kv_ref, kr_ref, g_ref, c_ref, sa_ref, sb_ref, k_ref, v_ref):
        k, v = _kv_post_f(kv_ref[...], kr_ref[...], g_ref[...], c_ref[...], sa_ref[...], sb_ref[...])
        k_ref[...] = k.astype(BF)
        v_ref[...] = v.astype(BF)

    return pl.pallas_call(
        body, grid=(r // tm,), in_specs=[_row(tm, w), _kr_spec(tm, p.shape[1]), _full((1, LANES))] + _tab_specs(tm),
        out_specs=[_row(tm, w), _row(tm, w)], out_shape=[jax.ShapeDtypeStruct((r, w), BF)] * 2,
        compiler_params=_cp("parallel"), name=name,
    )(kv, p, g, *tabs)


def _kv_post_bwd(kv, p, dk, dv, g, tabs, *, tm, name):
    r, w = kv.shape

    def body(kv_ref, kr_ref, dk_ref, dv_ref, g_ref, c_ref, sa_ref, sb_ref, dkv_ref, dkr_ref, dg_ref):
        i = pl.program_id(0)
        f = lambda kvv, krv, gv: _kv_post_f(kvv, krv, gv, c_ref[...], sa_ref[...], sb_ref[...])
        _, vjp = jax.vjp(f, kv_ref[...], kr_ref[...], g_ref[...])
        dkv, dkr, dg = vjp((dk_ref[...], dv_ref[...]))
        dkv_ref[...] = dkv.astype(BF)
        dkr_ref[...] = dkr
        _acc(dg_ref, dg, i == 0)

    return pl.pallas_call(
        body, grid=(r // tm,),
        in_specs=[_row(tm, w), _kr_spec(tm, p.shape[1]), _row(tm, w), _row(tm, w), _full((1, LANES))] + _tab_specs(tm),
        out_specs=[_row(tm, w), _row(tm, LANES), _full((1, LANES))],
        out_shape=[jax.ShapeDtypeStruct((r, w), BF), jax.ShapeDtypeStruct((r, LANES), F32), jax.ShapeDtypeStruct((1, LANES), F32)],
        compiler_params=_cp("arbitrary"), name=name,
    )(kv, p, dk, dv, g, *tabs)


_NT = (((1,), (1,)), ((), ()))
_TN = (((0,), (0,)), ((), ()))
_NN = (((1,), (0,)), ((), ()))


def _attn_tiles(s, nk):
    return _div(s, 512), _div(nk, 768)


def _attn_fwd(q, k, v, *, name):
    s, w = q.shape
    nk = k.shape[0]
    tq, tk = _attn_tiles(s, nk)
    nj = nk // tk

    def body(q_ref, k_ref, v_ref, o_ref, lse_ref, m_s, l_s, acc_s):
        j = pl.program_id(2)

        @pl.when(j == 0)
        def _():
            m_s[...] = jnp.full_like(m_s, -jnp.inf)
            l_s[...] = jnp.zeros_like(l_s)
            acc_s[...] = jnp.zeros_like(acc_s)

        sc = lax.dot_general(q_ref[...], k_ref[...], _NT, preferred_element_type=F32) * SM_SCALE
        m_prev = m_s[...]
        m_new = jnp.maximum(m_prev, jnp.max(sc, axis=1, keepdims=True))
        alpha = jnp.exp(m_prev - m_new)
        pr = jnp.exp(sc - m_new[:, :1])
        l_s[...] = alpha * l_s[...] + jnp.sum(pr, axis=1, keepdims=True)
        acc_s[...] = alpha * acc_s[...] + lax.dot_general(pr.astype(BF), v_ref[...], _NN, preferred_element_type=F32)
        m_s[...] = m_new

        @pl.when(j == nj - 1)
        def _():
            o_ref[...] = (acc_s[...] / l_s[...]).astype(BF)
            lse_ref[...] = m_s[...] + jnp.log(l_s[...])

    qs = pl.BlockSpec((tq, LANES), lambda h, i, j: (i, h))
    ks = pl.BlockSpec((tk, LANES), lambda h, i, j: (j, h))
    return pl.pallas_call(
        body, grid=(w // LANES, s // tq, nj), in_specs=[qs, ks, ks], out_specs=[qs, qs],
        out_shape=[jax.ShapeDtypeStruct((s, w), BF), jax.ShapeDtypeStruct((s, w), F32)],
        scratch_shapes=[pltpu.VMEM((tq, LANES), F32)] * 3, compiler_params=_cp("parallel", "parallel", "arbitrary"), name=name,
    )(q, k, v)


def _attn_delta(dcat, o, *, tm, name):
    s, w = o.shape
    wc = dcat.shape[1]

    def body(dc_ref, o_ref, do_ref, dl_ref):
        do = dc_ref[:, wc - w:]
        prod = do * o_ref[...].astype(F32)
        outs = []
        for h in range(w // LANES):
            outs.append(jnp.broadcast_to(jnp.sum(prod[:, h * LANES:(h + 1) * LANES], axis=1, keepdims=True), (tm, LANES)))
        do_ref[...] = do.astype(BF)
        dl_ref[...] = jnp.concatenate(outs, axis=1)

    return pl.pallas_call(
        body, grid=(s // tm,), in_specs=[_row(tm, wc), _row(tm, w)], out_specs=[_row(tm, w), _row(tm, w)],
        out_shape=[jax.ShapeDtypeStruct((s, w), BF), jax.ShapeDtypeStruct((s, w), F32)], compiler_params=_cp("parallel"), name=name,
    )(dcat, o)


def _attn_ds(q_ref, k_ref, v_ref, do_ref, lse_ref, dl_ref):
    sc = lax.dot_general(q_ref[...], k_ref[...], _NT, preferred_element_type=F32) * SM_SCALE
    pr = jnp.exp(sc - lse_ref[:, :1])
    dp = lax.dot_general(do_ref[...], v_ref[...], _NT, preferred_element_type=F32)
    return pr, pr * (dp - dl_ref[:, :1]) * SM_SCALE


def _attn_bwd_kv(q, k, v, do, lse, delta, *, name):
    s, w = q.shape
    nk = k.shape[0]
    tq, tk = _attn_tiles(s, nk)
    ni = s // tq

    def body(q_ref, k_ref, v_ref, do_ref, lse_ref, dl_ref, dk_ref, dv_ref, dk_s, dv_s):
        i = pl.program_id(2)

        @pl.when(i == 0)
        def _():
            dk_s[...] = jnp.zeros_like(dk_s)
            dv_s[...] = jnp.zeros_like(dv_s)

        pr, ds = _attn_ds(q_ref, k_ref, v_ref, do_ref, lse_ref, dl_ref)
        dv_s[...] += lax.dot_general(pr.astype(BF), do_ref[...], _TN, preferred_element_type=F32)
        dk_s[...] += lax.dot_general(ds.astype(BF), q_ref[...], _TN, preferred_element_type=F32)

        @pl.when(i == ni - 1)
        def _():
            dk_ref[...] = dk_s[...]
            dv_ref[...] = dv_s[...]

    qs = pl.BlockSpec((tq, LANES), lambda h, j, i: (i, h))
    ks = pl.BlockSpec((tk, LANES), lambda h, j, i: (j, h))
    return pl.pallas_call(
        body, grid=(w // LANES, nk // tk, ni), in_specs=[qs, ks, ks, qs, qs, qs], out_specs=[ks, ks],
        out_shape=[jax.ShapeDtypeStruct((nk, w), F32)] * 2, scratch_shapes=[pltpu.VMEM((tk, LANES), F32)] * 2,
        compiler_params=_cp("parallel", "parallel", "arbitrary"), name=name,
    )(q, k, v, do, lse, delta)


def _attn_bwd_q(q, k, v, do, lse, delta, *, name):
    s, w = q.shape
    nk = k.shape[0]
    tq, tk = _attn_tiles(s, nk)
    nj = nk // tk

    def body(q_ref, k_ref, v_ref, do_ref, lse_ref, dl_ref, dq_ref, dq_s):
        j = pl.program_id(2)

        @pl.when(j == 0)
        def _():
            dq_s[...] = jnp.zeros_like(dq_s)

        _, ds = _attn_ds(q_ref, k_ref, v_ref, do_ref, lse_ref, dl_ref)
        dq_s[...] += lax.dot_general(ds.astype(BF), k_ref[...], _NN, preferred_element_type=F32)

        @pl.when(j == nj - 1)
        def _():
            dq_ref[...] = dq_s[...]

    qs = pl.BlockSpec((tq, LANES), lambda h, i, j: (i, h))
    ks = pl.BlockSpec((tk, LANES), lambda h, i, j: (j, h))
    return pl.pallas_call(
        body, grid=(w // LANES, s // tq, nj), in_specs=[qs, ks, ks, qs, qs, qs], out_specs=qs,
        out_shape=jax.ShapeDtypeStruct((s, w), F32), scratch_shapes=[pltpu.VMEM((tq, LANES), F32)],
        compiler_params=_cp("parallel", "parallel", "arbitrary"), name=name,
    )(q, k, v, do, lse, delta)


def _adamw(w, g, m, v, *, name):
    r, c = w.shape
    tr = r
    for cand in (512, 256, 128, 64, 32, 16, 8):
        if r % cand == 0 and cand * c * 4 <= (2 << 20):
            tr = cand
            break
    bc1 = 1.0 - ADAM_B1 ** ADAM_STEP
    bc2 = 1.0 - ADAM_B2 ** ADAM_STEP

    def body(w_ref, g_ref, m_ref, v_ref, d_ref, nm_ref, nv_ref):
        gv = g_ref[...]
        nm = ADAM_B1 * m_ref[...] + (1.0 - ADAM_B1) * gv
        nv = ADAM_B2 * v_ref[...] + (1.0 - ADAM_B2) * (gv * gv)
        d_ref[...] = -ADAM_LR * ((nm / bc1) / (jnp.sqrt(nv / bc2) + ADAM_EPS) + ADAM_WD * w_ref[...])
        nm_ref[...] = nm
        nv_ref[...] = nv

    spec = _row(tr, c)
    return pl.pallas_call(
        body, grid=(r // tr,), in_specs=[spec] * 4, out_specs=[spec] * 3,
        out_shape=[jax.ShapeDtypeStruct((r, c), F32)] * 3, compiler_params=_cp("parallel"), name=name,
    )(w, g, m, v)


def _mesh_pos():
    return lax.axis_index("x"), lax.axis_index("y"), lax.axis_index("c")


def _allgather(x, *, in_vmem, name):
    r, c = x.shape
    spec = pl.BlockSpec(memory_space=pltpu.VMEM if in_vmem else pl.ANY)

    def body(x_ref, out_ref, send_sems, recv_sems, local_sem):
        ix, iy, ic = _mesh_pos()
        me, sibling = (ix, iy, ic), (ix, iy, 1 - ic)
        chips = [(1 - ix, iy), (ix, 1 - iy), (1 - ix, 1 - iy)]

        def slab(px, py, pc):
            return out_ref.at[4 * px + 2 * py + pc]

        def copy(k, block, to, src=None):
            return pltpu.make_async_remote_copy(
                src_ref=slab(*block) if src is None else src, dst_ref=slab(*block),
                send_sem=send_sems.at[k], recv_sem=recv_sems.at[k], device_id=to, device_id_type=pl.DeviceIdType.MESH)

        mine = pltpu.make_async_copy(x_ref, slab(*me), local_sem)
        mine.start()
        first = [copy(0, me, sibling, src=x_ref)]
        first += [copy(1 + j, me, (*chip, ic), src=x_ref) for j, chip in enumerate(chips)]
        for cp in first:
            cp.start()
        passed = [copy(4 + j, (*chip, ic), sibling) for j, chip in enumerate(chips)]
        for j, chip in enumerate(chips):
            copy(1 + j, (*chip, ic), me).wait_recv()
            passed[j].start()
        copy(0, sibling, me).wait_recv()
        for j, chip in enumerate(chips):
            copy(4 + j, (*chip, 1 - ic), me).wait_recv()
        for cp in first + passed:
            cp.wait_send()
        mine.wait()

    return pl.pallas_call(
        body, out_shape=jax.ShapeDtypeStruct((N_DEV, r, c), x.dtype), in_specs=[spec], out_specs=spec,
        scratch_shapes=[pltpu.SemaphoreType.DMA((7,)), pltpu.SemaphoreType.DMA((7,)), pltpu.SemaphoreType.DMA], name=name,
    )(x)


def _exchange(g, *, name):
    _, r, c = g.shape
    spec = pl.BlockSpec(memory_space=pl.ANY)

    def body(g_ref, out_ref, send_sems, recv_sems, local_sem):
        ix, iy, ic = _mesh_pos()
        me = 4 * ix + 2 * iy + ic
        mine = pltpu.make_async_copy(g_ref.at[me], out_ref.at[me], local_sem)
        mine.start()
        sends, recvs = [], []
        for k in range(1, N_DEV):
            px = 1 - ix if k & 4 else ix
            py = 1 - iy if k & 2 else iy
            pc = 1 - ic if k & 1 else ic
            peer = 4 * px + 2 * py + pc
            mk = lambda src, dst: pltpu.make_async_remote_copy(
                src_ref=g_ref.at[src], dst_ref=out_ref.at[dst], send_sem=send_sems.at[k - 1], recv_sem=recv_sems.at[k - 1],
                device_id=(px, py, pc), device_id_type=pl.DeviceIdType.MESH)
            sends.append(mk(peer, me))
            recvs.append(mk(me, peer))
        for cp in sends:
            cp.start()
        for cp in recvs:
            cp.wait_recv()
        for cp in sends:
            cp.wait_send()
        mine.wait()

    return pl.pallas_call(
        body, out_shape=jax.ShapeDtypeStruct(g.shape, g.dtype), in_specs=[spec], out_specs=spec,
        scratch_shapes=[pltpu.SemaphoreType.DMA((7,)), pltpu.SemaphoreType.DMA((7,)), pltpu.SemaphoreType.DMA], name=name,
    )(g)


def _sum8(a, *, name):
    _, r, c = a.shape
    tr = r
    for cand in (512, 256, 128, 64, 32, 16):
        if r % cand == 0 and cand * c * 4 <= (1 << 20):
            tr = cand
            break

    def body(a_ref, o_ref):
        acc = a_ref[0].astype(F32)
        for d in range(1, N_DEV):
            acc = acc + a_ref[d].astype(F32)
        o_ref[...] = acc

    return pl.pallas_call(
        body, grid=(r // tr,), in_specs=[pl.BlockSpec((N_DEV, tr, c), lambda i: (0, i, 0))], out_specs=_row(tr, c),
        out_shape=jax.ShapeDtypeStruct((r, c), F32), compiler_params=_cp("parallel"), name=name,
    )(a)


ADA_ROWS = 16


def _silu_rows(c8, c_ctx):
    d = c8.shape[1]
    rows = jnp.concatenate([c8, c_ctx, jnp.zeros((ADA_ROWS - N_DEV - 1, d), F32)], axis=0)
    return jax.nn.silu(rows)


def _ada_fwd(c8, c_ctx, ada_w, ada_b_cols, *, name):
    nl, d, cols = ada_w.shape

    def body(c8_ref, cc_ref, w_ref, b_ref, o_ref):
        sc = _silu_rows(c8_ref[...], cc_ref[...]).astype(BF)
        for l in range(nl):
            o_ref[l] = lax.dot_general(sc, w_ref[l].astype(BF), _NN, preferred_element_type=F32) + b_ref[l:l + 1, :]

    return pl.pallas_call(
        body, out_shape=jax.ShapeDtypeStruct((nl, ADA_ROWS, cols), F32),
        compiler_params=pltpu.CompilerParams(vmem_limit_bytes=VMEM_LIMIT), name=name,
    )(c8, c_ctx, ada_w, ada_b_cols)


def _ada_bwd(c8, c_ctx, ada_w, g16, dctx_cols, tot_dm, dmodc_pad, *, name):
    nl, d, cols = ada_w.shape
    hi = lax.Precision.HIGHEST

    def body(c8_ref, cc_ref, w_ref, g_ref, dc_ref, tot_ref, dmc_ref, dw_ref, db_ref, part_ref):
        sc = _silu_rows(c8_ref[...], cc_ref[...])
        for l in range(nl):
            dw_ref[l] = lax.dot_general(sc, g_ref[l], _TN, precision=hi, preferred_element_type=F32)
        db_ref[...] = tot_ref[...]
        db_ref[0:1, :] += dmc_ref[...]
        ccv = cc_ref[...]
        sg = jax.nn.sigmoid(ccv)
        dsilu = sg * (1.0 + ccv * (1.0 - sg))
        part = lax.dot_general(dc_ref[...], w_ref[0], _NT, precision=hi, preferred_element_type=F32) * dsilu
        part_ref[...] = jnp.concatenate([part, jnp.zeros((SUBLANES - 1, d), F32)], axis=0)

    return pl.pallas_call(
        body, out_shape=[jax.ShapeDtypeStruct((nl, d, cols), F32), jax.ShapeDtypeStruct(tot_dm.shape, F32),
                         jax.ShapeDtypeStruct((SUBLANES, d), F32)],
        compiler_params=pltpu.CompilerParams(vmem_limit_bytes=VMEM_LIMIT), name=name,
    )(c8, c_ctx, ada_w, g16, dctx_cols, tot_dm, dmodc_pad)


def _rope_tables(s, lc):
    t = jnp.arange(s)
    half = QK_ROPE // 2
    inv = ROPE_THETA ** (-jnp.arange(0, half, 2, dtype=F32) / half)
    ang_r = (t // GRID_W).astype(F32)[:, None] * inv[None, :]
    ang_c = (t % GRID_W).astype(F32)[:, None] * inv[None, :]
    ang = jnp.concatenate([ang_r, ang_r, ang_c, ang_c], axis=-1)
    cos, sin = jnp.cos(ang), jnp.sin(ang)
    first = (jnp.arange(QK_ROPE) % half) < half // 2
    sa, sb = jnp.where(first, -sin, 0.0), jnp.where(first, 0.0, sin)

    def slot(mid, fill):
        body = jnp.concatenate([jnp.full((s, QK_NOPE), fill, F32), mid, jnp.full((s, LANES - QK_DIM), fill, F32)], axis=1)
        return jnp.concatenate([jnp.full((lc, LANES), fill, F32), body], axis=0)

    return slot(cos, 1.0), slot(sa, 0.0), slot(sb, 0.0)


def _pad_last(a, n):
    return jnp.pad(a, [(0, 0)] * (a.ndim - 1) + [(0, n - a.shape[-1])])


def _local_step(x, ctx, target, mods, modc, w):
    s, d = x.shape
    lc = ctx.shape[0]
    c = d // 2
    nh = (d - c) // V_DIM
    hw = nh * LANES
    ql, kvl = w["ev_qa_norm_g"].shape[-1], w["ev_kva_norm_g"].shape[-1]
    ei = 2 * c + ql + kvl + QK_ROPE
    eip = 2 * c + ql + kvl + LANES
    tm = 256 if (s % 256 == 0 and lc % 256 == 0) else 128
    nctx = lc // tm
    row = lambda v: v.reshape(1, -1)
    cls1 = lambda v: v.reshape(1, 1, -1)

    w_inp = _pad_last(w["ev_w_in"][0], eip)
    w_uqp = _pad_last(w["ev_w_uq"][0].reshape(ql, nh, QK_DIM), LANES).reshape(ql, hw)
    w_ukv = w["ev_w_ukv"][0]
    w_out = w["ev_w_out"][0]
    w_att = jnp.pad(w_out[c:].reshape(nh, V_DIM, d), [(0, 0), (0, LANES - V_DIM), (0, 0)]).reshape(hw, d)
    w_outp = jnp.concatenate([w_out[:c], w_att], axis=0)
    qg = _pad_last(row(w["ev_q_norm_g"]), LANES)
    kg = _pad_last(row(w["ev_k_norm_g"]), LANES)
    qa_g, kva_g = row(w["ev_qa_norm_g"]), row(w["ev_kva_norm_g"])
    ev_cw, ev_cb = w["ev_conv_w"][0], row(w["ev_conv_b"])
    ln_g, ln_b = row(w["ev_ln_g"]), row(w["ev_ln_b"])
    tabs = _rope_tables(s, lc)

    xall = jnp.concatenate([ctx, x], axis=0)
    sc0 = jnp.stack([modc[1], mods[0, 1]])[:, None, :]
    sh0 = jnp.stack([modc[0], mods[0, 0]])[:, None, :]
    g_mix0 = row(w["norm_mix_g"][0])
    hall = _modnorm(xall, g_mix0, sc0, sh0, nctx=nctx, tm=tm, name="l0_mix_norm")
    p_all = _mm(hall, w_inp, name="l0_w_in")
    gl_all, qn_all, kvn_all = _even_mid(p_all, qa_g, kva_g, c=c, tm=tm, name="l0_even_mid")
    u, a = _conv_ln_silu(gl_all, ev_cw, ev_cb, ln_g, ln_b, s=s, row_off=lc, tm=tm, name="l0_conformer")
    q_all = _mm(qn_all, w_uqp, name="l0_w_uq")
    kv_all = _mm(kvn_all, w_ukv, name="l0_w_ukv")
    qh_all = _q_post(q_all, qg, tabs, tm=tm, name="l0_q_post")
    k_all, v_all = _kv_post(kv_all, p_all, kg, tabs, tm=tm, name="l0_kv_post")
    qh = qh_all[lc:]
    o, lse = _attn_fwd(qh, k_all, v_all, name="l0_attn_fwd")
    cat = jnp.concatenate([a, o], axis=1)
    x1, f_mix0 = _mm(cat, w_outp, res=x, gate=row(mods[0, 2]), name="l0_w_out")

    def ffn_fwd(l, x_in):
        hf = _modnorm(x_in, row(w["norm_ffn_g"][l]), cls1(mods[l, 4]), cls1(mods[l, 3]), nctx=0, tm=tm, name=f"l{l}_ffn_norm")
        uu = _mm(hf, w["ffn_w_up"][l], name=f"l{l}_w_up")
        z = _ffn_mid(uu, w["ffn_conv_w"][l], row(w["ffn_conv_b"][l]), tm=tm, name=f"l{l}_ffn_mid")
        x_out, f = _mm(z, w["ffn_w_down"][l], res=x_in, gate=row(mods[l, 5]), name=f"l{l}_w_down")
        return x_out, (x_in, hf, uu, z, f)

    x2, ffn0 = ffn_fwd(0, x1)
    h1 = _modnorm(x2, row(w["norm_mix_g"][1]), cls1(mods[1, 1]), cls1(mods[1, 0]), nctx=0, tm=tm, name="l1_mix_norm")
    r3 = _mm(h1, w["od_w_in"][0], name="l1_w_in")
    od_cw, od_cb = w["od_conv_w"][0], row(w["od_conv_b"])
    m1 = _odd_mid(r3, od_cw, od_cb, tm=tm, name="l1_odd_mid")
    x3, f_mix1 = _mm(m1, w["od_w_out"][0], res=x2, gate=row(mods[1, 2]), name="l1_w_out")
    x4, ffn1 = ffn_fwd(1, x3)
    dx, loss_row = _loss_head(x4, target, tm=tm, name="loss_head")

    g = {}
    dmods = [[None] * N_MOD for _ in range(2)]

    def ffn_bwd(l, dx, saved):
        x_in, hf, uu, z, f = saved
        df, dmods[l][5] = _gate_bwd(dx, f, row(mods[l, 5]), tm=tm, name=f"l{l}_ffn_gate_bwd")
        dz = _mm(df, w["ffn_w_down"][l], tb=True, name=f"l{l}_w_down_dx")
        dwd = _mm(z, df, ta=True, name=f"l{l}_w_down_dw")
        dgt, dvl, dcw, dcb = _ffn_mid_bwd(uu, dz, w["ffn_conv_w"][l], row(w["ffn_conv_b"][l]), tm=tm, name=f"l{l}_ffn_mid_bwd")
        du = jnp.concatenate([dgt, dvl], axis=1)
        dhf = _mm(du, w["ffn_w_up"][l], tb=True, name=f"l{l}_w_up_dx")
        dwu = _mm(hf, du, ta=True, name=f"l{l}_w_up_dw")
        dx, dgn, dsc, dsh = _modnorm_bwd(x_in, dhf, dx, row(w["norm_ffn_g"][l]), cls1(mods[l, 4]), cls1(mods[l, 3]),
                                         nctx=0, tm=tm, name=f"l{l}_ffn_norm_bwd")
        dmods[l][4], dmods[l][3] = dsc, dsh
        return dx, dict(ffn_w_up=dwu, ffn_w_down=dwd, ffn_conv_w=dcw, ffn_conv_b=dcb, norm_ffn_g=dgn)

    dx, gf1 = ffn_bwd(1, dx, ffn1)
    df, dmods[1][2] = _gate_bwd(dx, f_mix1, row(mods[1, 2]), tm=tm, name="l1_mix_gate_bwd")
    dm1 = _mm(df, w["od_w_out"][0], tb=True, name="l1_w_out_dx")
    g["od_w_out"] = _mm(m1, df, ta=True, name="l1_w_out_dw")[None]
    dbg, dcg, duu, dcw, dcb = _odd_mid_bwd(r3, dm1, od_cw, od_cb, tm=tm, name="l1_odd_mid_bwd")
    g["od_conv_w"], g["od_conv_b"] = dcw[None], dcb
    dr3 = jnp.concatenate([dbg, dcg, duu], axis=1)
    dh1 = _mm(dr3, w["od_w_in"][0], tb=True, name="l1_w_in_dx")
    g["od_w_in"] = _mm(h1, dr3, ta=True, name="l1_w_in_dw")[None]
    dx, dgn1, dsc, dsh = _modnorm_bwd(x2, dh1, dx, row(w["norm_mix_g"][1]), cls1(mods[1, 1]), cls1(mods[1, 0]),
                                      nctx=0, tm=tm, name="l1_mix_norm_bwd")
    dmods[1][1], dmods[1][0] = dsc, dsh
    dx, gf0 = ffn_bwd(0, dx, ffn0)
    df, dmods[0][2] = _gate_bwd(dx, f_mix0, row(mods[0, 2]), tm=tm, name="l0_mix_gate_bwd")
    dcat = _mm(df, w_outp, tb=True, name="l0_w_out_dx")
    dw_outp = _mm(cat, df, ta=True, name="l0_w_out_dw")
    du0, g["ev_ln_g"], g["ev_ln_b"] = _ln_silu_bwd(u, dcat, ln_g, ln_b, tm=tm, name="l0_ln_silu_bwd")
    dgl, dcw, g["ev_conv_b"] = _conv_bwd(du0, gl_all, ev_cw, row_off=lc, tm=tm, name="l0_conformer_conv_bwd")
    g["ev_conv_w"] = dcw[None]
    do, delta = _attn_delta(dcat, o, tm=tm, name="l0_attn_delta")
    dk, dv = _attn_bwd_kv(qh, k_all, v_all, do, lse, delta, name="l0_attn_bwd_kv")
    dq = _attn_bwd_q(qh, k_all, v_all, do, lse, delta, name="l0_attn_bwd_q")
    dq_all = jnp.concatenate([jnp.zeros((lc, hw), F32), dq], axis=0)
    dgl_all = jnp.concatenate([jnp.zeros((lc, c), F32), dgl], axis=0)
    dqp, dqg = _q_post_bwd(q_all, dq_all, qg, tabs, tm=tm, name="l0_q_post_bwd")
    dkvp, dkr, dkg = _kv_post_bwd(kv_all, p_all, dk, dv, kg, tabs, tm=tm, name="l0_kv_post_bwd")
    dqn = _mm(dqp, w_uqp, tb=True, name="l0_w_uq_dx")
    dw_uqp = _mm(qn_all, dqp, ta=True, name="l0_w_uq_dw")
    dkvn = _mm(dkvp, w_ukv, tb=True, name="l0_w_ukv_dx")
    g["ev_w_ukv"] = _mm(kvn_all, dkvp, ta=True, name="l0_w_ukv_dw")[None]
    dp, g["ev_qa_norm_g"], g["ev_kva_norm_g"] = _even_mid_bwd(p_all, dgl_all, dqn, dkvn, dkr, qa_g, kva_g, c=c, tm=tm, name="l0_even_mid_bwd")
    dhall = _mm(dp, w_inp, tb=True, name="l0_w_in_dx")
    dw_inp = _mm(hall, dp, ta=True, name="l0_w_in_dw")
    dx, dgn0, dsc2, dsh2 = _modnorm_bwd(xall, dhall, dx, g_mix0, sc0, sh0, nctx=nctx, tm=tm, name="l0_mix_norm_bwd")
    dmods[0][1], dmods[0][0] = dsc2[1], dsh2[1]
    dmodc = jnp.concatenate([dsh2[0], dsc2[0]], axis=0)

    g["ev_w_in"] = dw_inp[:, :ei][None]
    g["ev_w_uq"] = dw_uqp.reshape(ql, nh, LANES)[:, :, :QK_DIM].reshape(ql, nh * QK_DIM)[None]
    g["ev_w_out"] = jnp.concatenate([dw_outp[:c], dw_outp[c:].reshape(nh, LANES, d)[:, :V_DIM].reshape(nh * V_DIM, d)], axis=0)[None]
    g["ev_q_norm_g"], g["ev_k_norm_g"] = dqg[:, :QK_DIM], dkg[:, :QK_DIM]
    g["norm_mix_g"] = jnp.concatenate([dgn0, dgn1], axis=0)
    for name in ("ffn_w_up", "ffn_w_down", "ffn_conv_w"):
        g[name] = jnp.stack([gf0[name], gf1[name]])
    for name in ("ffn_conv_b", "norm_ffn_g"):
        g[name] = jnp.concatenate([gf0[name], gf1[name]], axis=0)
    dmods_arr = jnp.stack([jnp.concatenate([v.reshape(1, d) for v in dmods[l]], axis=0) for l in range(2)])
    return loss_row, dx, g, dmods_arr, dmodc


WEIGHTS = ("c_ctx", "ada_w", "ada_b", "norm_mix_g", "norm_ffn_g", "ffn_w_up", "ffn_conv_w", "ffn_conv_b", "ffn_w_down", "ev_w_in",
           "ev_conv_w", "ev_conv_b", "ev_ln_g", "ev_ln_b", "ev_qa_norm_g", "ev_w_uq", "ev_kva_norm_g", "ev_w_ukv", "ev_q_norm_g",
           "ev_k_norm_g", "ev_w_out", "od_w_in", "od_conv_w", "od_conv_b", "od_w_out")
SHARD_DIM = dict(ada_w=2, ffn_w_up=2, ffn_conv_w=2, ffn_w_down=1, ev_w_in=2, ev_conv_w=2, ev_w_uq=2, ev_w_ukv=2, ev_w_out=1,
                 od_w_in=2, od_conv_w=2, od_conv_b=1, od_w_out=1)
BIG = ("ffn_w_up", "ffn_w_down", "ev_w_in", "ev_w_uq", "ev_w_ukv", "ev_w_out", "od_w_in", "od_w_out")
SMALL_SHARDED = ("ffn_conv_w", "ev_conv_w", "od_conv_w", "od_conv_b")
SMALL_GRADS = ("norm_mix_g", "norm_ffn_g", "ffn_conv_w", "ffn_conv_b", "ev_conv_w", "ev_conv_b", "ev_ln_g", "ev_ln_b",
               "ev_qa_norm_g", "ev_kva_norm_g", "ev_q_norm_g", "ev_k_norm_g", "od_conv_w", "od_conv_b")
SMALL_ADAM = ("c_ctx", "ada_b") + SMALL_GRADS


def _size(shape):
    n = 1
    for v in shape:
        n *= v
    return n


def _pack(parts, dtype, row_mult, lead=0):
    lead_shape = parts[0].shape[:lead]
    flat = jnp.concatenate([p.astype(dtype).reshape(lead_shape + (-1,)) for p in parts], axis=-1)
    per = PACK_W * row_mult
    total = -(-flat.shape[-1] // per) * per
    flat = jnp.pad(flat, [(0, 0)] * lead + [(0, total - flat.shape[-1])])
    return flat.reshape(lead_shape + (total // PACK_W, PACK_W))


def _unpack(buf, shapes):
    lead_shape = buf.shape[:-2]
    flat = buf.reshape(lead_shape + (-1,))
    out, off = [], 0
    for shp in shapes:
        n = _size(shp)
        out.append(flat[..., off:off + n].reshape(lead_shape + tuple(shp)))
        off += n
    return out


def _unshard(pieces, k):
    t = jnp.moveaxis(pieces, 0, k)
    return t.reshape(t.shape[:k] + (t.shape[k] * t.shape[k + 1],) + t.shape[k + 2:])


def _shard_major(full, k):
    t = full.reshape(full.shape[:k] + (N_DEV, full.shape[k] // N_DEV) + full.shape[k + 1:])
    return jnp.moveaxis(t, k, 0)


def _my_shard(full, k, me):
    n = full.shape[k] // N_DEV
    return lax.dynamic_slice_in_dim(full, me * n, n, axis=k)


def kernel(x, c, ctx, c_ctx, ada_w, ada_b, norm_mix_g, norm_ffn_g, ffn_w_up, ffn_conv_w, ffn_conv_b, ffn_w_down, ev_w_in, ev_conv_w, ev_conv_b, ev_ln_g, ev_ln_b, ev_qa_norm_g, ev_w_uq, ev_kva_norm_g, ev_w_ukv, ev_q_norm_g, ev_k_norm_g, ev_w_out, od_w_in, od_conv_w, od_conv_b, od_w_out, loss_target, m_c_ctx, m_ada_w, m_ada_b, m_norm_mix_g, m_norm_ffn_g, m_ffn_w_up, m_ffn_conv_w, m_ffn_conv_b, m_ffn_w_down, m_ev_w_in, m_ev_conv_w, m_ev_conv_b, m_ev_ln_g, m_ev_ln_b, m_ev_qa_norm_g, m_ev_w_uq, m_ev_kva_norm_g, m_ev_w_ukv, m_ev_q_norm_g, m_ev_k_norm_g, m_ev_w_out, m_od_w_in, m_od_conv_w, m_od_conv_b, m_od_w_out, v_c_ctx, v_ada_w, v_ada_b, v_norm_mix_g, v_norm_ffn_g, v_ffn_w_up, v_ffn_conv_w, v_ffn_conv_b, v_ffn_w_down, v_ev_w_in, v_ev_conv_w, v_ev_conv_b, v_ev_ln_g, v_ev_ln_b, v_ev_qa_norm_g, v_ev_w_uq, v_ev_kva_norm_g, v_ev_w_ukv, v_ev_q_norm_g, v_ev_k_norm_g, v_ev_w_out, v_od_w_in, v_od_conv_w, v_od_conv_b, v_od_w_out):
    a = dict(locals())
    ix, iy, ic = _mesh_pos()
    me = 4 * ix + 2 * iy + ic
    xs, ctxs, target = x[0], ctx[0], loss_target[0]
    d = xs.shape[1]
    nl, _, cols = ada_w.shape

    wall = _allgather(_pack([a[n] for n in BIG], BF, 16), in_vmem=False, name="gather_big_weights")
    w = {n: _unshard(p, SHARD_DIM[n]) for n, p in zip(BIG, _unpack(wall, [a[n].shape for n in BIG]))}
    sall = _allgather(_pack([c] + [a[n] for n in SMALL_SHARDED], F32, SUBLANES), in_vmem=True, name="gather_cond")
    sp = _unpack(sall, [c.shape] + [a[n].shape for n in SMALL_SHARDED])
    c8 = sp[0].reshape(N_DEV, d)
    for n, p in zip(SMALL_SHARDED, sp[1:]):
        w[n] = _unshard(p, SHARD_DIM[n])
    for n in SMALL_GRADS:
        if n not in SMALL_SHARDED:
            w[n] = a[n]

    cc = c_ctx.reshape(1, d)
    mpart = _ada_fwd(c8, cc, ada_w, lax.dynamic_slice_in_dim(ada_b, me * cols, cols, axis=1), name="ada_fwd")
    mall = _allgather(mpart.reshape(nl * ADA_ROWS, cols), in_vmem=True, name="gather_mod").reshape(N_DEV, nl, ADA_ROWS, cols)
    mine = lax.dynamic_index_in_dim(mall, me, axis=2, keepdims=False)
    mods = jnp.transpose(mine, (1, 0, 2)).reshape(nl, N_MOD, d)
    modc = mall[:, 0, N_DEV, :].reshape(-1)[:2 * d].reshape(2, d)

    loss_row, dx, g, dmods, dmodc = _local_step(xs, ctxs, target, mods, modc, w)
    loss = lax.psum(loss_row[0, 0], MESH_AXES)

    gpack = _pack([_shard_major(g[n], SHARD_DIM[n]) for n in BIG], BF, 16, lead=1)
    gsum = _sum8(_exchange(gpack, name="exchange_big_grads"), name="sum_big_grads")
    grads = dict(zip(BIG, _unpack(gsum, [a[n].shape for n in BIG])))

    small_parts = [dmods, dmodc] + [g[n] for n in SMALL_GRADS]
    small_shapes = [p.shape for p in small_parts]
    small = _allgather(_pack(small_parts, F32, SUBLANES), in_vmem=True, name="gather_small_grads")
    tots = _unpack(_sum8(small, name="sum_small_grads"), small_shapes)
    for n, t in zip(SMALL_GRADS, tots[2:]):
        grads[n] = _my_shard(t, SHARD_DIM[n], me) if n in SMALL_SHARDED else t
    dm_all = _unpack(small, small_shapes[:1])[0].reshape(N_DEV, nl, N_MOD * d)
    tot_dm = tots[0].reshape(nl, N_MOD * d)
    dmodc_pad = _pad_last(tots[1].reshape(1, 2 * d), N_MOD * d)
    dm_cols = lax.dynamic_slice_in_dim(dm_all, me * cols, cols, axis=2)
    dctx_cols = lax.dynamic_slice_in_dim(dmodc_pad, me * cols, cols, axis=1)
    ctx_rows = jnp.concatenate([dctx_cols[None], jnp.zeros((nl - 1, 1, cols), F32)], axis=0)
    g16 = jnp.concatenate([jnp.transpose(dm_cols, (1, 0, 2)), ctx_rows, jnp.zeros((nl, ADA_ROWS - N_DEV - 1, cols), F32)], axis=1)
    grads["ada_w"], grads["ada_b"], cpart = _ada_bwd(c8, cc, ada_w, g16, dctx_cols, tot_dm, dmodc_pad, name="ada_bwd")
    grads["c_ctx"] = _sum8(_allgather(cpart, in_vmem=True, name="gather_c_ctx_grad"), name="sum_c_ctx_grad")[0]

    delta, new_m, new_v = {}, {}, {}
    for n in BIG + ("ada_w",):
        shp = a[n].shape
        two = lambda t: t.reshape(-1, shp[-1])
        outs = _adamw(two(a[n]), two(grads[n]), two(a["m_" + n]), two(a["v_" + n]), name="adamw_" + n)
        delta[n], new_m[n], new_v[n] = (o.reshape(shp) for o in outs)
    shapes = [a[n].shape for n in SMALL_ADAM]
    packs = [_pack([src[pre + n] for n in SMALL_ADAM], F32, SUBLANES) for src, pre in ((a, ""), (grads, ""), (a, "m_"), (a, "v_"))]
    outs = _adamw(*packs, name="adamw_small")
    for dst, o in zip((delta, new_m, new_v), outs):
        dst.update(zip(SMALL_ADAM, _unpack(o, shapes)))

    return (loss, dx[None], *[grads[n].reshape(a[n].shape) for n in WEIGHTS], *[delta[n] for n in WEIGHTS],
            *[new_m[n] for n in WEIGHTS], *[new_v[n] for n in WEIGHTS])
```

```python
import functools

import jax
import jax.numpy as jnp
from jax import lax
from jax.experimental import pallas as pl
from jax.experimental.pallas import tpu as pltpu

F32, BF = jnp.float32, jnp.bfloat16
N_DEV = 8
MESH_AXES = ("x", "y", "c")
LANES = 128
SUBLANES = 8
HALO_ROWS = 16
VMEM_LIMIT = 56 << 20
PACK_W = 1024
EPS = 1e-6
QK_NOPE, QK_ROPE, V_DIM, GRID_W = 64, 32, 64, 64
QK_DIM = QK_NOPE + QK_ROPE
ROPE_THETA = 10000.0
SM_SCALE = QK_DIM ** -0.5
N_MOD = 6
ADAM_LR, ADAM_B1, ADAM_B2, ADAM_EPS, ADAM_WD, ADAM_STEP = 0.001, 0.9, 0.999, 1e-08, 0.01, 10


def _div(n, cap):
    if n <= cap:
        return n
    best = None
    for d in range(LANES, cap + 1, LANES):
        if n % d == 0:
            best = d
    return n if best is None else best


def _cp(*sem):
    return pltpu.CompilerParams(dimension_semantics=sem, vmem_limit_bytes=VMEM_LIMIT)


def _rms(x, n=None):
    d = x.shape[-1] if n is None else n
    return x * lax.rsqrt(jnp.sum(x * x, axis=-1, keepdims=True) / d + EPS)


@functools.lru_cache(maxsize=None)
def _lane_roll(shift):
    @jax.custom_vjp
    def roll(x):
        return pltpu.roll(x, shift, 1)

    def fwd(x):
        return roll(x), None

    def bwd(_, g):
        return (pltpu.roll(g, (LANES - shift) % LANES, 1),)

    roll.defvjp(fwd, bwd)
    return roll


def _mm(a, b, *, ta=False, tb=False, out_dtype=F32, res=None, gate=None, name):
    m, k = (a.shape[1], a.shape[0]) if ta else a.shape
    n = b.shape[0] if tb else b.shape[1]
    tm, tn, tk = _div(m, 512), _div(n, 1408), _div(k, 1536)
    nk = k // tk
    a_spec = pl.BlockSpec((tk, tm), lambda i, j, kk: (kk, i)) if ta else pl.BlockSpec((tm, tk), lambda i, j, kk: (i, kk))
    b_spec = pl.BlockSpec((tn, tk), lambda i, j, kk: (j, kk)) if tb else pl.BlockSpec((tk, tn), lambda i, j, kk: (kk, j))
    o_spec = pl.BlockSpec((tm, tn), lambda i, j, kk: (i, j))
    dn = (((0 if ta else 1,), (1 if tb else 0,)), ((), ()))
    fused = res is not None

    def body(*refs):
        if fused:
            a_ref, b_ref, res_ref, gate_ref, o_ref, f_ref, acc = refs
        else:
            a_ref, b_ref, o_ref, acc = refs
        kk = pl.program_id(2)

        @pl.when(kk == 0)
        def _():
            acc[...] = jnp.zeros_like(acc)

        acc[...] += lax.dot_general(a_ref[...].astype(BF), b_ref[...].astype(BF), dn, preferred_element_type=F32)

        @pl.when(kk == nk - 1)
        def _():
            if fused:
                f_ref[...] = acc[...]
                o_ref[...] = res_ref[...] + gate_ref[...] * acc[...]
            else:
                o_ref[...] = acc[...].astype(out_dtype)

    in_specs, args = [a_spec, b_spec], [a, b]
    out_specs, out_shape = o_spec, jax.ShapeDtypeStruct((m, n), out_dtype)
    if fused:
        in_specs += [o_spec, pl.BlockSpec((1, tn), lambda i, j, kk: (0, j))]
        args += [res, gate]
        out_specs = [o_spec, o_spec]
        out_shape = [jax.ShapeDtypeStruct((m, n), F32), jax.ShapeDtypeStruct((m, n), F32)]
    return pl.pallas_call(
        body, grid=(m // tm, n // tn, nk), in_specs=in_specs, out_specs=out_specs, out_shape=out_shape,
        scratch_shapes=[pltpu.VMEM((tm, tn), F32)], compiler_params=_cp("parallel", "parallel", "arbitrary"), name=name,
    )(*args)


def _bmm(a, b, *, a_blk=None, b_blk=None, ta=False, tb=False, out_dtype=F32, res=None, gate=None, name):
    a2, b2 = a.shape[-2:], b.shape[-2:]
    m, k = (a2[1], a2[0]) if ta else a2
    n = b2[0] if tb else b2[1]
    nbo = max([x.shape[0] for x, blk in ((a, a_blk), (b, b_blk)) if blk == "o"], default=1)
    nbk = max([x.shape[0] for x, blk in ((a, a_blk), (b, b_blk)) if blk == "k"], default=1)
    tm, tn, tk = _div(m, 512), _div(n, 1408), _div(k, 1536)
    nk = k // tk
    dn = (((0 if ta else 1,), (1 if tb else 0,)), ((), ()))
    fused = res is not None

    def spec(blk, shape2, idx2):
        if blk is None:
            return pl.BlockSpec(shape2, lambda bo, i, j, bk, kk: idx2(i, j, kk))
        if blk == "o":
            return pl.BlockSpec((None,) + shape2, lambda bo, i, j, bk, kk: (bo,) + idx2(i, j, kk))
        return pl.BlockSpec((None,) + shape2, lambda bo, i, j, bk, kk: (bk,) + idx2(i, j, kk))

    a_spec = spec(a_blk, (tk, tm), lambda i, j, kk: (kk, i)) if ta else spec(a_blk, (tm, tk), lambda i, j, kk: (i, kk))
    b_spec = spec(b_blk, (tn, tk), lambda i, j, kk: (j, kk)) if tb else spec(b_blk, (tk, tn), lambda i, j, kk: (kk, j))
    o_spec = spec("o" if nbo > 1 else None, (tm, tn), lambda i, j, kk: (i, j))

    def body(*refs):
        if fused:
            a_ref, b_ref, res_ref, gate_ref, o_ref, f_ref, acc = refs
        else:
            a_ref, b_ref, o_ref, acc = refs
        bk, kk = pl.program_id(3), pl.program_id(4)

        @pl.when((bk == 0) & (kk == 0))
        def _():
            acc[...] = jnp.zeros_like(acc)

        acc[...] += lax.dot_general(a_ref[...].astype(BF), b_ref[...].astype(BF), dn, preferred_element_type=F32)

        @pl.when((bk == nbk - 1) & (kk == nk - 1))
        def _():
            if fused:
                f_ref[...] = acc[...]
                o_ref[...] = res_ref[...] + gate_ref[...] * acc[...]
            else:
                o_ref[...] = acc[...].astype(out_dtype)

    out_shape2 = (m, n) if nbo == 1 else (nbo, m, n)
    in_specs, args = [a_spec, b_spec], [a, b]
    out_specs, out_shape = o_spec, jax.ShapeDtypeStruct(out_shape2, out_dtype)
    if fused:
        in_specs += [o_spec, pl.BlockSpec((1, tn), lambda bo, i, j, bk, kk: (0, j))]
        args += [res, gate]
        out_specs = [o_spec, o_spec]
        out_shape = [jax.ShapeDtypeStruct(out_shape2, F32)] * 2
    return pl.pallas_call(
        body, grid=(nbo, m // tm, n // tn, nbk, nk), in_specs=in_specs, out_specs=out_specs, out_shape=out_shape,
        scratch_shapes=[pltpu.VMEM((tm, tn), F32)],
        compiler_params=_cp("parallel", "parallel", "parallel", "arbitrary", "arbitrary"), name=name,
    )(*args)


def _row(tm, c, off=0):
    return pl.BlockSpec((tm, c), lambda i: (i + off, 0))


def _full(shape):
    return pl.BlockSpec(shape, lambda *_: (0,) * len(shape))


def _acc(ref, val, first):
    @pl.when(first)
    def _():
        ref[...] = jnp.zeros_like(ref)

    ref[...] += val


def _modnorm_f(x, g, sc, sh):
    return (_rms(x) * g) * (1.0 + sc) + sh


def _cls_spec(ncls, nctx, d):
    if ncls == 2:
        return pl.BlockSpec((1, 1, d), lambda i: (jnp.where(i < nctx, 0, 1), 0, 0))
    return pl.BlockSpec((1, 1, d), lambda i: (0, 0, 0))


def _modnorm(x, g, sc, sh, *, nctx, tm, name):
    r, d = x.shape
    cls = _cls_spec(sc.shape[0], nctx, d)

    def body(x_ref, g_ref, sc_ref, sh_ref, o_ref):
        o_ref[...] = _modnorm_f(x_ref[...], g_ref[...], sc_ref[0], sh_ref[0]).astype(BF)

    return pl.pallas_call(
        body, grid=(r // tm,), in_specs=[_row(tm, d), _full((1, d)), cls, cls], out_specs=_row(tm, d),
        out_shape=jax.ShapeDtypeStruct((r, d), BF), compiler_params=_cp("parallel"), name=name,
    )(x, g, sc, sh)


def _modnorm_bwd(x, dh, dres, g, sc, sh, *, nctx, tm, name):
    r, d = x.shape
    ncls = sc.shape[0]
    s = r - nctx * tm
    cls = _cls_spec(ncls, nctx, d)
    lat = pl.BlockSpec((tm, d), lambda i: (jnp.maximum(i - nctx, 0), 0))

    def body(x_ref, dh_ref, dres_ref, g_ref, sc_ref, sh_ref, dx_ref, dg_ref, dsc_ref, dsh_ref):
        i = pl.program_id(0)
        _, vjp = jax.vjp(_modnorm_f, x_ref[...], g_ref[...], sc_ref[0], sh_ref[0])
        dx, dg, dsc, dsh = vjp(dh_ref[...])
        _acc(dg_ref, dg, i == 0)
        first = (i == 0) | (i == nctx) if ncls == 2 else i == 0
        _acc(dsc_ref, dsc[None], first)
        _acc(dsh_ref, dsh[None], first)

        @pl.when(i >= nctx)
        def _():
            dx_ref[...] = dx + dres_ref[...]

    return pl.pallas_call(
        body, grid=(r // tm,), in_specs=[_row(tm, d), _row(tm, d), lat, _full((1, d)), cls, cls],
        out_specs=[lat, _full((1, d)), cls, cls],
        out_shape=[jax.ShapeDtypeStruct((s, d), F32), jax.ShapeDtypeStruct((1, d), F32),
                   jax.ShapeDtypeStruct((ncls, 1, d), F32), jax.ShapeDtypeStruct((ncls, 1, d), F32)],
        compiler_params=_cp("arbitrary"), name=name,
    )(x, dh, dres, g, sc, sh)


def _even_parts(p, c, ql, kvl):
    return p[:, :c], p[:, c:2 * c], p[:, 2 * c:2 * c + ql], p[:, 2 * c + ql:2 * c + ql + kvl]


def _even_mid_f(val, gate, cq, ckv, qa_g, kva_g):
    return val * jax.nn.sigmoid(gate), _rms(cq) * qa_g, _rms(ckv) * kva_g


def _even_mid(p, qa_g, kva_g, *, c, tm, name):
    r, w = p.shape
    ql, kvl = qa_g.shape[1], kva_g.shape[1]

    def body(p_ref, qg_ref, kg_ref, gl_ref, qn_ref, kvn_ref):
        gl, qn, kvn = _even_mid_f(*_even_parts(p_ref[...], c, ql, kvl), qg_ref[...], kg_ref[...])
        gl_ref[...] = gl
        qn_ref[...] = qn.astype(BF)
        kvn_ref[...] = kvn.astype(BF)

    return pl.pallas_call(
        body, grid=(r // tm,), in_specs=[_row(tm, w), _full((1, ql)), _full((1, kvl))],
        out_specs=[_row(tm, c), _row(tm, ql), _row(tm, kvl)],
        out_shape=[jax.ShapeDtypeStruct((r, c), F32), jax.ShapeDtypeStruct((r, ql), BF), jax.ShapeDtypeStruct((r, kvl), BF)],
        compiler_params=_cp("parallel"), name=name,
    )(p, qa_g, kva_g)


def _even_mid_bwd(p, dgl, dqn, dkvn, dkr, qa_g, kva_g, *, c, tm, name):
    r, w = p.shape
    ql, kvl = qa_g.shape[1], kva_g.shape[1]
    tail = w - (2 * c + ql + kvl + LANES)

    def body(p_ref, dgl_ref, dqn_ref, dkvn_ref, dkr_ref, qg_ref, kg_ref, dp_ref, dqg_ref, dkg_ref):
        i = pl.program_id(0)
        _, vjp = jax.vjp(_even_mid_f, *_even_parts(p_ref[...], c, ql, kvl), qg_ref[...], kg_ref[...])
        dval, dgate, dcq, dckv, dqg, dkg = vjp((dgl_ref[...], dqn_ref[...], dkvn_ref[...]))
        parts = [dval, dgate, dcq, dckv, dkr_ref[...]]
        if tail:
            parts.append(jnp.zeros((tm, tail), F32))
        dp_ref[...] = jnp.concatenate(parts, axis=1).astype(BF)
        _acc(dqg_ref, dqg, i == 0)
        _acc(dkg_ref, dkg, i == 0)

    return pl.pallas_call(
        body, grid=(r // tm,),
        in_specs=[_row(tm, w), _row(tm, c), _row(tm, ql), _row(tm, kvl), _row(tm, LANES), _full((1, ql)), _full((1, kvl))],
        out_specs=[_row(tm, w), _full((1, ql)), _full((1, kvl))],
        out_shape=[jax.ShapeDtypeStruct((r, w), BF), jax.ShapeDtypeStruct((1, ql), F32), jax.ShapeDtypeStruct((1, kvl), F32)],
        compiler_params=_cp("arbitrary"), name=name,
    )(p, dgl, dqn, dkvn, dkr, qa_g, kva_g)


def _halo_specs(tm, tc, total_rows, col_off=0, row_off=0):
    hb = HALO_ROWS
    nb = total_rows // hb
    cur = pl.BlockSpec((tm, tc), lambda j, i: (i + row_off // tm, j + col_off))
    prev = pl.BlockSpec((hb, tc), lambda j, i: (jnp.maximum((i * tm + row_off) // hb - 1, 0), j + col_off))
    nxt = pl.BlockSpec((hb, tc), lambda j, i: (jnp.minimum(((i + 1) * tm + row_off) // hb, nb - 1), j + col_off))
    return [prev, cur, nxt]


def _halo_specs_blk(tm, tc, total_rows, pair):
    hb = HALO_ROWS
    nb = total_rows // hb
    head = (2, None) if pair else (None,)
    idx = (lambda j, r: (0, j, r, 0)) if pair else (lambda j, r: (j, r, 0))
    cur = pl.BlockSpec(head + (tm, tc), lambda j, i: idx(j, i))
    prev = pl.BlockSpec(head + (hb, tc), lambda j, i: idx(j, jnp.maximum(i * tm // hb - 1, 0)))
    nxt = pl.BlockSpec(head + (hb, tc), lambda j, i: idx(j, jnp.minimum((i + 1) * tm // hb, nb - 1)))
    return [prev, cur, nxt]


def _ext(prev_ref, cur_ref, next_ref, i, n):
    prev = jnp.where(i == 0, 0.0, prev_ref[...].astype(F32))
    nxt = jnp.where(i == n - 1, 0.0, next_ref[...].astype(F32))
    return jnp.concatenate([prev, cur_ref[...].astype(F32), nxt], axis=-2)


def _sroll(x, shift):
    return pltpu.roll(x, shift % x.shape[0], 0)


def _conv_taps(e, w_ref, ksize, sign):
    pad = (ksize - 1) // 2
    out = None
    for k in range(ksize):
        t = w_ref[k:k + 1, :] * _sroll(e, -sign * (k - pad))
        out = t if out is None else out + t
    return out


def _core(e, tm):
    return e[HALO_ROWS:HALO_ROWS + tm]


def _ln_silu_f(u, g, b):
    mu = jnp.mean(u, axis=-1, keepdims=True)
    xc = u - mu
    y = xc * lax.rsqrt(jnp.mean(xc * xc, axis=-1, keepdims=True) + EPS)
    return jax.nn.silu(y * g + b)


def _conv_ln_silu(gl_all, w, b, ln_g, ln_b, *, s, row_off, tm, name):
    c = gl_all.shape[1]
    ksize = w.shape[0]
    n = s // tm

    def body(p_ref, c_ref, n_ref, w_ref, b_ref, g_ref, lb_ref, u_ref, a_ref):
        i = pl.program_id(1)
        e = _ext(p_ref, c_ref, n_ref, i, n)
        u = _core(_conv_taps(e, w_ref, ksize, 1), tm) + b_ref[...]
        u_ref[...] = u
        a_ref[...] = _ln_silu_f(u, g_ref[...], lb_ref[...]).astype(BF)

    out = pl.BlockSpec((tm, c), lambda j, i: (i, 0))
    return pl.pallas_call(
        body, grid=(1, n),
        in_specs=_halo_specs(tm, c, gl_all.shape[0], row_off=row_off) + [_full((ksize, c)), _full((1, c)), _full((1, c)), _full((1, c))],
        out_specs=[out, out], out_shape=[jax.ShapeDtypeStruct((s, c), F32), jax.ShapeDtypeStruct((s, c), BF)],
        compiler_params=_cp("parallel", "parallel"), name=name,
    )(gl_all, gl_all, gl_all, w, b, ln_g, ln_b)


def _ln_silu_bwd(u, dcat, ln_g, ln_b, *, tm, name):
    s, c = u.shape
    wc = dcat.shape[1]

    def body(u_ref, dc_ref, g_ref, b_ref, du_ref, dg_ref, db_ref):
        i = pl.program_id(0)
        _, vjp = jax.vjp(_ln_silu_f, u_ref[...], g_ref[...], b_ref[...])
        du, dg, db = vjp(dc_ref[:, :c])
        du_ref[...] = du
        _acc(dg_ref, dg, i == 0)
        _acc(db_ref, db, i == 0)

    return pl.pallas_call(
        body, grid=(s // tm,), in_specs=[_row(tm, c), _row(tm, wc), _full((1, c)), _full((1, c))],
        out_specs=[_row(tm, c), _full((1, c)), _full((1, c))],
        out_shape=[jax.ShapeDtypeStruct((s, c), F32), jax.ShapeDtypeStruct((1, c), F32), jax.ShapeDtypeStruct((1, c), F32)],
        compiler_params=_cp("arbitrary"), name=name,
    )(u, dcat, ln_g, ln_b)


def _dw_taps(dw_ref, g_core, e, ksize, tm, first):
    pad = (ksize - 1) // 2

    @pl.when(first)
    def _():
        dw_ref[...] = jnp.zeros_like(dw_ref)

    for k in range(ksize):
        dw_ref[k:k + 1, :] += jnp.sum(g_core * _core(_sroll(e, -(k - pad)), tm), axis=0, keepdims=True)


def _conv_bwd(du, gl_all, w, *, row_off, tm, name):
    s, c = du.shape
    ksize = w.shape[0]
    n = s // tm

    def body(dp_ref, dc_ref, dn_ref, gp_ref, gc_ref, gn_ref, w_ref, dgl_ref, dw_ref, db_ref):
        i = pl.program_id(1)
        de = _ext(dp_ref, dc_ref, dn_ref, i, n)
        ge = _ext(gp_ref, gc_ref, gn_ref, i, n)
        dgl_ref[...] = _core(_conv_taps(de, w_ref, ksize, -1), tm)
        du_core = dc_ref[...]
        _dw_taps(dw_ref, du_core, ge, ksize, tm, i == 0)
        _acc(db_ref, jnp.sum(du_core, axis=0, keepdims=True), i == 0)

    return pl.pallas_call(
        body, grid=(1, n),
        in_specs=_halo_specs(tm, c, s) + _halo_specs(tm, c, gl_all.shape[0], row_off=row_off) + [_full((ksize, c))],
        out_specs=[pl.BlockSpec((tm, c), lambda j, i: (i, 0)), _full((ksize, c)), _full((1, c))],
        out_shape=[jax.ShapeDtypeStruct((s, c), F32), jax.ShapeDtypeStruct((ksize, c), F32), jax.ShapeDtypeStruct((1, c), F32)],
        compiler_params=_cp("arbitrary", "arbitrary"), name=name,
    )(du, du, du, gl_all, gl_all, gl_all, w)


def _ffn_mid(u, w, b, *, tm, name):
    _, nb, s, tc = u.shape
    ksize = w.shape[1]
    n = s // tm

    def body(p_ref, c_ref, n_ref, w_ref, b_ref, z_ref):
        i = pl.program_id(1)
        e = _ext(p_ref, c_ref, n_ref, i, n)
        cg = _core(_conv_taps(e[0], w_ref, ksize, 1), tm) + b_ref[...]
        z_ref[...] = (jax.nn.silu(cg) * c_ref[1]).astype(BF)

    blk = lambda r: pl.BlockSpec((None, r, tc), lambda j, i: (j, 0, 0))
    return pl.pallas_call(
        body, grid=(nb, n), in_specs=_halo_specs_blk(tm, tc, s, True) + [blk(ksize), blk(1)],
        out_specs=pl.BlockSpec((None, tm, tc), lambda j, i: (j, i, 0)), out_shape=jax.ShapeDtypeStruct((nb, s, tc), BF),
        compiler_params=_cp("parallel", "parallel"), name=name,
    )(u, u, u, w, b)


def _ffn_mid_bwd(u, dz, w, b, *, tm, name):
    _, nb, s, tc = u.shape
    ksize = w.shape[1]
    n = s // tm

    def body(up_ref, uc_ref, un_ref, zp_ref, zc_ref, zn_ref, w_ref, b_ref, du_ref, dw_ref, db_ref):
        i = pl.program_id(1)
        ue = _ext(up_ref, uc_ref, un_ref, i, n)
        ge, ve = ue[0], ue[1]
        ze = _ext(zp_ref, zc_ref, zn_ref, i, n)
        cg = _conv_taps(ge, w_ref, ksize, 1) + b_ref[...]
        sg = jax.nn.sigmoid(cg)
        dcg = ze * ve * (sg * (1.0 + cg * (1.0 - sg)))
        du_ref[0] = _core(_conv_taps(dcg, w_ref, ksize, -1), tm).astype(BF)
        du_ref[1] = _core(ze * cg * sg, tm).astype(BF)
        dcg_core = _core(dcg, tm)
        _dw_taps(dw_ref, dcg_core, ge, ksize, tm, i == 0)
        _acc(db_ref, jnp.sum(dcg_core, axis=0, keepdims=True), i == 0)

    blk = lambda r: pl.BlockSpec((None, r, tc), lambda j, i: (j, 0, 0))
    return pl.pallas_call(
        body, grid=(nb, n), in_specs=_halo_specs_blk(tm, tc, s, True) + _halo_specs_blk(tm, tc, s, False) + [blk(ksize), blk(1)],
        out_specs=[pl.BlockSpec((2, None, tm, tc), lambda j, i: (0, j, i, 0)), blk(ksize), blk(1)],
        out_shape=[jax.ShapeDtypeStruct((2, nb, s, tc), BF), jax.ShapeDtypeStruct((nb, ksize, tc), F32), jax.ShapeDtypeStruct((nb, 1, tc), F32)],
        compiler_params=_cp("parallel", "arbitrary"), name=name,
    )(u, u, u, dz, dz, dz, w, b)


def _odd_mid(r3, w, b, *, tm, name):
    s, d3 = r3.shape
    d = d3 // 3
    ksize = w.shape[0]
    tc = _div(d, 1024)
    nc, n = d // tc, s // tm

    def body(bg_ref, cp_ref, cc_ref, cn_ref, up_ref, uc_ref, un_ref, w_ref, b_ref, m_ref):
        i = pl.program_id(1)
        pe = _ext(cp_ref, cc_ref, cn_ref, i, n) * _ext(up_ref, uc_ref, un_ref, i, n)
        sv = _core(_conv_taps(pe, w_ref, ksize, 1), tm) + b_ref[...]
        m_ref[...] = (bg_ref[...] * sv).astype(BF)

    cb = lambda j, i: (0, j)
    return pl.pallas_call(
        body, grid=(nc, n),
        in_specs=[pl.BlockSpec((tm, tc), lambda j, i: (i, j))] + _halo_specs(tm, tc, s, col_off=nc) + _halo_specs(tm, tc, s, col_off=2 * nc)
        + [pl.BlockSpec((ksize, tc), cb), pl.BlockSpec((1, tc), cb)],
        out_specs=pl.BlockSpec((tm, tc), lambda j, i: (i, j)), out_shape=jax.ShapeDtypeStruct((s, d), BF),
        compiler_params=_cp("parallel", "parallel"), name=name,
    )(r3, r3, r3, r3, r3, r3, r3, w, b)


def _odd_mid_bwd(r3, dm, w, b, *, tm, name):
    s, d3 = r3.shape
    d = d3 // 3
    ksize = w.shape[0]
    tc = _div(d, 1024)
    nc, n = d // tc, s // tm

    def body(bp_ref, bc_ref, bn_ref, cp_ref, cc_ref, cn_ref, up_ref, uc_ref, un_ref, mp_ref, mc_ref, mn_ref, w_ref, b_ref,
             dbg_ref, dcg_ref, du_ref, dw_ref, db_ref):
        i = pl.program_id(1)
        be = _ext(bp_ref, bc_ref, bn_ref, i, n)
        ce = _ext(cp_ref, cc_ref, cn_ref, i, n)
        ue = _ext(up_ref, uc_ref, un_ref, i, n)
        me = _ext(mp_ref, mc_ref, mn_ref, i, n)
        pe = ce * ue
        sv = _conv_taps(pe, w_ref, ksize, 1) + b_ref[...]
        ds = me * be
        dp = _core(_conv_taps(ds, w_ref, ksize, -1), tm)
        dbg_ref[...] = _core(me * sv, tm).astype(BF)
        dcg_ref[...] = (dp * uc_ref[...]).astype(BF)
        du_ref[...] = (dp * cc_ref[...]).astype(BF)
        ds_core = _core(ds, tm)
        _dw_taps(dw_ref, ds_core, pe, ksize, tm, i == 0)
        _acc(db_ref, jnp.sum(ds_core, axis=0, keepdims=True), i == 0)

    cb = lambda j, i: (0, j)
    col = pl.BlockSpec((tm, tc), lambda j, i: (i, j))
    return pl.pallas_call(
        body, grid=(nc, n),
        in_specs=_halo_specs(tm, tc, s) + _halo_specs(tm, tc, s, col_off=nc) + _halo_specs(tm, tc, s, col_off=2 * nc) + _halo_specs(tm, tc, s)
        + [pl.BlockSpec((ksize, tc), cb), pl.BlockSpec((1, tc), cb)],
        out_specs=[col, col, col, pl.BlockSpec((ksize, tc), cb), pl.BlockSpec((1, tc), cb)],
        out_shape=[jax.ShapeDtypeStruct((s, d), BF)] * 3 + [jax.ShapeDtypeStruct((ksize, d), F32), jax.ShapeDtypeStruct((1, d), F32)],
        compiler_params=_cp("parallel", "arbitrary"), name=name,
    )(r3, r3, r3, r3, r3, r3, r3, r3, r3, dm, dm, dm, w, b)


def _gate_bwd(dx, f, g, *, tm, name):
    s, d = dx.shape

    def body(dx_ref, f_ref, g_ref, df_ref, dg_ref):
        i = pl.program_id(0)
        dxv = dx_ref[...]
        df_ref[...] = (dxv * g_ref[...]).astype(BF)
        _acc(dg_ref, jnp.sum(dxv * f_ref[...], axis=0, keepdims=True), i == 0)

    return pl.pallas_call(
        body, grid=(s // tm,), in_specs=[_row(tm, d), _row(tm, d), _full((1, d))], out_specs=[_row(tm, d), _full((1, d))],
        out_shape=[jax.ShapeDtypeStruct((s, d), BF), jax.ShapeDtypeStruct((1, d), F32)], compiler_params=_cp("arbitrary"), name=name,
    )(dx, f, g)


def _loss_head(y, target, *, tm, name):
    s, d = y.shape

    def body(y_ref, t_ref, dy_ref, l_ref):
        i = pl.program_id(0)
        err = y_ref[...] - t_ref[...]
        dy_ref[...] = err * (1.0 / d)
        _acc(l_ref, jnp.full((1, LANES), 0.5 / d, F32) * jnp.sum(err * err), i == 0)

    return pl.pallas_call(
        body, grid=(s // tm,), in_specs=[_row(tm, d), _row(tm, d)], out_specs=[_row(tm, d), _full((1, LANES))],
        out_shape=[jax.ShapeDtypeStruct((s, d), F32), jax.ShapeDtypeStruct((1, LANES), F32)], compiler_params=_cp("arbitrary"), name=name,
    )(y, target)


def _rope(n, cos, sa, sb):
    return n * cos + _lane_roll(LANES - 8)(n) * sa + _lane_roll(8)(n) * sb


def _q_post_f(q, g, cos, sa, sb):
    outs = []
    for h in range(q.shape[1] // LANES):
        outs.append(_rope(_rms(q[:, h * LANES:(h + 1) * LANES], QK_DIM) * g, cos, sa, sb) * Q_PRESCALE)
    return jnp.concatenate(outs, axis=1)


def _kv_post_f(kv, kr, g, cos, sa, sb):
    lane = lax.broadcasted_iota(jnp.int32, (1, LANES), 1)
    krs = jnp.where((lane >= QK_NOPE) & (lane < QK_DIM), _lane_roll(QK_NOPE)(kr), 0.0)
    ks, vs = [], []
    for h in range(kv.shape[1] // LANES):
        kvh = kv[:, h * LANES:(h + 1) * LANES]
        ks.append(_rope(_rms(jnp.where(lane < QK_NOPE, kvh, krs), QK_DIM) * g, cos, sa, sb))
        vs.append(jnp.where(lane < V_DIM, _lane_roll(LANES - QK_NOPE)(kvh), 0.0))
    return jnp.concatenate(ks, axis=1), jnp.concatenate(vs, axis=1)


def _tab_specs(tm):
    return [_row(tm, LANES)] * 3


def _q_post(q, g, tabs, *, tm, name):
    r, w = q.shape

    def body(q_ref, g_ref, c_ref, sa_ref, sb_ref, o_ref):
        o_ref[...] = _q_post_f(q_ref[...], g_ref[...], c_ref[...], sa_ref[...], sb_ref[...]).astype(BF)

    return pl.pallas_call(
        body, grid=(r // tm,), in_specs=[_row(tm, w), _full((1, LANES))] + _tab_specs(tm), out_specs=_row(tm, w),
        out_shape=jax.ShapeDtypeStruct((r, w), BF), compiler_params=_cp("parallel"), name=name,
    )(q, g, *tabs)


def _q_post_bwd(q, dqh, g, tabs, *, tm, name):
    r, w = q.shape

    def body(q_ref, d_ref, g_ref, c_ref, sa_ref, sb_ref, dq_ref, dg_ref):
        i = pl.program_id(0)
        f = lambda qv, gv: _q_post_f(qv, gv, c_ref[...], sa_ref[...], sb_ref[...])
        _, vjp = jax.vjp(f, q_ref[...], g_ref[...])
        dq, dg = vjp(d_ref[...])
        dq_ref[...] = dq.astype(BF)
        _acc(dg_ref, dg, i == 0)

    return pl.pallas_call(
        body, grid=(r // tm,), in_specs=[_row(tm, w), _row(tm, w), _full((1, LANES))] + _tab_specs(tm),
        out_specs=[_row(tm, w), _full((1, LANES))],
        out_shape=[jax.ShapeDtypeStruct((r, w), BF), jax.ShapeDtypeStruct((1, LANES), F32)], compiler_params=_cp("arbitrary"), name=name,
    )(q, dqh, g, *tabs)


def _kr_spec(tm, w):
    return pl.BlockSpec((tm, LANES), lambda i: (i, w // LANES - 1))


def _kv_post(kv, p, g, tabs, *, tm, name):
    r, w = kv.shape

    def body(kv_ref, kr_ref, g_ref, c_ref, sa_ref, sb_ref, k_ref, v_ref):
        k, v = _kv_post_f(kv_ref[...], kr_ref[...], g_ref[...], c_ref[...], sa_ref[...], sb_ref[...])
        k_ref[...] = k.astype(BF)
        v_ref[...] = v.astype(BF)

    return pl.pallas_call(
        body, grid=(r // tm,), in_specs=[_row(tm, w), _kr_spec(tm, p.shape[1]), _full((1, LANES))] + _tab_specs(tm),
        out_specs=[_row(tm, w), _row(tm, w)], out_shape=[jax.ShapeDtypeStruct((r, w), BF)] * 2,
        compiler_params=_cp("parallel"), name=name,
    )(kv, p, g, *tabs)


def _kv_post_bwd(kv, p, dk, dv, g, tabs, *, tm, name):
    r, w = kv.shape

    def body(kv_ref, kr_ref, dk_ref, dv_ref, g_ref, c_ref, sa_ref, sb_ref, dkv_ref, dkr_ref, dg_ref):
        i = pl.program_id(0)
        f = lambda kvv, krv, gv: _kv_post_f(kvv, krv, gv, c_ref[...], sa_ref[...], sb_ref[...])
        _, vjp = jax.vjp(f, kv_ref[...], kr_ref[...], g_ref[...])
        dkv, dkr, dg = vjp((dk_ref[...], dv_ref[...]))
        dkv_ref[...] = dkv.astype(BF)
        dkr_ref[...] = dkr
        _acc(dg_ref, dg, i == 0)

    return pl.pallas_call(
        body, grid=(r // tm,),
        in_specs=[_row(tm, w), _kr_spec(tm, p.shape[1]), _row(tm, w), _row(tm, w), _full((1, LANES))] + _tab_specs(tm),
        out_specs=[_row(tm, w), _row(tm, LANES), _full((1, LANES))],
        out_shape=[jax.ShapeDtypeStruct((r, w), BF), jax.ShapeDtypeStruct((r, LANES), F32), jax.ShapeDtypeStruct((1, LANES), F32)],
        compiler_params=_cp("arbitrary"), name=name,
    )(kv, p, dk, dv, g, *tabs)


_NT = (((1,), (1,)), ((), ()))
_TN = (((0,), (0,)), ((), ()))
_NN = (((1,), (0,)), ((), ()))


ATTN_TQ, ATTN_TK = 512, 768
Q_PRESCALE = SM_SCALE * 1.4426950408889634
LN2 = 0.6931471805599453


def _attn_tiles(s, nk):
    return _div(s, ATTN_TQ), _div(nk, ATTN_TK)


def _chunk(ref, i, n):
    return ref[pl.ds(pl.multiple_of(i * n, n), n), :]


def _attn_fwd(q, k, v, *, name):
    s, w = q.shape
    nk = k.shape[0]
    tq, tk = _attn_tiles(s, nk)
    nj = nk // tk

    def body(q_ref, k_ref, v_ref, o_ref, lse_ref, m_s, l_s, acc_s):
        m_s[...] = jnp.full_like(m_s, -jnp.inf)
        l_s[...] = jnp.zeros_like(l_s)
        acc_s[...] = jnp.zeros_like(acc_s)
        qv = q_ref[...]

        def step(j, carry):
            sc = lax.dot_general(qv, _chunk(k_ref, j, tk), _NT, preferred_element_type=F32)
            m_prev = m_s[...]
            m_new = jnp.maximum(m_prev, jnp.max(sc, axis=1, keepdims=True))
            alpha = jnp.exp2(m_prev - m_new)
            pr = jnp.exp2(sc - m_new[:, :1])
            l_s[...] = alpha * l_s[...] + jnp.sum(pr, axis=1, keepdims=True)
            acc_s[...] = alpha * acc_s[...] + lax.dot_general(pr.astype(BF), _chunk(v_ref, j, tk), _NN, preferred_element_type=F32)
            m_s[...] = m_new
            return carry

        lax.fori_loop(0, nj, step, 0, unroll=True)
        o_ref[...] = (acc_s[...] / l_s[...]).astype(BF)
        lse_ref[...] = m_s[...] + jnp.log2(l_s[...])

    qs = pl.BlockSpec((tq, LANES), lambda h, i: (i, h))
    ks = pl.BlockSpec((nk, LANES), lambda h, i: (0, h))
    return pl.pallas_call(
        body, grid=(w // LANES, s // tq), in_specs=[qs, ks, ks], out_specs=[qs, qs],
        out_shape=[jax.ShapeDtypeStruct((s, w), BF), jax.ShapeDtypeStruct((s, w), F32)],
        scratch_shapes=[pltpu.VMEM((tq, LANES), F32)] * 3, compiler_params=_cp("parallel", "parallel"), name=name,
    )(q, k, v)


def _attn_delta(dcat, o, *, tm, name):
    s, w = o.shape
    wc = dcat.shape[1]

    def body(dc_ref, o_ref, do_ref, dl_ref):
        do = dc_ref[:, wc - w:]
        prod = do * o_ref[...].astype(F32)
        outs = []
        for h in range(w // LANES):
            outs.append(jnp.broadcast_to(jnp.sum(prod[:, h * LANES:(h + 1) * LANES], axis=1, keepdims=True), (tm, LANES)))
        do_ref[...] = do.astype(BF)
        dl_ref[...] = jnp.concatenate(outs, axis=1)

    return pl.pallas_call(
        body, grid=(s // tm,), in_specs=[_row(tm, wc), _row(tm, w)], out_specs=[_row(tm, w), _row(tm, w)],
        out_shape=[jax.ShapeDtypeStruct((s, w), BF), jax.ShapeDtypeStruct((s, w), F32)], compiler_params=_cp("parallel"), name=name,
    )(dcat, o)


def _attn_bwd(q, k, v, do, lse, delta, *, name):
    s, w = q.shape
    nk = k.shape[0]
    tq, tk = _attn_tiles(s, nk)
    ni, nj = s // tq, nk // tk

    def body(q_ref, k_ref, v_ref, do_ref, lse_ref, dl_ref, dq_ref, dk_ref, dv_ref, dk_s, dv_s):
        j = pl.program_id(1)

        @pl.when(j == 0)
        def _():
            dq_ref[...] = jnp.zeros_like(dq_ref)

        dk_s[...] = jnp.zeros_like(dk_s)
        dv_s[...] = jnp.zeros_like(dv_s)
        kv, vv = k_ref[...], v_ref[...]

        def step(i, carry):
            rows = pl.ds(pl.multiple_of(i * tq, tq), tq)
            qi, doi = q_ref[rows, :], do_ref[rows, :]
            sc = lax.dot_general(qi, kv, _NT, preferred_element_type=F32)
            pr = jnp.exp2(sc - lse_ref[rows, :][:, :1])
            dp = lax.dot_general(doi, vv, _NT, preferred_element_type=F32)
            ds = (pr * (dp - dl_ref[rows, :][:, :1])).astype(BF)
            dv_s[...] += lax.dot_general(pr.astype(BF), doi, _TN, preferred_element_type=F32)
            dk_s[...] += lax.dot_general(ds, qi, _TN, preferred_element_type=F32)
            dq_ref[rows, :] += lax.dot_general(ds, kv, _NN, preferred_element_type=F32)
            return carry

        lax.fori_loop(0, ni, step, 0, unroll=2 if ni % 2 == 0 else 1)
        dk_ref[...] = dk_s[...] * LN2
        dv_ref[...] = dv_s[...]

        @pl.when(j == nj - 1)
        def _():
            dq_ref[...] = dq_ref[...] * LN2

    qs = pl.BlockSpec((s, LANES), lambda h, j: (0, h))
    ks = pl.BlockSpec((tk, LANES), lambda h, j: (j, h))
    return pl.pallas_call(
        body, grid=(w // LANES, nj), in_specs=[qs, ks, ks, qs, qs, qs], out_specs=[qs, ks, ks],
        out_shape=[jax.ShapeDtypeStruct((s, w), F32), jax.ShapeDtypeStruct((nk, w), F32), jax.ShapeDtypeStruct((nk, w), F32)],
        scratch_shapes=[pltpu.VMEM((tk, LANES), F32)] * 2, compiler_params=_cp("parallel", "arbitrary"), name=name,
    )(q, k, v, do, lse, delta)


def _adamw(w, g, m, v, *, name):
    r, c = w.shape
    tr = r
    for cand in (512, 256, 128, 64, 32, 16, 8):
        if r % cand == 0 and cand * c * 4 <= (2 << 20):
            tr = cand
            break
    bc1 = 1.0 - ADAM_B1 ** ADAM_STEP
    bc2 = 1.0 - ADAM_B2 ** ADAM_STEP

    def body(w_ref, g_ref, m_ref, v_ref, d_ref, nm_ref, nv_ref):
        gv = g_ref[...]
        nm = ADAM_B1 * m_ref[...] + (1.0 - ADAM_B1) * gv
        nv = ADAM_B2 * v_ref[...] + (1.0 - ADAM_B2) * (gv * gv)
        d_ref[...] = -ADAM_LR * ((nm / bc1) / (jnp.sqrt(nv / bc2) + ADAM_EPS) + ADAM_WD * w_ref[...])
        nm_ref[...] = nm
        nv_ref[...] = nv

    spec = _row(tr, c)
    return pl.pallas_call(
        body, grid=(r // tr,), in_specs=[spec] * 4, out_specs=[spec] * 3,
        out_shape=[jax.ShapeDtypeStruct((r, c), F32)] * 3, compiler_params=_cp("parallel"), name=name,
    )(w, g, m, v)


def _mesh_pos():
    return lax.axis_index("x"), lax.axis_index("y"), lax.axis_index("c")


def _allgather(x, *, in_vmem, name):
    r, c = x.shape
    spec = pl.BlockSpec(memory_space=pltpu.VMEM if in_vmem else pl.ANY)

    def body(x_ref, out_ref, send_sems, recv_sems, local_sem):
        ix, iy, ic = _mesh_pos()
        me, sibling = (ix, iy, ic), (ix, iy, 1 - ic)
        chips = [(1 - ix, iy), (ix, 1 - iy), (1 - ix, 1 - iy)]

        def slab(px, py, pc):
            return out_ref.at[4 * px + 2 * py + pc]

        def copy(k, block, to, src=None):
            return pltpu.make_async_remote_copy(
                src_ref=slab(*block) if src is None else src, dst_ref=slab(*block),
                send_sem=send_sems.at[k], recv_sem=recv_sems.at[k], device_id=to, device_id_type=pl.DeviceIdType.MESH)

        mine = pltpu.make_async_copy(x_ref, slab(*me), local_sem)
        mine.start()
        first = [copy(0, me, sibling, src=x_ref)]
        first += [copy(1 + j, me, (*chip, ic), src=x_ref) for j, chip in enumerate(chips)]
        for cp in first:
            cp.start()
        passed = [copy(4 + j, (*chip, ic), sibling) for j, chip in enumerate(chips)]
        for j, chip in enumerate(chips):
            copy(1 + j, (*chip, ic), me).wait_recv()
            passed[j].start()
        copy(0, sibling, me).wait_recv()
        for j, chip in enumerate(chips):
            copy(4 + j, (*chip, 1 - ic), me).wait_recv()
        for cp in first + passed:
            cp.wait_send()
        mine.wait()

    return pl.pallas_call(
        body, out_shape=jax.ShapeDtypeStruct((N_DEV, r, c), x.dtype), in_specs=[spec], out_specs=spec,
        scratch_shapes=[pltpu.SemaphoreType.DMA((7,)), pltpu.SemaphoreType.DMA((7,)), pltpu.SemaphoreType.DMA], name=name,
    )(x)


def _exchange(g, *, name):
    _, r, c = g.shape
    spec = pl.BlockSpec(memory_space=pl.ANY)

    def body(g_ref, out_ref, send_sems, recv_sems, local_sem):
        ix, iy, ic = _mesh_pos()
        me = 4 * ix + 2 * iy + ic
        mine = pltpu.make_async_copy(g_ref.at[me], out_ref.at[me], local_sem)
        mine.start()
        sends, recvs = [], []
        for k in range(1, N_DEV):
            px = 1 - ix if k & 4 else ix
            py = 1 - iy if k & 2 else iy
            pc = 1 - ic if k & 1 else ic
            peer = 4 * px + 2 * py + pc
            mk = lambda src, dst: pltpu.make_async_remote_copy(
                src_ref=g_ref.at[src], dst_ref=out_ref.at[dst], send_sem=send_sems.at[k - 1], recv_sem=recv_sems.at[k - 1],
                device_id=(px, py, pc), device_id_type=pl.DeviceIdType.MESH)
            sends.append(mk(peer, me))
            recvs.append(mk(me, peer))
        for cp in sends:
            cp.start()
        for cp in recvs:
            cp.wait_recv()
        for cp in sends:
            cp.wait_send()
        mine.wait()

    return pl.pallas_call(
        body, out_shape=jax.ShapeDtypeStruct(g.shape, g.dtype), in_specs=[spec], out_specs=spec,
        scratch_shapes=[pltpu.SemaphoreType.DMA((7,)), pltpu.SemaphoreType.DMA((7,)), pltpu.SemaphoreType.DMA], name=name,
    )(g)


def _sum8(a, *, name):
    _, r, c = a.shape
    tr = r
    for cand in (512, 256, 128, 64, 32, 16):
        if r % cand == 0 and cand * c * 4 <= (1 << 20):
            tr = cand
            break

    def body(a_ref, o_ref):
        acc = a_ref[0].astype(F32)
        for d in range(1, N_DEV):
            acc = acc + a_ref[d].astype(F32)
        o_ref[...] = acc

    return pl.pallas_call(
        body, grid=(r // tr,), in_specs=[pl.BlockSpec((N_DEV, tr, c), lambda i: (0, i, 0))], out_specs=_row(tr, c),
        out_shape=jax.ShapeDtypeStruct((r, c), F32), compiler_params=_cp("parallel"), name=name,
    )(a)


ADA_ROWS = 16


def _silu_rows(c8, c_ctx):
    d = c8.shape[1]
    rows = jnp.concatenate([c8, c_ctx, jnp.zeros((ADA_ROWS - N_DEV - 1, d), F32)], axis=0)
    return jax.nn.silu(rows)


def _ada_fwd(c8, c_ctx, ada_w, ada_b_cols, *, name):
    nl, d, cols = ada_w.shape

    def body(c8_ref, cc_ref, w_ref, b_ref, o_ref):
        sc = _silu_rows(c8_ref[...], cc_ref[...]).astype(BF)
        for l in range(nl):
            o_ref[l] = lax.dot_general(sc, w_ref[l].astype(BF), _NN, preferred_element_type=F32) + b_ref[l:l + 1, :]

    return pl.pallas_call(
        body, out_shape=jax.ShapeDtypeStruct((nl, ADA_ROWS, cols), F32),
        compiler_params=pltpu.CompilerParams(vmem_limit_bytes=VMEM_LIMIT), name=name,
    )(c8, c_ctx, ada_w, ada_b_cols)


def _ada_bwd(c8, c_ctx, ada_w, g16, dctx_cols, tot_dm, dmodc_pad, *, name):
    nl, d, cols = ada_w.shape
    hi = lax.Precision.HIGHEST

    def body(c8_ref, cc_ref, w_ref, g_ref, dc_ref, tot_ref, dmc_ref, dw_ref, db_ref, part_ref):
        sc = _silu_rows(c8_ref[...], cc_ref[...])
        for l in range(nl):
            dw_ref[l] = lax.dot_general(sc, g_ref[l], _TN, precision=hi, preferred_element_type=F32)
        db_ref[...] = tot_ref[...]
        db_ref[0:1, :] += dmc_ref[...]
        ccv = cc_ref[...]
        sg = jax.nn.sigmoid(ccv)
        dsilu = sg * (1.0 + ccv * (1.0 - sg))
        part = lax.dot_general(dc_ref[...], w_ref[0], _NT, precision=hi, preferred_element_type=F32) * dsilu
        part_ref[...] = jnp.concatenate([part, jnp.zeros((SUBLANES - 1, d), F32)], axis=0)

    return pl.pallas_call(
        body, out_shape=[jax.ShapeDtypeStruct((nl, d, cols), F32), jax.ShapeDtypeStruct(tot_dm.shape, F32),
                         jax.ShapeDtypeStruct((SUBLANES, d), F32)],
        compiler_params=pltpu.CompilerParams(vmem_limit_bytes=VMEM_LIMIT), name=name,
    )(c8, c_ctx, ada_w, g16, dctx_cols, tot_dm, dmodc_pad)


def _rope_tables(s, lc):
    t = jnp.arange(s)
    half = QK_ROPE // 2
    inv = ROPE_THETA ** (-jnp.arange(0, half, 2, dtype=F32) / half)
    ang_r = (t // GRID_W).astype(F32)[:, None] * inv[None, :]
    ang_c = (t % GRID_W).astype(F32)[:, None] * inv[None, :]
    ang = jnp.concatenate([ang_r, ang_r, ang_c, ang_c], axis=-1)
    cos, sin = jnp.cos(ang), jnp.sin(ang)
    first = (jnp.arange(QK_ROPE) % half) < half // 2
    sa, sb = jnp.where(first, -sin, 0.0), jnp.where(first, 0.0, sin)

    def slot(mid, fill):
        body = jnp.concatenate([jnp.full((s, QK_NOPE), fill, F32), mid, jnp.full((s, LANES - QK_DIM), fill, F32)], axis=1)
        return jnp.concatenate([jnp.full((lc, LANES), fill, F32), body], axis=0)

    return slot(cos, 1.0), slot(sa, 0.0), slot(sb, 0.0)


def _pad_last(a, n):
    return jnp.pad(a, [(0, 0)] * (a.ndim - 1) + [(0, n - a.shape[-1])])


def _local_step(x, ctx, target, mods, modc, w):
    s, d = x.shape
    lc = ctx.shape[0]
    c = d // 2
    nh = (d - c) // V_DIM
    hw = nh * LANES
    ql, kvl = w["ev_qa_norm_g"].shape[-1], w["ev_kva_norm_g"].shape[-1]
    ei = 2 * c + ql + kvl + QK_ROPE
    eip = 2 * c + ql + kvl + LANES
    tm = 256 if (s % 256 == 0 and lc % 256 == 0) else 128
    nctx = lc // tm
    row = lambda v: v.reshape(1, -1)
    cls1 = lambda v: v.reshape(1, 1, -1)

    w_inp = _pad_last(w["ev_w_in"][0], eip)
    w_uqp = _pad_last(w["ev_w_uq"][0].reshape(ql, nh, QK_DIM), LANES).reshape(ql, hw)
    w_ukv = w["ev_w_ukv"][0]
    w_out = w["ev_w_out"][0]
    w_att = jnp.pad(w_out[c:].reshape(nh, V_DIM, d), [(0, 0), (0, LANES - V_DIM), (0, 0)]).reshape(hw, d)
    w_outp = jnp.concatenate([w_out[:c], w_att], axis=0)
    qg = _pad_last(row(w["ev_q_norm_g"]), LANES)
    kg = _pad_last(row(w["ev_k_norm_g"]), LANES)
    qa_g, kva_g = row(w["ev_qa_norm_g"]), row(w["ev_kva_norm_g"])
    ev_cw, ev_cb = w["ev_conv_w"][0], row(w["ev_conv_b"])
    ln_g, ln_b = row(w["ev_ln_g"]), row(w["ev_ln_b"])
    tabs = _rope_tables(s, lc)

    xall = jnp.concatenate([ctx, x], axis=0)
    sc0 = jnp.stack([modc[1], mods[0, 1]])[:, None, :]
    sh0 = jnp.stack([modc[0], mods[0, 0]])[:, None, :]
    g_mix0 = row(w["norm_mix_g"][0])
    hall = _modnorm(xall, g_mix0, sc0, sh0, nctx=nctx, tm=tm, name="l0_mix_norm")
    p_all = _mm(hall, w_inp, name="l0_w_in")
    gl_all, qn_all, kvn_all = _even_mid(p_all, qa_g, kva_g, c=c, tm=tm, name="l0_even_mid")
    u, a = _conv_ln_silu(gl_all, ev_cw, ev_cb, ln_g, ln_b, s=s, row_off=lc, tm=tm, name="l0_conformer")
    q_all = _mm(qn_all, w_uqp, name="l0_w_uq")
    kv_all = _mm(kvn_all, w_ukv, name="l0_w_ukv")
    qh_all = _q_post(q_all, qg, tabs, tm=tm, name="l0_q_post")
    k_all, v_all = _kv_post(kv_all, p_all, kg, tabs, tm=tm, name="l0_kv_post")
    qh = qh_all[lc:]
    o, lse = _attn_fwd(qh, k_all, v_all, name="l0_attn_fwd")
    cat = jnp.concatenate([a, o], axis=1)
    x1, f_mix0 = _mm(cat, w_outp, res=x, gate=row(mods[0, 2]), name="l0_w_out")

    nb = N_DEV // 2
    ffn_dim = w["ffn_conv_b"].shape[-1]
    tc = ffn_dim // nb
    ffn_k = w["ffn_conv_w"].shape[1]
    ffn_cw = [jnp.transpose(w["ffn_conv_w"][l].reshape(ffn_k, nb, tc), (1, 0, 2)) for l in range(2)]
    ffn_cb = [w["ffn_conv_b"][l].reshape(nb, 1, tc) for l in range(2)]

    def ffn_fwd(l, x_in):
        hf = _modnorm(x_in, row(w["norm_ffn_g"][l]), cls1(mods[l, 4]), cls1(mods[l, 3]), nctx=0, tm=tm, name=f"l{l}_ffn_norm")
        uu = _bmm(hf, w["ffn_w_up8"][l], b_blk="o", name=f"l{l}_w_up").reshape(2, nb, s, tc)
        z = _ffn_mid(uu, ffn_cw[l], ffn_cb[l], tm=tm, name=f"l{l}_ffn_mid")
        x_out, f = _bmm(z, w["ffn_w_down4"][l], a_blk="k", b_blk="k", res=x_in, gate=row(mods[l, 5]), name=f"l{l}_w_down")
        return x_out, (x_in, hf, uu, z, f)

    x2, ffn0 = ffn_fwd(0, x1)
    h1 = _modnorm(x2, row(w["norm_mix_g"][1]), cls1(mods[1, 1]), cls1(mods[1, 0]), nctx=0, tm=tm, name="l1_mix_norm")
    r3 = _mm(h1, w["od_w_in"][0], name="l1_w_in")
    od_cw, od_cb = w["od_conv_w"][0], row(w["od_conv_b"])
    m1 = _odd_mid(r3, od_cw, od_cb, tm=tm, name="l1_odd_mid")
    x3, f_mix1 = _mm(m1, w["od_w_out"][0], res=x2, gate=row(mods[1, 2]), name="l1_w_out")
    x4, ffn1 = ffn_fwd(1, x3)
    dx, loss_row = _loss_head(x4, target, tm=tm, name="loss_head")

    g = {}
    dmods = [[None] * N_MOD for _ in range(2)]

    def ffn_bwd(l, dx, saved):
        x_in, hf, uu, z, f = saved
        df, dmods[l][5] = _gate_bwd(dx, f, row(mods[l, 5]), tm=tm, name=f"l{l}_ffn_gate_bwd")
        dz = _bmm(df, w["ffn_w_down4"][l], b_blk="o", tb=True, name=f"l{l}_w_down_dx")
        dwd = _bmm(z, df, a_blk="o", ta=True, out_dtype=BF, name=f"l{l}_w_down_dw")
        du, dcw, dcb = _ffn_mid_bwd(uu, dz, ffn_cw[l], ffn_cb[l], tm=tm, name=f"l{l}_ffn_mid_bwd")
        du = du.reshape(N_DEV, s, tc)
        dhf = _bmm(du, w["ffn_w_up8"][l], a_blk="k", b_blk="k", tb=True, name=f"l{l}_w_up_dx")
        dwu = _bmm(hf, du, b_blk="o", ta=True, out_dtype=BF, name=f"l{l}_w_up_dw")
        dx, dgn, dsc, dsh = _modnorm_bwd(x_in, dhf, dx, row(w["norm_ffn_g"][l]), cls1(mods[l, 4]), cls1(mods[l, 3]),
                                         nctx=0, tm=tm, name=f"l{l}_ffn_norm_bwd")
        dmods[l][4], dmods[l][3] = dsc, dsh
        return dx, dict(ffn_w_up8=dwu, ffn_w_down8=dwd.reshape(N_DEV, tc // 2, d),
                        ffn_conv_w=jnp.transpose(dcw, (1, 0, 2)).reshape(ffn_k, ffn_dim), ffn_conv_b=dcb.reshape(1, ffn_dim), norm_ffn_g=dgn)

    dx, gf1 = ffn_bwd(1, dx, ffn1)
    df, dmods[1][2] = _gate_bwd(dx, f_mix1, row(mods[1, 2]), tm=tm, name="l1_mix_gate_bwd")
    dm1 = _mm(df, w["od_w_out"][0], tb=True, name="l1_w_out_dx")
    g["od_w_out"] = _mm(m1, df, ta=True, name="l1_w_out_dw")[None]
    dbg, dcg, duu, dcw, dcb = _odd_mid_bwd(r3, dm1, od_cw, od_cb, tm=tm, name="l1_odd_mid_bwd")
    g["od_conv_w"], g["od_conv_b"] = dcw[None], dcb
    dr3 = jnp.concatenate([dbg, dcg, duu], axis=1)
    dh1 = _mm(dr3, w["od_w_in"][0], tb=True, name="l1_w_in_dx")
    g["od_w_in"] = _mm(h1, dr3, ta=True, name="l1_w_in_dw")[None]
    dx, dgn1, dsc, dsh = _modnorm_bwd(x2, dh1, dx, row(w["norm_mix_g"][1]), cls1(mods[1, 1]), cls1(mods[1, 0]),
                                      nctx=0, tm=tm, name="l1_mix_norm_bwd")
    dmods[1][1], dmods[1][0] = dsc, dsh
    dx, gf0 = ffn_bwd(0, dx, ffn0)
    df, dmods[0][2] = _gate_bwd(dx, f_mix0, row(mods[0, 2]), tm=tm, name="l0_mix_gate_bwd")
    dcat = _mm(df, w_outp, tb=True, name="l0_w_out_dx")
    dw_outp = _mm(cat, df, ta=True, name="l0_w_out_dw")
    du0, g["ev_ln_g"], g["ev_ln_b"] = _ln_silu_bwd(u, dcat, ln_g, ln_b, tm=tm, name="l0_ln_silu_bwd")
    dgl, dcw, g["ev_conv_b"] = _conv_bwd(du0, gl_all, ev_cw, row_off=lc, tm=tm, name="l0_conformer_conv_bwd")
    g["ev_conv_w"] = dcw[None]
    do, delta = _attn_delta(dcat, o, tm=tm, name="l0_attn_delta")
    dq, dk, dv = _attn_bwd(qh, k_all, v_all, do, lse, delta, name="l0_attn_bwd")
    dq_all = jnp.concatenate([jnp.zeros((lc, hw), F32), dq], axis=0)
    dgl_all = jnp.concatenate([jnp.zeros((lc, c), F32), dgl], axis=0)
    dqp, dqg = _q_post_bwd(q_all, dq_all, qg, tabs, tm=tm, name="l0_q_post_bwd")
    dkvp, dkr, dkg = _kv_post_bwd(kv_all, p_all, dk, dv, kg, tabs, tm=tm, name="l0_kv_post_bwd")
    dqn = _mm(dqp, w_uqp, tb=True, name="l0_w_uq_dx")
    dw_uqp = _mm(qn_all, dqp, ta=True, name="l0_w_uq_dw")
    dkvn = _mm(dkvp, w_ukv, tb=True, name="l0_w_ukv_dx")
    g["ev_w_ukv"] = _mm(kvn_all, dkvp, ta=True, name="l0_w_ukv_dw")[None]
    dp, g["ev_qa_norm_g"], g["ev_kva_norm_g"] = _even_mid_bwd(p_all, dgl_all, dqn, dkvn, dkr, qa_g, kva_g, c=c, tm=tm, name="l0_even_mid_bwd")
    dhall = _mm(dp, w_inp, tb=True, name="l0_w_in_dx")
    dw_inp = _mm(hall, dp, ta=True, name="l0_w_in_dw")
    dx, dgn0, dsc2, dsh2 = _modnorm_bwd(xall, dhall, dx, g_mix0, sc0, sh0, nctx=nctx, tm=tm, name="l0_mix_norm_bwd")
    dmods[0][1], dmods[0][0] = dsc2[1], dsh2[1]
    dmodc = jnp.concatenate([dsh2[0], dsc2[0]], axis=0)

    g["ev_w_in"] = dw_inp[:, :ei][None]
    g["ev_w_uq"] = dw_uqp.reshape(ql, nh, LANES)[:, :, :QK_DIM].reshape(ql, nh * QK_DIM)[None]
    g["ev_w_out"] = jnp.concatenate([dw_outp[:c], dw_outp[c:].reshape(nh, LANES, d)[:, :V_DIM].reshape(nh * V_DIM, d)], axis=0)[None]
    g["ev_q_norm_g"], g["ev_k_norm_g"] = dqg[:, :QK_DIM], dkg[:, :QK_DIM]
    g["norm_mix_g"] = jnp.concatenate([dgn0, dgn1], axis=0)
    for name in ("ffn_w_up8", "ffn_w_down8"):
        g[name] = [gf0[name], gf1[name]]
    g["ffn_conv_w"] = jnp.stack([gf0["ffn_conv_w"], gf1["ffn_conv_w"]])
    for name in ("ffn_conv_b", "norm_ffn_g"):
        g[name] = jnp.concatenate([gf0[name], gf1[name]], axis=0)
    dmods_arr = jnp.stack([jnp.concatenate([v.reshape(1, d) for v in dmods[l]], axis=0) for l in range(2)])
    return loss_row, dx, g, dmods_arr, dmodc


WEIGHTS = ("c_ctx", "ada_w", "ada_b", "norm_mix_g", "norm_ffn_g", "ffn_w_up", "ffn_conv_w", "ffn_conv_b", "ffn_w_down", "ev_w_in",
           "ev_conv_w", "ev_conv_b", "ev_ln_g", "ev_ln_b", "ev_qa_norm_g", "ev_w_uq", "ev_kva_norm_g", "ev_w_ukv", "ev_q_norm_g",
           "ev_k_norm_g", "ev_w_out", "od_w_in", "od_conv_w", "od_conv_b", "od_w_out")
SHARD_DIM = dict(ada_w=2, ffn_w_up=2, ffn_conv_w=2, ffn_w_down=1, ev_w_in=2, ev_conv_w=2, ev_w_uq=2, ev_w_ukv=2, ev_w_out=1,
                 od_w_in=2, od_conv_w=2, od_conv_b=1, od_w_out=1)
BIG = ("ffn_w_up", "ffn_w_down", "ev_w_in", "ev_w_uq", "ev_w_ukv", "ev_w_out", "od_w_in", "od_w_out")
SMALL_SHARDED = ("ffn_conv_w", "ev_conv_w", "od_conv_w", "od_conv_b")
SMALL_GRADS = ("norm_mix_g", "norm_ffn_g", "ffn_conv_w", "ffn_conv_b", "ev_conv_w", "ev_conv_b", "ev_ln_g", "ev_ln_b",
               "ev_qa_norm_g", "ev_kva_norm_g", "ev_q_norm_g", "ev_k_norm_g", "od_conv_w", "od_conv_b")
SMALL_ADAM = ("c_ctx", "ada_b") + SMALL_GRADS


def _size(shape):
    n = 1
    for v in shape:
        n *= v
    return n


def _pack(parts, dtype, row_mult, lead=0):
    lead_shape = parts[0].shape[:lead]
    flat = jnp.concatenate([p.astype(dtype).reshape(lead_shape + (-1,)) for p in parts], axis=-1)
    per = PACK_W * row_mult
    total = -(-flat.shape[-1] // per) * per
    flat = jnp.pad(flat, [(0, 0)] * lead + [(0, total - flat.shape[-1])])
    return flat.reshape(lead_shape + (total // PACK_W, PACK_W))


def _unpack(buf, shapes):
    lead_shape = buf.shape[:-2]
    flat = buf.reshape(lead_shape + (-1,))
    out, off = [], 0
    for shp in shapes:
        n = _size(shp)
        out.append(flat[..., off:off + n].reshape(lead_shape + tuple(shp)))
        off += n
    return out


def _unshard(pieces, k):
    t = jnp.moveaxis(pieces, 0, k)
    return t.reshape(t.shape[:k] + (t.shape[k] * t.shape[k + 1],) + t.shape[k + 2:])


def _shard_major(full, k):
    t = full.reshape(full.shape[:k] + (N_DEV, full.shape[k] // N_DEV) + full.shape[k + 1:])
    return jnp.moveaxis(t, k, 0)


def _my_shard(full, k, me):
    n = full.shape[k] // N_DEV
    return lax.dynamic_slice_in_dim(full, me * n, n, axis=k)


def kernel(x, c, ctx, c_ctx, ada_w, ada_b, norm_mix_g, norm_ffn_g, ffn_w_up, ffn_conv_w, ffn_conv_b, ffn_w_down, ev_w_in, ev_conv_w, ev_conv_b, ev_ln_g, ev_ln_b, ev_qa_norm_g, ev_w_uq, ev_kva_norm_g, ev_w_ukv, ev_q_norm_g, ev_k_norm_g, ev_w_out, od_w_in, od_conv_w, od_conv_b, od_w_out, loss_target, m_c_ctx, m_ada_w, m_ada_b, m_norm_mix_g, m_norm_ffn_g, m_ffn_w_up, m_ffn_conv_w, m_ffn_conv_b, m_ffn_w_down, m_ev_w_in, m_ev_conv_w, m_ev_conv_b, m_ev_ln_g, m_ev_ln_b, m_ev_qa_norm_g, m_ev_w_uq, m_ev_kva_norm_g, m_ev_w_ukv, m_ev_q_norm_g, m_ev_k_norm_g, m_ev_w_out, m_od_w_in, m_od_conv_w, m_od_conv_b, m_od_w_out, v_c_ctx, v_ada_w, v_ada_b, v_norm_mix_g, v_norm_ffn_g, v_ffn_w_up, v_ffn_conv_w, v_ffn_conv_b, v_ffn_w_down, v_ev_w_in, v_ev_conv_w, v_ev_conv_b, v_ev_ln_g, v_ev_ln_b, v_ev_qa_norm_g, v_ev_w_uq, v_ev_kva_norm_g, v_ev_w_ukv, v_ev_q_norm_g, v_ev_k_norm_g, v_ev_w_out, v_od_w_in, v_od_conv_w, v_od_conv_b, v_od_w_out):
    a = dict(locals())
    ix, iy, ic = _mesh_pos()
    me = 4 * ix + 2 * iy + ic
    xs, ctxs, target = x[0], ctx[0], loss_target[0]
    d = xs.shape[1]
    nl, _, cols = ada_w.shape

    pieces = [(n, l) for n in BIG for l in range(a[n].shape[0])]
    piece_shapes = [a[n].shape[1:] for n, _ in pieces]
    wall = _allgather(_pack([a[n][l] for n, l in pieces], BF, 16), in_vmem=False, name="gather_big_weights")
    w = {"ffn_w_up8": [], "ffn_w_down4": []}
    for (n, l), p in zip(pieces, _unpack(wall, piece_shapes)):
        if n == "ffn_w_up":
            w["ffn_w_up8"].append(p)
        elif n == "ffn_w_down":
            w["ffn_w_down4"].append(p.reshape(N_DEV // 2, 2 * p.shape[1], p.shape[2]))
        else:
            w[n] = _unshard(p, SHARD_DIM[n] - 1)[None]
    sall = _allgather(_pack([c] + [a[n] for n in SMALL_SHARDED], F32, SUBLANES), in_vmem=True, name="gather_cond")
    sp = _unpack(sall, [c.shape] + [a[n].shape for n in SMALL_SHARDED])
    c8 = sp[0].reshape(N_DEV, d)
    for n, p in zip(SMALL_SHARDED, sp[1:]):
        w[n] = _unshard(p, SHARD_DIM[n])
    for n in SMALL_GRADS:
        if n not in SMALL_SHARDED:
            w[n] = a[n]

    cc = c_ctx.reshape(1, d)
    mpart = _ada_fwd(c8, cc, ada_w, lax.dynamic_slice_in_dim(ada_b, me * cols, cols, axis=1), name="ada_fwd")
    mall = _allgather(mpart.reshape(nl * ADA_ROWS, cols), in_vmem=True, name="gather_mod").reshape(N_DEV, nl, ADA_ROWS, cols)
    mine = lax.dynamic_index_in_dim(mall, me, axis=2, keepdims=False)
    mods = jnp.transpose(mine, (1, 0, 2)).reshape(nl, N_MOD, d)
    modc = mall[:, 0, N_DEV, :].reshape(-1)[:2 * d].reshape(2, d)

    loss_row, dx, g, dmods, dmodc = _local_step(xs, ctxs, target, mods, modc, w)
    loss = lax.psum(loss_row[0, 0], MESH_AXES)

    dest_major = {"ffn_w_up": g["ffn_w_up8"], "ffn_w_down": g["ffn_w_down8"]}
    gparts = [dest_major[n][l] if n in dest_major else _shard_major(g[n][l], SHARD_DIM[n] - 1) for n, l in pieces]
    gsum = _sum8(_exchange(_pack(gparts, BF, 16, lead=1), name="exchange_big_grads"), name="sum_big_grads")
    grads = {}
    for (n, l), t in zip(pieces, _unpack(gsum, piece_shapes)):
        grads.setdefault(n, []).append(t)
    grads = {n: jnp.stack(ts) for n, ts in grads.items()}

    small_parts = [dmods, dmodc] + [g[n] for n in SMALL_GRADS]
    small_shapes = [p.shape for p in small_parts]
    small = _allgather(_pack(small_parts, F32, SUBLANES), in_vmem=True, name="gather_small_grads")
    tots = _unpack(_sum8(small, name="sum_small_grads"), small_shapes)
    for n, t in zip(SMALL_GRADS, tots[2:]):
        grads[n] = _my_shard(t, SHARD_DIM[n], me) if n in SMALL_SHARDED else t
    dm_all = _unpack(small, small_shapes[:1])[0].reshape(N_DEV, nl, N_MOD * d)
    tot_dm = tots[0].reshape(nl, N_MOD * d)
    dmodc_pad = _pad_last(tots[1].reshape(1, 2 * d), N_MOD * d)
    dm_cols = lax.dynamic_slice_in_dim(dm_all, me * cols, cols, axis=2)
    dctx_cols = lax.dynamic_slice_in_dim(dmodc_pad, me * cols, cols, axis=1)
    ctx_rows = jnp.concatenate([dctx_cols[None], jnp.zeros((nl - 1, 1, cols), F32)], axis=0)
    g16 = jnp.concatenate([jnp.transpose(dm_cols, (1, 0, 2)), ctx_rows, jnp.zeros((nl, ADA_ROWS - N_DEV - 1, cols), F32)], axis=1)
    grads["ada_w"], grads["ada_b"], cpart = _ada_bwd(c8, cc, ada_w, g16, dctx_cols, tot_dm, dmodc_pad, name="ada_bwd")
    grads["c_ctx"] = _sum8(_allgather(cpart, in_vmem=True, name="gather_c_ctx_grad"), name="sum_c_ctx_grad")[0]

    delta, new_m, new_v = {}, {}, {}
    for n in BIG + ("ada_w",):
        shp = a[n].shape
        two = lambda t: t.reshape(-1, shp[-1])
        outs = _adamw(two(a[n]), two(grads[n]), two(a["m_" + n]), two(a["v_" + n]), name="adamw_" + n)
        delta[n], new_m[n], new_v[n] = (o.reshape(shp) for o in outs)
    shapes = [a[n].shape for n in SMALL_ADAM]
    packs = [_pack([src[pre + n] for n in SMALL_ADAM], F32, SUBLANES) for src, pre in ((a, ""), (grads, ""), (a, "m_"), (a, "v_"))]
    outs = _adamw(*packs, name="adamw_small")
    for dst, o in zip((delta, new_m, new_v), outs):
        dst.update(zip(SMALL_ADAM, _unpack(o, shapes)))

    return (loss, dx[None], *[grads[n].reshape(a[n].shape) for n in WEIGHTS], *[delta[n] for n in WEIGHTS],
            *[new_m[n] for n in WEIGHTS], *[new_v[n] for n in WEIGHTS])
```

```python
import functools

import jax
import jax.numpy as jnp
from jax import lax
from jax.experimental import pallas as pl
from jax.experimental.pallas import tpu as pltpu

F32, BF = jnp.float32, jnp.bfloat16
N_DEV = 8
MESH_AXES = ("x", "y", "c")
LANES = 128
SUBLANES = 8
HALO_ROWS = 16
VMEM_LIMIT = 56 << 20
PACK_W = 1024
EPS = 1e-6
QK_NOPE, QK_ROPE, V_DIM, GRID_W = 64, 32, 64, 64
QK_DIM = QK_NOPE + QK_ROPE
ROPE_THETA = 10000.0
SM_SCALE = QK_DIM ** -0.5
N_MOD = 6
ADAM_LR, ADAM_B1, ADAM_B2, ADAM_EPS, ADAM_WD, ADAM_STEP = 0.001, 0.9, 0.999, 1e-08, 0.01, 10


def _div(n, cap):
    if n <= cap:
        return n
    best = None
    for d in range(LANES, cap + 1, LANES):
        if n % d == 0:
            best = d
    return n if best is None else best


def _cp(*sem):
    return pltpu.CompilerParams(dimension_semantics=sem, vmem_limit_bytes=VMEM_LIMIT)


def _rms(x, n=None):
    d = x.shape[-1] if n is None else n
    return x * lax.rsqrt(jnp.sum(x * x, axis=-1, keepdims=True) / d + EPS)


@functools.lru_cache(maxsize=None)
def _lane_roll(shift):
    @jax.custom_vjp
    def roll(x):
        return pltpu.roll(x, shift, 1)

    def fwd(x):
        return roll(x), None

    def bwd(_, g):
        return (pltpu.roll(g, (LANES - shift) % LANES, 1),)

    roll.defvjp(fwd, bwd)
    return roll


def _mm(a, b, *, ta=False, tb=False, out_dtype=F32, res=None, gate=None, name):
    m, k = (a.shape[1], a.shape[0]) if ta else a.shape
    n = b.shape[0] if tb else b.shape[1]
    tm, tn, tk = _div(m, 512), _div(n, 1408), _div(k, 1536)
    nk = k // tk
    a_spec = pl.BlockSpec((tk, tm), lambda i, j, kk: (kk, i)) if ta else pl.BlockSpec((tm, tk), lambda i, j, kk: (i, kk))
    b_spec = pl.BlockSpec((tn, tk), lambda i, j, kk: (j, kk)) if tb else pl.BlockSpec((tk, tn), lambda i, j, kk: (kk, j))
    o_spec = pl.BlockSpec((tm, tn), lambda i, j, kk: (i, j))
    dn = (((0 if ta else 1,), (1 if tb else 0,)), ((), ()))
    fused = res is not None

    def body(*refs):
        if fused:
            a_ref, b_ref, res_ref, gate_ref, o_ref, f_ref, acc = refs
        else:
            a_ref, b_ref, o_ref, acc = refs
        kk = pl.program_id(2)

        @pl.when(kk == 0)
        def _():
            acc[...] = jnp.zeros_like(acc)

        acc[...] += lax.dot_general(a_ref[...].astype(BF), b_ref[...].astype(BF), dn, preferred_element_type=F32)

        @pl.when(kk == nk - 1)
        def _():
            if fused:
                f_ref[...] = acc[...]
                o_ref[...] = res_ref[...] + gate_ref[...] * acc[...]
            else:
                o_ref[...] = acc[...].astype(out_dtype)

    in_specs, args = [a_spec, b_spec], [a, b]
    out_specs, out_shape = o_spec, jax.ShapeDtypeStruct((m, n), out_dtype)
    if fused:
        in_specs += [o_spec, pl.BlockSpec((1, tn), lambda i, j, kk: (0, j))]
        args += [res, gate]
        out_specs = [o_spec, o_spec]
        out_shape = [jax.ShapeDtypeStruct((m, n), F32), jax.ShapeDtypeStruct((m, n), F32)]
    return pl.pallas_call(
        body, grid=(m // tm, n // tn, nk), in_specs=in_specs, out_specs=out_specs, out_shape=out_shape,
        scratch_shapes=[pltpu.VMEM((tm, tn), F32)], compiler_params=_cp("parallel", "parallel", "arbitrary"), name=name,
    )(*args)


def _bmm(a, b, *, a_blk=None, b_blk=None, ta=False, tb=False, out_dtype=F32, res=None, gate=None, name):
    a2, b2 = a.shape[-2:], b.shape[-2:]
    m, k = (a2[1], a2[0]) if ta else a2
    n = b2[0] if tb else b2[1]
    nbo = max([x.shape[0] for x, blk in ((a, a_blk), (b, b_blk)) if blk == "o"], default=1)
    nbk = max([x.shape[0] for x, blk in ((a, a_blk), (b, b_blk)) if blk == "k"], default=1)
    tm, tn, tk = _div(m, 512), _div(n, 1408), _div(k, 1536)
    nk = k // tk
    dn = (((0 if ta else 1,), (1 if tb else 0,)), ((), ()))
    fused = res is not None

    def spec(blk, shape2, idx2):
        if blk is None:
            return pl.BlockSpec(shape2, lambda bo, i, j, bk, kk: idx2(i, j, kk))
        if blk == "o":
            return pl.BlockSpec((None,) + shape2, lambda bo, i, j, bk, kk: (bo,) + idx2(i, j, kk))
        return pl.BlockSpec((None,) + shape2, lambda bo, i, j, bk, kk: (bk,) + idx2(i, j, kk))

    a_spec = spec(a_blk, (tk, tm), lambda i, j, kk: (kk, i)) if ta else spec(a_blk, (tm, tk), lambda i, j, kk: (i, kk))
    b_spec = spec(b_blk, (tn, tk), lambda i, j, kk: (j, kk)) if tb else spec(b_blk, (tk, tn), lambda i, j, kk: (kk, j))
    o_spec = spec("o" if nbo > 1 else None, (tm, tn), lambda i, j, kk: (i, j))

    def body(*refs):
        if fused:
            a_ref, b_ref, res_ref, gate_ref, o_ref, f_ref, acc = refs
        else:
            a_ref, b_ref, o_ref, acc = refs
        bk, kk = pl.program_id(3), pl.program_id(4)

        @pl.when((bk == 0) & (kk == 0))
        def _():
            acc[...] = jnp.zeros_like(acc)

        acc[...] += lax.dot_general(a_ref[...].astype(BF), b_ref[...].astype(BF), dn, preferred_element_type=F32)

        @pl.when((bk == nbk - 1) & (kk == nk - 1))
        def _():
            if fused:
                f_ref[...] = acc[...]
                o_ref[...] = res_ref[...] + gate_ref[...] * acc[...]
            else:
                o_ref[...] = acc[...].astype(out_dtype)

    out_shape2 = (m, n) if nbo == 1 else (nbo, m, n)
    in_specs, args = [a_spec, b_spec], [a, b]
    out_specs, out_shape = o_spec, jax.ShapeDtypeStruct(out_shape2, out_dtype)
    if fused:
        in_specs += [o_spec, pl.BlockSpec((1, tn), lambda bo, i, j, bk, kk: (0, j))]
        args += [res, gate]
        out_specs = [o_spec, o_spec]
        out_shape = [jax.ShapeDtypeStruct(out_shape2, F32)] * 2
    return pl.pallas_call(
        body, grid=(nbo, m // tm, n // tn, nbk, nk), in_specs=in_specs, out_specs=out_specs, out_shape=out_shape,
        scratch_shapes=[pltpu.VMEM((tm, tn), F32)],
        compiler_params=_cp("parallel", "parallel", "parallel", "arbitrary", "arbitrary"), name=name,
    )(*args)


def _row(tm, c, off=0):
    return pl.BlockSpec((tm, c), lambda i: (i + off, 0))


def _full(shape):
    return pl.BlockSpec(shape, lambda *_: (0,) * len(shape))


def _acc(ref, val, first):
    @pl.when(first)
    def _():
        ref[...] = jnp.zeros_like(ref)

    ref[...] += val


def _modnorm_f(x, g, sc, sh):
    return (_rms(x) * g) * (1.0 + sc) + sh


def _cls_spec(ncls, nctx, d):
    if ncls == 2:
        return pl.BlockSpec((1, 1, d), lambda i: (jnp.where(i < nctx, 0, 1), 0, 0))
    return pl.BlockSpec((1, 1, d), lambda i: (0, 0, 0))


def _modnorm(x, g, sc, sh, *, nctx, tm, name):
    r, d = x.shape
    cls = _cls_spec(sc.shape[0], nctx, d)

    def body(x_ref, g_ref, sc_ref, sh_ref, o_ref):
        o_ref[...] = _modnorm_f(x_ref[...], g_ref[...], sc_ref[0], sh_ref[0]).astype(BF)

    return pl.pallas_call(
        body, grid=(r // tm,), in_specs=[_row(tm, d), _full((1, d)), cls, cls], out_specs=_row(tm, d),
        out_shape=jax.ShapeDtypeStruct((r, d), BF), compiler_params=_cp("parallel"), name=name,
    )(x, g, sc, sh)


def _modnorm_bwd(x, dh, dres, g, sc, sh, *, nctx, tm, name):
    r, d = x.shape
    ncls = sc.shape[0]
    s = r - nctx * tm
    cls = _cls_spec(ncls, nctx, d)
    lat = pl.BlockSpec((tm, d), lambda i: (jnp.maximum(i - nctx, 0), 0))

    def body(x_ref, dh_ref, dres_ref, g_ref, sc_ref, sh_ref, dx_ref, dg_ref, dsc_ref, dsh_ref):
        i = pl.program_id(0)
        _, vjp = jax.vjp(_modnorm_f, x_ref[...], g_ref[...], sc_ref[0], sh_ref[0])
        dx, dg, dsc, dsh = vjp(dh_ref[...])
        _acc(dg_ref, dg, i == 0)
        first = (i == 0) | (i == nctx) if ncls == 2 else i == 0
        _acc(dsc_ref, dsc[None], first)
        _acc(dsh_ref, dsh[None], first)

        @pl.when(i >= nctx)
        def _():
            dx_ref[...] = dx + dres_ref[...]

    return pl.pallas_call(
        body, grid=(r // tm,), in_specs=[_row(tm, d), _row(tm, d), lat, _full((1, d)), cls, cls],
        out_specs=[lat, _full((1, d)), cls, cls],
        out_shape=[jax.ShapeDtypeStruct((s, d), F32), jax.ShapeDtypeStruct((1, d), F32),
                   jax.ShapeDtypeStruct((ncls, 1, d), F32), jax.ShapeDtypeStruct((ncls, 1, d), F32)],
        compiler_params=_cp("arbitrary"), name=name,
    )(x, dh, dres, g, sc, sh)


def _even_parts(p, c, ql, kvl):
    return p[:, :c], p[:, c:2 * c], p[:, 2 * c:2 * c + ql], p[:, 2 * c + ql:2 * c + ql + kvl]


def _even_mid_f(val, gate, cq, ckv, qa_g, kva_g):
    return val * jax.nn.sigmoid(gate), _rms(cq) * qa_g, _rms(ckv) * kva_g


def _even_mid(p, qa_g, kva_g, *, c, tm, name):
    r, w = p.shape
    ql, kvl = qa_g.shape[1], kva_g.shape[1]

    def body(p_ref, qg_ref, kg_ref, gl_ref, qn_ref, kvn_ref):
        gl, qn, kvn = _even_mid_f(*_even_parts(p_ref[...], c, ql, kvl), qg_ref[...], kg_ref[...])
        gl_ref[...] = gl
        qn_ref[...] = qn.astype(BF)
        kvn_ref[...] = kvn.astype(BF)

    return pl.pallas_call(
        body, grid=(r // tm,), in_specs=[_row(tm, w), _full((1, ql)), _full((1, kvl))],
        out_specs=[_row(tm, c), _row(tm, ql), _row(tm, kvl)],
        out_shape=[jax.ShapeDtypeStruct((r, c), F32), jax.ShapeDtypeStruct((r, ql), BF), jax.ShapeDtypeStruct((r, kvl), BF)],
        compiler_params=_cp("parallel"), name=name,
    )(p, qa_g, kva_g)


def _even_mid_bwd(p, dgl, dqn, dkvn, dkr, qa_g, kva_g, *, c, tm, name):
    r, w = p.shape
    ql, kvl = qa_g.shape[1], kva_g.shape[1]
    tail = w - (2 * c + ql + kvl + LANES)

    def body(p_ref, dgl_ref, dqn_ref, dkvn_ref, dkr_ref, qg_ref, kg_ref, dp_ref, dqg_ref, dkg_ref):
        i = pl.program_id(0)
        _, vjp = jax.vjp(_even_mid_f, *_even_parts(p_ref[...], c, ql, kvl), qg_ref[...], kg_ref[...])
        dval, dgate, dcq, dckv, dqg, dkg = vjp((dgl_ref[...], dqn_ref[...], dkvn_ref[...]))
        parts = [dval, dgate, dcq, dckv, dkr_ref[...]]
        if tail:
            parts.append(jnp.zeros((tm, tail), F32))
        dp_ref[...] = jnp.concatenate(parts, axis=1).astype(BF)
        _acc(dqg_ref, dqg, i == 0)
        _acc(dkg_ref, dkg, i == 0)

    return pl.pallas_call(
        body, grid=(r // tm,),
        in_specs=[_row(tm, w), _row(tm, c), _row(tm, ql), _row(tm, kvl), _row(tm, LANES), _full((1, ql)), _full((1, kvl))],
        out_specs=[_row(tm, w), _full((1, ql)), _full((1, kvl))],
        out_shape=[jax.ShapeDtypeStruct((r, w), BF), jax.ShapeDtypeStruct((1, ql), F32), jax.ShapeDtypeStruct((1, kvl), F32)],
        compiler_params=_cp("arbitrary"), name=name,
    )(p, dgl, dqn, dkvn, dkr, qa_g, kva_g)


def _halo_specs(tm, tc, total_rows, col_off=0, row_off=0):
    hb = HALO_ROWS
    nb = total_rows // hb
    cur = pl.BlockSpec((tm, tc), lambda j, i: (i + row_off // tm, j + col_off))
    prev = pl.BlockSpec((hb, tc), lambda j, i: (jnp.maximum((i * tm + row_off) // hb - 1, 0), j + col_off))
    nxt = pl.BlockSpec((hb, tc), lambda j, i: (jnp.minimum(((i + 1) * tm + row_off) // hb, nb - 1), j + col_off))
    return [prev, cur, nxt]


def _halo_specs_blk(tm, tc, total_rows, pair):
    hb = HALO_ROWS
    nb = total_rows // hb
    head = (2, None) if pair else (None,)
    idx = (lambda j, r: (0, j, r, 0)) if pair else (lambda j, r: (j, r, 0))
    cur = pl.BlockSpec(head + (tm, tc), lambda j, i: idx(j, i))
    prev = pl.BlockSpec(head + (hb, tc), lambda j, i: idx(j, jnp.maximum(i * tm // hb - 1, 0)))
    nxt = pl.BlockSpec(head + (hb, tc), lambda j, i: idx(j, jnp.minimum((i + 1) * tm // hb, nb - 1)))
    return [prev, cur, nxt]


def _ext(prev_ref, cur_ref, next_ref, i, n):
    prev = jnp.where(i == 0, 0.0, prev_ref[...].astype(F32))
    nxt = jnp.where(i == n - 1, 0.0, next_ref[...].astype(F32))
    return jnp.concatenate([prev, cur_ref[...].astype(F32), nxt], axis=-2)


def _sroll(x, shift):
    return pltpu.roll(x, shift % x.shape[0], 0)


def _conv_taps(e, w_ref, ksize, sign):
    pad = (ksize - 1) // 2
    out = None
    for k in range(ksize):
        t = w_ref[k:k + 1, :] * _sroll(e, -sign * (k - pad))
        out = t if out is None else out + t
    return out


def _core(e, tm):
    return e[HALO_ROWS:HALO_ROWS + tm]


def _ln_silu_f(u, g, b):
    mu = jnp.mean(u, axis=-1, keepdims=True)
    xc = u - mu
    y = xc * lax.rsqrt(jnp.mean(xc * xc, axis=-1, keepdims=True) + EPS)
    return jax.nn.silu(y * g + b)


def _conv_ln_silu(gl_all, w, b, ln_g, ln_b, *, s, row_off, tm, name):
    c = gl_all.shape[1]
    ksize = w.shape[0]
    n = s // tm

    def body(p_ref, c_ref, n_ref, w_ref, b_ref, g_ref, lb_ref, u_ref, a_ref):
        i = pl.program_id(1)
        e = _ext(p_ref, c_ref, n_ref, i, n)
        u = _core(_conv_taps(e, w_ref, ksize, 1), tm) + b_ref[...]
        u_ref[...] = u
        a_ref[...] = _ln_silu_f(u, g_ref[...], lb_ref[...]).astype(BF)

    out = pl.BlockSpec((tm, c), lambda j, i: (i, 0))
    return pl.pallas_call(
        body, grid=(1, n),
        in_specs=_halo_specs(tm, c, gl_all.shape[0], row_off=row_off) + [_full((ksize, c)), _full((1, c)), _full((1, c)), _full((1, c))],
        out_specs=[out, out], out_shape=[jax.ShapeDtypeStruct((s, c), F32), jax.ShapeDtypeStruct((s, c), BF)],
        compiler_params=_cp("parallel", "parallel"), name=name,
    )(gl_all, gl_all, gl_all, w, b, ln_g, ln_b)


def _ln_silu_bwd(u, dcat, ln_g, ln_b, *, tm, name):
    s, c = u.shape
    wc = dcat.shape[1]

    def body(u_ref, dc_ref, g_ref, b_ref, du_ref, dg_ref, db_ref):
        i = pl.program_id(0)
        _, vjp = jax.vjp(_ln_silu_f, u_ref[...], g_ref[...], b_ref[...])
        du, dg, db = vjp(dc_ref[:, :c])
        du_ref[...] = du
        _acc(dg_ref, dg, i == 0)
        _acc(db_ref, db, i == 0)

    return pl.pallas_call(
        body, grid=(s // tm,), in_specs=[_row(tm, c), _row(tm, wc), _full((1, c)), _full((1, c))],
        out_specs=[_row(tm, c), _full((1, c)), _full((1, c))],
        out_shape=[jax.ShapeDtypeStruct((s, c), F32), jax.ShapeDtypeStruct((1, c), F32), jax.ShapeDtypeStruct((1, c), F32)],
        compiler_params=_cp("arbitrary"), name=name,
    )(u, dcat, ln_g, ln_b)


def _dw_taps(dw_ref, g_core, e, ksize, tm, first):
    pad = (ksize - 1) // 2

    @pl.when(first)
    def _():
        dw_ref[...] = jnp.zeros_like(dw_ref)

    for k in range(ksize):
        dw_ref[k:k + 1, :] += jnp.sum(g_core * _core(_sroll(e, -(k - pad)), tm), axis=0, keepdims=True)


def _conv_bwd(du, gl_all, w, *, row_off, tm, name):
    s, c = du.shape
    ksize = w.shape[0]
    n = s // tm

    def body(dp_ref, dc_ref, dn_ref, gp_ref, gc_ref, gn_ref, w_ref, dgl_ref, dw_ref, db_ref):
        i = pl.program_id(1)
        de = _ext(dp_ref, dc_ref, dn_ref, i, n)
        ge = _ext(gp_ref, gc_ref, gn_ref, i, n)
        dgl_ref[...] = _core(_conv_taps(de, w_ref, ksize, -1), tm)
        du_core = dc_ref[...]
        _dw_taps(dw_ref, du_core, ge, ksize, tm, i == 0)
        _acc(db_ref, jnp.sum(du_core, axis=0, keepdims=True), i == 0)

    return pl.pallas_call(
        body, grid=(1, n),
        in_specs=_halo_specs(tm, c, s) + _halo_specs(tm, c, gl_all.shape[0], row_off=row_off) + [_full((ksize, c))],
        out_specs=[pl.BlockSpec((tm, c), lambda j, i: (i, 0)), _full((ksize, c)), _full((1, c))],
        out_shape=[jax.ShapeDtypeStruct((s, c), F32), jax.ShapeDtypeStruct((ksize, c), F32), jax.ShapeDtypeStruct((1, c), F32)],
        compiler_params=_cp("arbitrary", "arbitrary"), name=name,
    )(du, du, du, gl_all, gl_all, gl_all, w)


def _ffn_mid(u, w, b, *, tm, name):
    _, nb, s, tc = u.shape
    ksize = w.shape[1]
    n = s // tm

    def body(p_ref, c_ref, n_ref, w_ref, b_ref, z_ref):
        i = pl.program_id(1)
        e = _ext(p_ref, c_ref, n_ref, i, n)
        cg = _core(_conv_taps(e[0], w_ref, ksize, 1), tm) + b_ref[...]
        z_ref[...] = (jax.nn.silu(cg) * c_ref[1]).astype(BF)

    blk = lambda r: pl.BlockSpec((None, r, tc), lambda j, i: (j, 0, 0))
    return pl.pallas_call(
        body, grid=(nb, n), in_specs=_halo_specs_blk(tm, tc, s, True) + [blk(ksize), blk(1)],
        out_specs=pl.BlockSpec((None, tm, tc), lambda j, i: (j, i, 0)), out_shape=jax.ShapeDtypeStruct((nb, s, tc), BF),
        compiler_params=_cp("parallel", "parallel"), name=name,
    )(u, u, u, w, b)


def _ffn_mid_bwd(u, dz, w, b, *, tm, name):
    _, nb, s, tc = u.shape
    ksize = w.shape[1]
    n = s // tm

    def body(up_ref, uc_ref, un_ref, zp_ref, zc_ref, zn_ref, w_ref, b_ref, du_ref, dw_ref, db_ref):
        i = pl.program_id(1)
        ue = _ext(up_ref, uc_ref, un_ref, i, n)
        ge, ve = ue[0], ue[1]
        ze = _ext(zp_ref, zc_ref, zn_ref, i, n)
        cg = _conv_taps(ge, w_ref, ksize, 1) + b_ref[...]
        sg = jax.nn.sigmoid(cg)
        dcg = ze * ve * (sg * (1.0 + cg * (1.0 - sg)))
        du_ref[0] = _core(_conv_taps(dcg, w_ref, ksize, -1), tm).astype(BF)
        du_ref[1] = _core(ze * cg * sg, tm).astype(BF)
        dcg_core = _core(dcg, tm)
        _dw_taps(dw_ref, dcg_core, ge, ksize, tm, i == 0)
        _acc(db_ref, jnp.sum(dcg_core, axis=0, keepdims=True), i == 0)

    blk = lambda r: pl.BlockSpec((None, r, tc), lambda j, i: (j, 0, 0))
    return pl.pallas_call(
        body, grid=(nb, n), in_specs=_halo_specs_blk(tm, tc, s, True) + _halo_specs_blk(tm, tc, s, False) + [blk(ksize), blk(1)],
        out_specs=[pl.BlockSpec((2, None, tm, tc), lambda j, i: (0, j, i, 0)), blk(ksize), blk(1)],
        out_shape=[jax.ShapeDtypeStruct((2, nb, s, tc), BF), jax.ShapeDtypeStruct((nb, ksize, tc), F32), jax.ShapeDtypeStruct((nb, 1, tc), F32)],
        compiler_params=_cp("parallel", "arbitrary"), name=name,
    )(u, u, u, dz, dz, dz, w, b)


def _odd_mid(r3, w, b, *, tm, name):
    s, d3 = r3.shape
    d = d3 // 3
    ksize = w.shape[0]
    tc = _div(d, 1024)
    nc, n = d // tc, s // tm

    def body(bg_ref, cp_ref, cc_ref, cn_ref, up_ref, uc_ref, un_ref, w_ref, b_ref, m_ref):
        i = pl.program_id(1)
        pe = _ext(cp_ref, cc_ref, cn_ref, i, n) * _ext(up_ref, uc_ref, un_ref, i, n)
        sv = _core(_conv_taps(pe, w_ref, ksize, 1), tm) + b_ref[...]
        m_ref[...] = (bg_ref[...] * sv).astype(BF)

    cb = lambda j, i: (0, j)
    return pl.pallas_call(
        body, grid=(nc, n),
        in_specs=[pl.BlockSpec((tm, tc), lambda j, i: (i, j))] + _halo_specs(tm, tc, s, col_off=nc) + _halo_specs(tm, tc, s, col_off=2 * nc)
        + [pl.BlockSpec((ksize, tc), cb), pl.BlockSpec((1, tc), cb)],
        out_specs=pl.BlockSpec((tm, tc), lambda j, i: (i, j)), out_shape=jax.ShapeDtypeStruct((s, d), BF),
        compiler_params=_cp("parallel", "parallel"), name=name,
    )(r3, r3, r3, r3, r3, r3, r3, w, b)


def _odd_mid_bwd(r3, dm, w, b, *, tm, name):
    s, d3 = r3.shape
    d = d3 // 3
    ksize = w.shape[0]
    tc = _div(d, 1024)
    nc, n = d // tc, s // tm

    def body(bp_ref, bc_ref, bn_ref, cp_ref, cc_ref, cn_ref, up_ref, uc_ref, un_ref, mp_ref, mc_ref, mn_ref, w_ref, b_ref,
             dbg_ref, dcg_ref, du_ref, dw_ref, db_ref):
        i = pl.program_id(1)
        be = _ext(bp_ref, bc_ref, bn_ref, i, n)
        ce = _ext(cp_ref, cc_ref, cn_ref, i, n)
        ue = _ext(up_ref, uc_ref, un_ref, i, n)
        me = _ext(mp_ref, mc_ref, mn_ref, i, n)
        pe = ce * ue
        sv = _conv_taps(pe, w_ref, ksize, 1) + b_ref[...]
        ds = me * be
        dp = _core(_conv_taps(ds, w_ref, ksize, -1), tm)
        dbg_ref[...] = _core(me * sv, tm).astype(BF)
        dcg_ref[...] = (dp * uc_ref[...]).astype(BF)
        du_ref[...] = (dp * cc_ref[...]).astype(BF)
        ds_core = _core(ds, tm)
        _dw_taps(dw_ref, ds_core, pe, ksize, tm, i == 0)
        _acc(db_ref, jnp.sum(ds_core, axis=0, keepdims=True), i == 0)

    cb = lambda j, i: (0, j)
    col = pl.BlockSpec((tm, tc), lambda j, i: (i, j))
    return pl.pallas_call(
        body, grid=(nc, n),
        in_specs=_halo_specs(tm, tc, s) + _halo_specs(tm, tc, s, col_off=nc) + _halo_specs(tm, tc, s, col_off=2 * nc) + _halo_specs(tm, tc, s)
        + [pl.BlockSpec((ksize, tc), cb), pl.BlockSpec((1, tc), cb)],
        out_specs=[col, col, col, pl.BlockSpec((ksize, tc), cb), pl.BlockSpec((1, tc), cb)],
        out_shape=[jax.ShapeDtypeStruct((s, d), BF)] * 3 + [jax.ShapeDtypeStruct((ksize, d), F32), jax.ShapeDtypeStruct((1, d), F32)],
        compiler_params=_cp("parallel", "arbitrary"), name=name,
    )(r3, r3, r3, r3, r3, r3, r3, r3, r3, dm, dm, dm, w, b)


def _gate_bwd(dx, f, g, *, tm, name):
    s, d = dx.shape

    def body(dx_ref, f_ref, g_ref, df_ref, dg_ref):
        i = pl.program_id(0)
        dxv = dx_ref[...]
        df_ref[...] = (dxv * g_ref[...]).astype(BF)
        _acc(dg_ref, jnp.sum(dxv * f_ref[...], axis=0, keepdims=True), i == 0)

    return pl.pallas_call(
        body, grid=(s // tm,), in_specs=[_row(tm, d), _row(tm, d), _full((1, d))], out_specs=[_row(tm, d), _full((1, d))],
        out_shape=[jax.ShapeDtypeStruct((s, d), BF), jax.ShapeDtypeStruct((1, d), F32)], compiler_params=_cp("arbitrary"), name=name,
    )(dx, f, g)


def _loss_head(y, target, *, tm, name):
    s, d = y.shape

    def body(y_ref, t_ref, dy_ref, l_ref):
        i = pl.program_id(0)
        err = y_ref[...] - t_ref[...]
        dy_ref[...] = err * (1.0 / d)
        _acc(l_ref, jnp.full((1, LANES), 0.5 / d, F32) * jnp.sum(err * err), i == 0)

    return pl.pallas_call(
        body, grid=(s // tm,), in_specs=[_row(tm, d), _row(tm, d)], out_specs=[_row(tm, d), _full((1, LANES))],
        out_shape=[jax.ShapeDtypeStruct((s, d), F32), jax.ShapeDtypeStruct((1, LANES), F32)], compiler_params=_cp("arbitrary"), name=name,
    )(y, target)


def _rope(n, cos, sa, sb):
    return n * cos + _lane_roll(LANES - 8)(n) * sa + _lane_roll(8)(n) * sb


def _q_post_f(q, g, cos, sa, sb):
    outs = []
    for h in range(q.shape[1] // LANES):
        outs.append(_rope(_rms(q[:, h * LANES:(h + 1) * LANES], QK_DIM) * g, cos, sa, sb) * Q_PRESCALE)
    return jnp.concatenate(outs, axis=1)


def _kv_post_f(kv, kr, g, cos, sa, sb):
    lane = lax.broadcasted_iota(jnp.int32, (1, LANES), 1)
    krs = jnp.where((lane >= QK_NOPE) & (lane < QK_DIM), _lane_roll(QK_NOPE)(kr), 0.0)
    ks, vs = [], []
    for h in range(kv.shape[1] // LANES):
        kvh = kv[:, h * LANES:(h + 1) * LANES]
        ks.append(_rope(_rms(jnp.where(lane < QK_NOPE, kvh, krs), QK_DIM) * g, cos, sa, sb))
        vs.append(jnp.where(lane < V_DIM, _lane_roll(LANES - QK_NOPE)(kvh), 0.0))
    return jnp.concatenate(ks, axis=1), jnp.concatenate(vs, axis=1)


def _tab_specs(tm):
    return [_row(tm, LANES)] * 3


def _q_post(q, g, tabs, *, tm, name):
    r, w = q.shape

    def body(q_ref, g_ref, c_ref, sa_ref, sb_ref, o_ref):
        o_ref[...] = _q_post_f(q_ref[...], g_ref[...], c_ref[...], sa_ref[...], sb_ref[...]).astype(BF)

    return pl.pallas_call(
        body, grid=(r // tm,), in_specs=[_row(tm, w), _full((1, LANES))] + _tab_specs(tm), out_specs=_row(tm, w),
        out_shape=jax.ShapeDtypeStruct((r, w), BF), compiler_params=_cp("parallel"), name=name,
    )(q, g, *tabs)


def _q_post_bwd(q, dqh, g, tabs, *, tm, name):
    r, w = q.shape

    def body(q_ref, d_ref, g_ref, c_ref, sa_ref, sb_ref, dq_ref, dg_ref):
        i = pl.program_id(0)
        f = lambda qv, gv: _q_post_f(qv, gv, c_ref[...], sa_ref[...], sb_ref[...])
        _, vjp = jax.vjp(f, q_ref[...], g_ref[...])
        dq, dg = vjp(d_ref[...])
        dq_ref[...] = dq.astype(BF)
        _acc(dg_ref, dg, i == 0)

    return pl.pallas_call(
        body, grid=(r // tm,), in_specs=[_row(tm, w), _row(tm, w), _full((1, LANES))] + _tab_specs(tm),
        out_specs=[_row(tm, w), _full((1, LANES))],
        out_shape=[jax.ShapeDtypeStruct((r, w), BF), jax.ShapeDtypeStruct((1, LANES), F32)], compiler_params=_cp("arbitrary"), name=name,
    )(q, dqh, g, *tabs)


def _kr_spec(tm, w):
    return pl.BlockSpec((tm, LANES), lambda i: (i, w // LANES - 1))


def _kv_post(kv, p, g, tabs, *, tm, name):
    r, w = kv.shape

    def body(kv_ref, kr_ref, g_ref, c_ref, sa_ref, sb_ref, k_ref, v_ref):
        k, v = _kv_post_f(kv_ref[...], kr_ref[...], g_ref[...], c_ref[...], sa_ref[...], sb_ref[...])
        k_ref[...] = k.astype(BF)
        v_ref[...] = v.astype(BF)

    return pl.pallas_call(
        body, grid=(r // tm,), in_specs=[_row(tm, w), _kr_spec(tm, p.shape[1]), _full((1, LANES))] + _tab_specs(tm),
        out_specs=[_row(tm, w), _row(tm, w)], out_shape=[jax.ShapeDtypeStruct((r, w), BF)] * 2,
        compiler_params=_cp("parallel"), name=name,
    )(kv, p, g, *tabs)


def _kv_post_bwd(kv, p, dk, dv, g, tabs, *, tm, name):
    r, w = kv.shape

    def body(kv_ref, kr_ref, dk_ref, dv_ref, g_ref, c_ref, sa_ref, sb_ref, dkv_ref, dkr_ref, dg_ref):
        i = pl.program_id(0)
        f = lambda kvv, krv, gv: _kv_post_f(kvv, krv, gv, c_ref[...], sa_ref[...], sb_ref[...])
        _, vjp = jax.vjp(f, kv_ref[...], kr_ref[...], g_ref[...])
        dkv, dkr, dg = vjp((dk_ref[...], dv_ref[...]))
        dkv_ref[...] = dkv.astype(BF)
        dkr_ref[...] = dkr
        _acc(dg_ref, dg, i == 0)

    return pl.pallas_call(
        body, grid=(r // tm,),
        in_specs=[_row(tm, w), _kr_spec(tm, p.shape[1]), _row(tm, w), _row(tm, w), _full((1, LANES))] + _tab_specs(tm),
        out_specs=[_row(tm, w), _row(tm, LANES), _full((1, LANES))],
        out_shape=[jax.ShapeDtypeStruct((r, w), BF), jax.ShapeDtypeStruct((r, LANES), F32), jax.ShapeDtypeStruct((1, LANES), F32)],
        compiler_params=_cp("arbitrary"), name=name,
    )(kv, p, dk, dv, g, *tabs)


_NT = (((1,), (1,)), ((), ()))
_TN = (((0,), (0,)), ((), ()))
_NN = (((1,), (0,)), ((), ()))


ATTN_TQ, ATTN_TK = 512, 768
Q_PRESCALE = SM_SCALE * 1.4426950408889634
LN2 = 0.6931471805599453


def _attn_tiles(s, nk):
    return _div(s, ATTN_TQ), _div(nk, ATTN_TK)


def _chunk(ref, i, n):
    return ref[pl.ds(pl.multiple_of(i * n, n), n), :]


def _attn_fwd(q, k, v, *, name):
    s, w = q.shape
    nk = k.shape[0]
    tq, tk = _attn_tiles(s, nk)
    nj = nk // tk

    def body(q_ref, k_ref, v_ref, o_ref, lse_ref, m_s, l_s, acc_s):
        m_s[...] = jnp.full_like(m_s, -jnp.inf)
        l_s[...] = jnp.zeros_like(l_s)
        acc_s[...] = jnp.zeros_like(acc_s)
        qv = q_ref[...]

        def step(j, carry):
            sc = lax.dot_general(qv, _chunk(k_ref, j, tk), _NT, preferred_element_type=F32)
            m_prev = m_s[...]
            m_new = jnp.maximum(m_prev, jnp.max(sc, axis=1, keepdims=True))
            alpha = jnp.exp2(m_prev - m_new)
            pr = jnp.exp2(sc - m_new[:, :1])
            l_s[...] = alpha * l_s[...] + jnp.sum(pr, axis=1, keepdims=True)
            acc_s[...] = alpha * acc_s[...] + lax.dot_general(pr.astype(BF), _chunk(v_ref, j, tk), _NN, preferred_element_type=F32)
            m_s[...] = m_new
            return carry

        lax.fori_loop(0, nj, step, 0, unroll=True)
        o_ref[...] = (acc_s[...] / l_s[...]).astype(BF)
        lse_ref[...] = m_s[...] + jnp.log2(l_s[...])

    qs = pl.BlockSpec((tq, LANES), lambda h, i: (i, h))
    ks = pl.BlockSpec((nk, LANES), lambda h, i: (0, h))
    return pl.pallas_call(
        body, grid=(w // LANES, s // tq), in_specs=[qs, ks, ks], out_specs=[qs, qs],
        out_shape=[jax.ShapeDtypeStruct((s, w), BF), jax.ShapeDtypeStruct((s, w), F32)],
        scratch_shapes=[pltpu.VMEM((tq, LANES), F32)] * 3, compiler_params=_cp("parallel", "parallel"), name=name,
    )(q, k, v)


def _attn_delta(dcat, o, *, tm, name):
    s, w = o.shape
    wc = dcat.shape[1]

    def body(dc_ref, o_ref, do_ref, dl_ref):
        do = dc_ref[:, wc - w:]
        prod = do * o_ref[...].astype(F32)
        outs = []
        for h in range(w // LANES):
            outs.append(jnp.broadcast_to(jnp.sum(prod[:, h * LANES:(h + 1) * LANES], axis=1, keepdims=True), (tm, LANES)))
        do_ref[...] = do.astype(BF)
        dl_ref[...] = jnp.concatenate(outs, axis=1)

    return pl.pallas_call(
        body, grid=(s // tm,), in_specs=[_row(tm, wc), _row(tm, w)], out_specs=[_row(tm, w), _row(tm, w)],
        out_shape=[jax.ShapeDtypeStruct((s, w), BF), jax.ShapeDtypeStruct((s, w), F32)], compiler_params=_cp("parallel"), name=name,
    )(dcat, o)


def _attn_bwd(q, k, v, do, lse, delta, *, name):
    s, w = q.shape
    nk = k.shape[0]
    tq, tk = _attn_tiles(s, nk)
    ni, nj = s // tq, nk // tk

    def body(q_ref, k_ref, v_ref, do_ref, lse_ref, dl_ref, dq_ref, dk_ref, dv_ref, dk_s, dv_s):
        j = pl.program_id(1)

        @pl.when(j == 0)
        def _():
            dq_ref[...] = jnp.zeros_like(dq_ref)

        dk_s[...] = jnp.zeros_like(dk_s)
        dv_s[...] = jnp.zeros_like(dv_s)
        kv, vv = k_ref[...], v_ref[...]

        def step(i, carry):
            rows = pl.ds(pl.multiple_of(i * tq, tq), tq)
            qi, doi = q_ref[rows, :], do_ref[rows, :]
            sc = lax.dot_general(qi, kv, _NT, preferred_element_type=F32)
            pr = jnp.exp2(sc - lse_ref[rows, :][:, :1])
            dp = lax.dot_general(doi, vv, _NT, preferred_element_type=F32)
            ds = (pr * (dp - dl_ref[rows, :][:, :1])).astype(BF)
            dv_s[...] += lax.dot_general(pr.astype(BF), doi, _TN, preferred_element_type=F32)
            dk_s[...] += lax.dot_general(ds, qi, _TN, preferred_element_type=F32)
            dq_ref[rows, :] += lax.dot_general(ds, kv, _NN, preferred_element_type=F32)
            return carry

        lax.fori_loop(0, ni, step, 0, unroll=2 if ni % 2 == 0 else 1)
        dk_ref[...] = dk_s[...] * LN2
        dv_ref[...] = dv_s[...]

        @pl.when(j == nj - 1)
        def _():
            dq_ref[...] = dq_ref[...] * LN2

    qs = pl.BlockSpec((s, LANES), lambda h, j: (0, h))
    ks = pl.BlockSpec((tk, LANES), lambda h, j: (j, h))
    return pl.pallas_call(
        body, grid=(w // LANES, nj), in_specs=[qs, ks, ks, qs, qs, qs], out_specs=[qs, ks, ks],
        out_shape=[jax.ShapeDtypeStruct((s, w), F32), jax.ShapeDtypeStruct((nk, w), F32), jax.ShapeDtypeStruct((nk, w), F32)],
        scratch_shapes=[pltpu.VMEM((tk, LANES), F32)] * 2, compiler_params=_cp("parallel", "arbitrary"), name=name,
    )(q, k, v, do, lse, delta)


def _adamw(w, g, m, v, *, name):
    r, c = w.shape
    tr = r
    for cand in (512, 256, 128, 64, 32, 16, 8):
        if r % cand == 0 and cand * c * 4 <= (2 << 20):
            tr = cand
            break
    bc1 = 1.0 - ADAM_B1 ** ADAM_STEP
    bc2 = 1.0 - ADAM_B2 ** ADAM_STEP

    def body(w_ref, g_ref, m_ref, v_ref, d_ref, nm_ref, nv_ref):
        gv = g_ref[...]
        nm = ADAM_B1 * m_ref[...] + (1.0 - ADAM_B1) * gv
        nv = ADAM_B2 * v_ref[...] + (1.0 - ADAM_B2) * (gv * gv)
        d_ref[...] = -ADAM_LR * ((nm / bc1) / (jnp.sqrt(nv / bc2) + ADAM_EPS) + ADAM_WD * w_ref[...])
        nm_ref[...] = nm
        nv_ref[...] = nv

    spec = _row(tr, c)
    return pl.pallas_call(
        body, grid=(r // tr,), in_specs=[spec] * 4, out_specs=[spec] * 3,
        out_shape=[jax.ShapeDtypeStruct((r, c), F32)] * 3, compiler_params=_cp("parallel"), name=name,
    )(w, g, m, v)


def _mesh_pos():
    return lax.axis_index("x"), lax.axis_index("y"), lax.axis_index("c")


def _allgather(x, *, in_vmem, name):
    r, c = x.shape
    spec = pl.BlockSpec(memory_space=pltpu.VMEM if in_vmem else pl.ANY)

    def body(x_ref, out_ref, send_sems, recv_sems, local_sem):
        ix, iy, ic = _mesh_pos()
        me, sibling = (ix, iy, ic), (ix, iy, 1 - ic)
        chips = [(1 - ix, iy), (ix, 1 - iy), (1 - ix, 1 - iy)]

        def slab(px, py, pc):
            return out_ref.at[4 * px + 2 * py + pc]

        def copy(k, block, to, src=None):
            return pltpu.make_async_remote_copy(
                src_ref=slab(*block) if src is None else src, dst_ref=slab(*block),
                send_sem=send_sems.at[k], recv_sem=recv_sems.at[k], device_id=to, device_id_type=pl.DeviceIdType.MESH)

        mine = pltpu.make_async_copy(x_ref, slab(*me), local_sem)
        mine.start()
        first = [copy(0, me, sibling, src=x_ref)]
        first += [copy(1 + j, me, (*chip, ic), src=x_ref) for j, chip in enumerate(chips)]
        for cp in first:
            cp.start()
        passed = [copy(4 + j, (*chip, ic), sibling) for j, chip in enumerate(chips)]
        for j, chip in enumerate(chips):
            copy(1 + j, (*chip, ic), me).wait_recv()
            passed[j].start()
        copy(0, sibling, me).wait_recv()
        for j, chip in enumerate(chips):
            copy(4 + j, (*chip, 1 - ic), me).wait_recv()
        for cp in first + passed:
            cp.wait_send()
        mine.wait()

    return pl.pallas_call(
        body, out_shape=jax.ShapeDtypeStruct((N_DEV, r, c), x.dtype), in_specs=[spec], out_specs=spec,
        scratch_shapes=[pltpu.SemaphoreType.DMA((7,)), pltpu.SemaphoreType.DMA((7,)), pltpu.SemaphoreType.DMA], name=name,
    )(x)


def _exchange(g, *, name):
    _, r, c = g.shape
    spec = pl.BlockSpec(memory_space=pl.ANY)

    def body(g_ref, out_ref, send_sems, recv_sems, local_sem):
        ix, iy, ic = _mesh_pos()
        me = 4 * ix + 2 * iy + ic
        mine = pltpu.make_async_copy(g_ref.at[me], out_ref.at[me], local_sem)
        mine.start()
        sends, recvs = [], []
        for k in range(1, N_DEV):
            px = 1 - ix if k & 4 else ix
            py = 1 - iy if k & 2 else iy
            pc = 1 - ic if k & 1 else ic
            peer = 4 * px + 2 * py + pc
            mk = lambda src, dst: pltpu.make_async_remote_copy(
                src_ref=g_ref.at[src], dst_ref=out_ref.at[dst], send_sem=send_sems.at[k - 1], recv_sem=recv_sems.at[k - 1],
                device_id=(px, py, pc), device_id_type=pl.DeviceIdType.MESH)
            sends.append(mk(peer, me))
            recvs.append(mk(me, peer))
        for cp in sends:
            cp.start()
        for cp in recvs:
            cp.wait_recv()
        for cp in sends:
            cp.wait_send()
        mine.wait()

    return pl.pallas_call(
        body, out_shape=jax.ShapeDtypeStruct(g.shape, g.dtype), in_specs=[spec], out_specs=spec,
        scratch_shapes=[pltpu.SemaphoreType.DMA((7,)), pltpu.SemaphoreType.DMA((7,)), pltpu.SemaphoreType.DMA], name=name,
    )(g)


def _sum8(a, *, name):
    _, r, c = a.shape
    tr = r
    for cand in (512, 256, 128, 64, 32, 16):
        if r % cand == 0 and cand * c * 4 <= (1 << 20):
            tr = cand
            break

    def body(a_ref, o_ref):
        acc = a_ref[0].astype(F32)
        for d in range(1, N_DEV):
            acc = acc + a_ref[d].astype(F32)
        o_ref[...] = acc

    return pl.pallas_call(
        body, grid=(r // tr,), in_specs=[pl.BlockSpec((N_DEV, tr, c), lambda i: (0, i, 0))], out_specs=_row(tr, c),
        out_shape=jax.ShapeDtypeStruct((r, c), F32), compiler_params=_cp("parallel"), name=name,
    )(a)


ADA_ROWS = 16


def _silu_rows(c8, c_ctx):
    d = c8.shape[1]
    rows = jnp.concatenate([c8, c_ctx, jnp.zeros((ADA_ROWS - N_DEV - 1, d), F32)], axis=0)
    return jax.nn.silu(rows)


def _ada_fwd(c8, c_ctx, ada_w, ada_b_cols, *, name):
    nl, d, cols = ada_w.shape

    def body(c8_ref, cc_ref, w_ref, b_ref, o_ref):
        sc = _silu_rows(c8_ref[...], cc_ref[...]).astype(BF)
        for l in range(nl):
            o_ref[l] = lax.dot_general(sc, w_ref[l].astype(BF), _NN, preferred_element_type=F32) + b_ref[l:l + 1, :]

    return pl.pallas_call(
        body, out_shape=jax.ShapeDtypeStruct((nl, ADA_ROWS, cols), F32),
        compiler_params=pltpu.CompilerParams(vmem_limit_bytes=VMEM_LIMIT), name=name,
    )(c8, c_ctx, ada_w, ada_b_cols)


def _ada_bwd(c8, c_ctx, ada_w, g16, dctx_cols, tot_dm, dmodc_pad, *, name):
    nl, d, cols = ada_w.shape
    hi = lax.Precision.HIGHEST

    def body(c8_ref, cc_ref, w_ref, g_ref, dc_ref, tot_ref, dmc_ref, dw_ref, db_ref, part_ref):
        sc = _silu_rows(c8_ref[...], cc_ref[...])
        for l in range(nl):
            dw_ref[l] = lax.dot_general(sc, g_ref[l], _TN, precision=hi, preferred_element_type=F32)
        db_ref[...] = tot_ref[...]
        db_ref[0:1, :] += dmc_ref[...]
        ccv = cc_ref[...]
        sg = jax.nn.sigmoid(ccv)
        dsilu = sg * (1.0 + ccv * (1.0 - sg))
        part = lax.dot_general(dc_ref[...], w_ref[0], _NT, precision=hi, preferred_element_type=F32) * dsilu
        part_ref[...] = jnp.concatenate([part, jnp.zeros((SUBLANES - 1, d), F32)], axis=0)

    return pl.pallas_call(
        body, out_shape=[jax.ShapeDtypeStruct((nl, d, cols), F32), jax.ShapeDtypeStruct(tot_dm.shape, F32),
                         jax.ShapeDtypeStruct((SUBLANES, d), F32)],
        compiler_params=pltpu.CompilerParams(vmem_limit_bytes=VMEM_LIMIT), name=name,
    )(c8, c_ctx, ada_w, g16, dctx_cols, tot_dm, dmodc_pad)


def _rope_tables(s, lc):
    t = jnp.arange(s)
    half = QK_ROPE // 2
    inv = ROPE_THETA ** (-jnp.arange(0, half, 2, dtype=F32) / half)
    ang_r = (t // GRID_W).astype(F32)[:, None] * inv[None, :]
    ang_c = (t % GRID_W).astype(F32)[:, None] * inv[None, :]
    ang = jnp.concatenate([ang_r, ang_r, ang_c, ang_c], axis=-1)
    cos, sin = jnp.cos(ang), jnp.sin(ang)
    first = (jnp.arange(QK_ROPE) % half) < half // 2
    sa, sb = jnp.where(first, -sin, 0.0), jnp.where(first, 0.0, sin)

    def slot(mid, fill):
        body = jnp.concatenate([jnp.full((s, QK_NOPE), fill, F32), mid, jnp.full((s, LANES - QK_DIM), fill, F32)], axis=1)
        return jnp.concatenate([jnp.full((lc, LANES), fill, F32), body], axis=0)

    return slot(cos, 1.0), slot(sa, 0.0), slot(sb, 0.0)


def _pad_last(a, n):
    return jnp.pad(a, [(0, 0)] * (a.ndim - 1) + [(0, n - a.shape[-1])])


def _local_step(x, ctx, target, mods, modc, w):
    s, d = x.shape
    lc = ctx.shape[0]
    c = d // 2
    nh = (d - c) // V_DIM
    hw = nh * LANES
    ql, kvl = w["ev_qa_norm_g"].shape[-1], w["ev_kva_norm_g"].shape[-1]
    ei = 2 * c + ql + kvl + QK_ROPE
    eip = 2 * c + ql + kvl + LANES
    tm = 256 if (s % 256 == 0 and lc % 256 == 0) else 128
    nctx = lc // tm
    row = lambda v: v.reshape(1, -1)
    cls1 = lambda v: v.reshape(1, 1, -1)

    w_inp = _pad_last(w["ev_w_in"][0], eip)
    w_uqp = _pad_last(w["ev_w_uq"][0].reshape(ql, nh, QK_DIM), LANES).reshape(ql, hw)
    w_ukv = w["ev_w_ukv"][0]
    w_out = w["ev_w_out"][0]
    w_att = jnp.pad(w_out[c:].reshape(nh, V_DIM, d), [(0, 0), (0, LANES - V_DIM), (0, 0)]).reshape(hw, d)
    w_outp = jnp.concatenate([w_out[:c], w_att], axis=0)
    qg = _pad_last(row(w["ev_q_norm_g"]), LANES)
    kg = _pad_last(row(w["ev_k_norm_g"]), LANES)
    qa_g, kva_g = row(w["ev_qa_norm_g"]), row(w["ev_kva_norm_g"])
    ev_cw, ev_cb = w["ev_conv_w"][0], row(w["ev_conv_b"])
    ln_g, ln_b = row(w["ev_ln_g"]), row(w["ev_ln_b"])
    tabs = _rope_tables(s, lc)

    xall = jnp.concatenate([ctx, x], axis=0)
    sc0 = jnp.stack([modc[1], mods[0, 1]])[:, None, :]
    sh0 = jnp.stack([modc[0], mods[0, 0]])[:, None, :]
    g_mix0 = row(w["norm_mix_g"][0])
    hall = _modnorm(xall, g_mix0, sc0, sh0, nctx=nctx, tm=tm, name="l0_mix_norm")
    p_all = _mm(hall, w_inp, name="l0_w_in")
    gl_all, qn_all, kvn_all = _even_mid(p_all, qa_g, kva_g, c=c, tm=tm, name="l0_even_mid")
    u, a = _conv_ln_silu(gl_all, ev_cw, ev_cb, ln_g, ln_b, s=s, row_off=lc, tm=tm, name="l0_conformer")
    q_all = _mm(qn_all, w_uqp, name="l0_w_uq")
    kv_all = _mm(kvn_all, w_ukv, name="l0_w_ukv")
    qh_all = _q_post(q_all, qg, tabs, tm=tm, name="l0_q_post")
    k_all, v_all = _kv_post(kv_all, p_all, kg, tabs, tm=tm, name="l0_kv_post")
    qh = qh_all[lc:]
    o, lse = _attn_fwd(qh, k_all, v_all, name="l0_attn_fwd")
    cat = jnp.concatenate([a, o], axis=1)
    x1, f_mix0 = _mm(cat, w_outp, res=x, gate=row(mods[0, 2]), name="l0_w_out")

    nb = N_DEV // 2
    ffn_dim = w["ffn_conv_b"].shape[-1]
    tc = ffn_dim // nb
    ffn_k = w["ffn_conv_w"].shape[1]
    ffn_cw = [jnp.transpose(w["ffn_conv_w"][l].reshape(ffn_k, nb, tc), (1, 0, 2)) for l in range(2)]
    ffn_cb = [w["ffn_conv_b"][l].reshape(nb, 1, tc) for l in range(2)]

    def ffn_fwd(l, x_in):
        hf = _modnorm(x_in, row(w["norm_ffn_g"][l]), cls1(mods[l, 4]), cls1(mods[l, 3]), nctx=0, tm=tm, name=f"l{l}_ffn_norm")
        uu = _bmm(hf, w["ffn_w_up8"][l], b_blk="o", name=f"l{l}_w_up").reshape(2, nb, s, tc)
        z = _ffn_mid(uu, ffn_cw[l], ffn_cb[l], tm=tm, name=f"l{l}_ffn_mid")
        x_out, f = _bmm(z, w["ffn_w_down4"][l], a_blk="k", b_blk="k", res=x_in, gate=row(mods[l, 5]), name=f"l{l}_w_down")
        return x_out, (x_in, hf, uu, z, f)

    x2, ffn0 = ffn_fwd(0, x1)
    h1 = _modnorm(x2, row(w["norm_mix_g"][1]), cls1(mods[1, 1]), cls1(mods[1, 0]), nctx=0, tm=tm, name="l1_mix_norm")
    r3 = _mm(h1, w["od_w_in"][0], name="l1_w_in")
    od_cw, od_cb = w["od_conv_w"][0], row(w["od_conv_b"])
    m1 = _odd_mid(r3, od_cw, od_cb, tm=tm, name="l1_odd_mid")
    x3, f_mix1 = _mm(m1, w["od_w_out"][0], res=x2, gate=row(mods[1, 2]), name="l1_w_out")
    x4, ffn1 = ffn_fwd(1, x3)
    dx, loss_row = _loss_head(x4, target, tm=tm, name="loss_head")

    g = {}
    dmods = [[None] * N_MOD for _ in range(2)]

    def ffn_bwd(l, dx, saved):
        x_in, hf, uu, z, f = saved
        df, dmods[l][5] = _gate_bwd(dx, f, row(mods[l, 5]), tm=tm, name=f"l{l}_ffn_gate_bwd")
        dz = _bmm(df, w["ffn_w_down4"][l], b_blk="o", tb=True, name=f"l{l}_w_down_dx")
        dwd = _bmm(z, df, a_blk="o", ta=True, out_dtype=BF, name=f"l{l}_w_down_dw")
        du, dcw, dcb = _ffn_mid_bwd(uu, dz, ffn_cw[l], ffn_cb[l], tm=tm, name=f"l{l}_ffn_mid_bwd")
        du = du.reshape(N_DEV, s, tc)
        dhf = _bmm(du, w["ffn_w_up8"][l], a_blk="k", b_blk="k", tb=True, name=f"l{l}_w_up_dx")
        dwu = _bmm(hf, du, b_blk="o", ta=True, out_dtype=BF, name=f"l{l}_w_up_dw")
        dx, dgn, dsc, dsh = _modnorm_bwd(x_in, dhf, dx, row(w["norm_ffn_g"][l]), cls1(mods[l, 4]), cls1(mods[l, 3]),
                                         nctx=0, tm=tm, name=f"l{l}_ffn_norm_bwd")
        dmods[l][4], dmods[l][3] = dsc, dsh
        return dx, dict(ffn_w_up8=dwu, ffn_w_down8=dwd.reshape(N_DEV, tc // 2, d),
                        ffn_conv_w=jnp.transpose(dcw, (1, 0, 2)).reshape(ffn_k, ffn_dim), ffn_conv_b=dcb.reshape(1, ffn_dim), norm_ffn_g=dgn)

    dx, gf1 = ffn_bwd(1, dx, ffn1)
    df, dmods[1][2] = _gate_bwd(dx, f_mix1, row(mods[1, 2]), tm=tm, name="l1_mix_gate_bwd")
    dm1 = _mm(df, w["od_w_out"][0], tb=True, name="l1_w_out_dx")
    g["od_w_out"] = _mm(m1, df, ta=True, name="l1_w_out_dw")[None]
    dbg, dcg, duu, dcw, dcb = _odd_mid_bwd(r3, dm1, od_cw, od_cb, tm=tm, name="l1_odd_mid_bwd")
    g["od_conv_w"], g["od_conv_b"] = dcw[None], dcb
    dr3 = jnp.concatenate([dbg, dcg, duu], axis=1)
    dh1 = _mm(dr3, w["od_w_in"][0], tb=True, name="l1_w_in_dx")
    g["od_w_in"] = _mm(h1, dr3, ta=True, name="l1_w_in_dw")[None]
    dx, dgn1, dsc, dsh = _modnorm_bwd(x2, dh1, dx, row(w["norm_mix_g"][1]), cls1(mods[1, 1]), cls1(mods[1, 0]),
                                      nctx=0, tm=tm, name="l1_mix_norm_bwd")
    dmods[1][1], dmods[1][0] = dsc, dsh
    dx, gf0 = ffn_bwd(0, dx, ffn0)
    df, dmods[0][2] = _gate_bwd(dx, f_mix0, row(mods[0, 2]), tm=tm, name="l0_mix_gate_bwd")
    dcat = _mm(df, w_outp, tb=True, name="l0_w_out_dx")
    dw_outp = _mm(cat, df, ta=True, name="l0_w_out_dw")
    du0, g["ev_ln_g"], g["ev_ln_b"] = _ln_silu_bwd(u, dcat, ln_g, ln_b, tm=tm, name="l0_ln_silu_bwd")
    dgl, dcw, g["ev_conv_b"] = _conv_bwd(du0, gl_all, ev_cw, row_off=lc, tm=tm, name="l0_conformer_conv_bwd")
    g["ev_conv_w"] = dcw[None]
    do, delta = _attn_delta(dcat, o, tm=tm, name="l0_attn_delta")
    dq, dk, dv = _attn_bwd(qh, k_all, v_all, do, lse, delta, name="l0_attn_bwd")
    dq_all = jnp.concatenate([jnp.zeros((lc, hw), F32), dq], axis=0)
    dgl_all = jnp.concatenate([jnp.zeros((lc, c), F32), dgl], axis=0)
    dqp, dqg = _q_post_bwd(q_all, dq_all, qg, tabs, tm=tm, name="l0_q_post_bwd")
    dkvp, dkr, dkg = _kv_post_bwd(kv_all, p_all, dk, dv, kg, tabs, tm=tm, name="l0_kv_post_bwd")
    dqn = _mm(dqp, w_uqp, tb=True, name="l0_w_uq_dx")
    dw_uqp = _mm(qn_all, dqp, ta=True, name="l0_w_uq_dw")
    dkvn = _mm(dkvp, w_ukv, tb=True, name="l0_w_ukv_dx")
    g["ev_w_ukv"] = _mm(kvn_all, dkvp, ta=True, name="l0_w_ukv_dw")[None]
    dp, g["ev_qa_norm_g"], g["ev_kva_norm_g"] = _even_mid_bwd(p_all, dgl_all, dqn, dkvn, dkr, qa_g, kva_g, c=c, tm=tm, name="l0_even_mid_bwd")
    dhall = _mm(dp, w_inp, tb=True, name="l0_w_in_dx")
    dw_inp = _mm(hall, dp, ta=True, name="l0_w_in_dw")
    dx, dgn0, dsc2, dsh2 = _modnorm_bwd(xall, dhall, dx, g_mix0, sc0, sh0, nctx=nctx, tm=tm, name="l0_mix_norm_bwd")
    dmods[0][1], dmods[0][0] = dsc2[1], dsh2[1]
    dmodc = jnp.concatenate([dsh2[0], dsc2[0]], axis=0)

    g["ev_w_in"] = dw_inp[:, :ei][None]
    g["ev_w_uq"] = dw_uqp.reshape(ql, nh, LANES)[:, :, :QK_DIM].reshape(ql, nh * QK_DIM)[None]
    g["ev_w_out"] = jnp.concatenate([dw_outp[:c], dw_outp[c:].reshape(nh, LANES, d)[:, :V_DIM].reshape(nh * V_DIM, d)], axis=0)[None]
    g["ev_q_norm_g"], g["ev_k_norm_g"] = dqg[:, :QK_DIM], dkg[:, :QK_DIM]
    g["norm_mix_g"] = jnp.concatenate([dgn0, dgn1], axis=0)
    for name in ("ffn_w_up8", "ffn_w_down8"):
        g[name] = [gf0[name], gf1[name]]
    g["ffn_conv_w"] = jnp.stack([gf0["ffn_conv_w"], gf1["ffn_conv_w"]])
    for name in ("ffn_conv_b", "norm_ffn_g"):
        g[name] = jnp.concatenate([gf0[name], gf1[name]], axis=0)
    dmods_arr = jnp.stack([jnp.concatenate([v.reshape(1, d) for v in dmods[l]], axis=0) for l in range(2)])
    return loss_row, dx, g, dmods_arr, dmodc


WEIGHTS = ("c_ctx", "ada_w", "ada_b", "norm_mix_g", "norm_ffn_g", "ffn_w_up", "ffn_conv_w", "ffn_conv_b", "ffn_w_down", "ev_w_in",
           "ev_conv_w", "ev_conv_b", "ev_ln_g", "ev_ln_b", "ev_qa_norm_g", "ev_w_uq", "ev_kva_norm_g", "ev_w_ukv", "ev_q_norm_g",
           "ev_k_norm_g", "ev_w_out", "od_w_in", "od_conv_w", "od_conv_b", "od_w_out")
SHARD_DIM = dict(ada_w=2, ffn_w_up=2, ffn_conv_w=2, ffn_w_down=1, ev_w_in=2, ev_conv_w=2, ev_w_uq=2, ev_w_ukv=2, ev_w_out=1,
                 od_w_in=2, od_conv_w=2, od_conv_b=1, od_w_out=1)
BIG = ("ffn_w_up", "ffn_w_down", "ev_w_in", "ev_w_uq", "ev_w_ukv", "ev_w_out", "od_w_in", "od_w_out")
NATIVE = ("ffn_w_up", "od_w_in")
SMALL_SHARDED = ("ffn_conv_w", "ev_conv_w", "od_conv_w", "od_conv_b")
SMALL_GRADS = ("norm_mix_g", "norm_ffn_g", "ffn_conv_w", "ffn_conv_b", "ev_conv_w", "ev_conv_b", "ev_ln_g", "ev_ln_b",
               "ev_qa_norm_g", "ev_kva_norm_g", "ev_q_norm_g", "ev_k_norm_g", "od_conv_w", "od_conv_b")
SMALL_ADAM = ("c_ctx", "ada_b") + SMALL_GRADS


def _size(shape):
    n = 1
    for v in shape:
        n *= v
    return n


def _pack(parts, dtype, row_mult, lead=0):
    lead_shape = parts[0].shape[:lead]
    flat = jnp.concatenate([p.astype(dtype).reshape(lead_shape + (-1,)) for p in parts], axis=-1)
    per = PACK_W * row_mult
    total = -(-flat.shape[-1] // per) * per
    flat = jnp.pad(flat, [(0, 0)] * lead + [(0, total - flat.shape[-1])])
    return flat.reshape(lead_shape + (total // PACK_W, PACK_W))


def _unpack(buf, shapes):
    lead_shape = buf.shape[:-2]
    flat = buf.reshape(lead_shape + (-1,))
    out, off = [], 0
    for shp in shapes:
        n = _size(shp)
        out.append(flat[..., off:off + n].reshape(lead_shape + tuple(shp)))
        off += n
    return out


def _unshard(pieces, k):
    t = jnp.moveaxis(pieces, 0, k)
    return t.reshape(t.shape[:k] + (t.shape[k] * t.shape[k + 1],) + t.shape[k + 2:])


def _shard_major(full, k):
    t = full.reshape(full.shape[:k] + (N_DEV, full.shape[k] // N_DEV) + full.shape[k + 1:])
    return jnp.moveaxis(t, k, 0)


def _my_shard(full, k, me):
    n = full.shape[k] // N_DEV
    return lax.dynamic_slice_in_dim(full, me * n, n, axis=k)


def kernel(x, c, ctx, c_ctx, ada_w, ada_b, norm_mix_g, norm_ffn_g, ffn_w_up, ffn_conv_w, ffn_conv_b, ffn_w_down, ev_w_in, ev_conv_w, ev_conv_b, ev_ln_g, ev_ln_b, ev_qa_norm_g, ev_w_uq, ev_kva_norm_g, ev_w_ukv, ev_q_norm_g, ev_k_norm_g, ev_w_out, od_w_in, od_conv_w, od_conv_b, od_w_out, loss_target, m_c_ctx, m_ada_w, m_ada_b, m_norm_mix_g, m_norm_ffn_g, m_ffn_w_up, m_ffn_conv_w, m_ffn_conv_b, m_ffn_w_down, m_ev_w_in, m_ev_conv_w, m_ev_conv_b, m_ev_ln_g, m_ev_ln_b, m_ev_qa_norm_g, m_ev_w_uq, m_ev_kva_norm_g, m_ev_w_ukv, m_ev_q_norm_g, m_ev_k_norm_g, m_ev_w_out, m_od_w_in, m_od_conv_w, m_od_conv_b, m_od_w_out, v_c_ctx, v_ada_w, v_ada_b, v_norm_mix_g, v_norm_ffn_g, v_ffn_w_up, v_ffn_conv_w, v_ffn_conv_b, v_ffn_w_down, v_ev_w_in, v_ev_conv_w, v_ev_conv_b, v_ev_ln_g, v_ev_ln_b, v_ev_qa_norm_g, v_ev_w_uq, v_ev_kva_norm_g, v_ev_w_ukv, v_ev_q_norm_g, v_ev_k_norm_g, v_ev_w_out, v_od_w_in, v_od_conv_w, v_od_conv_b, v_od_w_out):
    a = dict(locals())
    ix, iy, ic = _mesh_pos()
    me = 4 * ix + 2 * iy + ic
    xs, ctxs, target = x[0], ctx[0], loss_target[0]
    d = xs.shape[1]
    nl, _, cols = ada_w.shape

    pieces = [(n, l) for n in BIG if n not in NATIVE for l in range(a[n].shape[0])]
    piece_shapes = [a[n].shape[1:] for n, _ in pieces]
    wall = _allgather(_pack([a[n][l] for n, l in pieces], BF, 16), in_vmem=False, name="gather_big_weights")
    w = {"ffn_w_down4": []}
    for (n, l), p in zip(pieces, _unpack(wall, piece_shapes)):
        if n == "ffn_w_down":
            w["ffn_w_down4"].append(p.reshape(N_DEV // 2, 2 * p.shape[1], p.shape[2]))
        else:
            w[n] = _unshard(p, SHARD_DIM[n] - 1)[None]
    native = {}
    for n in NATIVE:
        nlay, rows, width = a[n].shape
        got = _allgather(a[n].astype(BF).reshape(nlay * rows, width), in_vmem=False, name="gather_" + n)
        native[n] = [got[:, l * rows:(l + 1) * rows] for l in range(nlay)]
    w["ffn_w_up8"] = native["ffn_w_up"]
    w["od_w_in"] = _unshard(native["od_w_in"][0], SHARD_DIM["od_w_in"] - 1)[None]
    sall = _allgather(_pack([c] + [a[n] for n in SMALL_SHARDED], F32, SUBLANES), in_vmem=True, name="gather_cond")
    sp = _unpack(sall, [c.shape] + [a[n].shape for n in SMALL_SHARDED])
    c8 = sp[0].reshape(N_DEV, d)
    for n, p in zip(SMALL_SHARDED, sp[1:]):
        w[n] = _unshard(p, SHARD_DIM[n])
    for n in SMALL_GRADS:
        if n not in SMALL_SHARDED:
            w[n] = a[n]

    cc = c_ctx.reshape(1, d)
    mpart = _ada_fwd(c8, cc, ada_w, lax.dynamic_slice_in_dim(ada_b, me * cols, cols, axis=1), name="ada_fwd")
    mall = _allgather(mpart.reshape(nl * ADA_ROWS, cols), in_vmem=True, name="gather_mod").reshape(N_DEV, nl, ADA_ROWS, cols)
    mine = lax.dynamic_index_in_dim(mall, me, axis=2, keepdims=False)
    mods = jnp.transpose(mine, (1, 0, 2)).reshape(nl, N_MOD, d)
    modc = mall[:, 0, N_DEV, :].reshape(-1)[:2 * d].reshape(2, d)

    loss_row, dx, g, dmods, dmodc = _local_step(xs, ctxs, target, mods, modc, w)
    loss = lax.psum(loss_row[0, 0], MESH_AXES)

    gparts = [g["ffn_w_down8"][l] if n == "ffn_w_down" else _shard_major(g[n][l], SHARD_DIM[n] - 1) for n, l in pieces]
    gsum = _sum8(_exchange(_pack(gparts, BF, 16, lead=1), name="exchange_big_grads"), name="sum_big_grads")
    grads = {}
    for (n, l), t in zip(pieces, _unpack(gsum, piece_shapes)):
        grads.setdefault(n, []).append(t)
    grads = {n: jnp.stack(ts) for n, ts in grads.items()}
    native_grads = {"ffn_w_up": jnp.concatenate(g["ffn_w_up8"], axis=1),
                    "od_w_in": _shard_major(g["od_w_in"][0], SHARD_DIM["od_w_in"] - 1).astype(BF)}
    for n in NATIVE:
        t = _sum8(_exchange(native_grads[n], name="exchange_grad_" + n), name="sum_grad_" + n)
        grads[n] = t.reshape(a[n].shape)

    small_parts = [dmods, dmodc] + [g[n] for n in SMALL_GRADS]
    small_shapes = [p.shape for p in small_parts]
    small = _allgather(_pack(small_parts, F32, SUBLANES), in_vmem=True, name="gather_small_grads")
    tots = _unpack(_sum8(small, name="sum_small_grads"), small_shapes)
    for n, t in zip(SMALL_GRADS, tots[2:]):
        grads[n] = _my_shard(t, SHARD_DIM[n], me) if n in SMALL_SHARDED else t
    dm_all = _unpack(small, small_shapes[:1])[0].reshape(N_DEV, nl, N_MOD * d)
    tot_dm = tots[0].reshape(nl, N_MOD * d)
    dmodc_pad = _pad_last(tots[1].reshape(1, 2 * d), N_MOD * d)
    dm_cols = lax.dynamic_slice_in_dim(dm_all, me * cols, cols, axis=2)
    dctx_cols = lax.dynamic_slice_in_dim(dmodc_pad, me * cols, cols, axis=1)
    ctx_rows = jnp.concatenate([dctx_cols[None], jnp.zeros((nl - 1, 1, cols), F32)], axis=0)
    g16 = jnp.concatenate([jnp.transpose(dm_cols, (1, 0, 2)), ctx_rows, jnp.zeros((nl, ADA_ROWS - N_DEV - 1, cols), F32)], axis=1)
    grads["ada_w"], grads["ada_b"], cpart = _ada_bwd(c8, cc, ada_w, g16, dctx_cols, tot_dm, dmodc_pad, name="ada_bwd")
    grads["c_ctx"] = _sum8(_allgather(cpart, in_vmem=True, name="gather_c_ctx_grad"), name="sum_c_ctx_grad")[0]

    delta, new_m, new_v = {}, {}, {}
    for n in BIG + ("ada_w",):
        shp = a[n].shape
        two = lambda t: t.reshape(-1, shp[-1])
        outs = _adamw(two(a[n]), two(grads[n]), two(a["m_" + n]), two(a["v_" + n]), name="adamw_" + n)
        delta[n], new_m[n], new_v[n] = (o.reshape(shp) for o in outs)
    shapes = [a[n].shape for n in SMALL_ADAM]
    packs = [_pack([src[pre + n] for n in SMALL_ADAM], F32, SUBLANES) for src, pre in ((a, ""), (grads, ""), (a, "m_"), (a, "v_"))]
    outs = _adamw(*packs, name="adamw_small")
    for dst, o in zip((delta, new_m, new_v), outs):
        dst.update(zip(SMALL_ADAM, _unpack(o, shapes)))

    return (loss, dx[None], *[grads[n].reshape(a[n].shape) for n in WEIGHTS], *[delta[n] for n in WEIGHTS],
            *[new_m[n] for n in WEIGHTS], *[new_v[n] for n in WEIGHTS])
```

```python
import functools

import jax
import jax.numpy as jnp
from jax import lax
from jax.experimental import pallas as pl
from jax.experimental.pallas import tpu as pltpu

F32, BF = jnp.float32, jnp.bfloat16
N_DEV = 8
MESH_AXES = ("x", "y", "c")
LANES = 128
SUBLANES = 8
HALO_ROWS = 16
VMEM_LIMIT = 56 << 20
PACK_W = 1024
MM_TM, MM_TN, MM_TK = 1024, 1408, 2048
EPS = 1e-6
QK_NOPE, QK_ROPE, V_DIM, GRID_W = 64, 32, 64, 64
QK_DIM = QK_NOPE + QK_ROPE
ROPE_THETA = 10000.0
SM_SCALE = QK_DIM ** -0.5
N_MOD = 6
ADAM_LR, ADAM_B1, ADAM_B2, ADAM_EPS, ADAM_WD, ADAM_STEP = 0.001, 0.9, 0.999, 1e-08, 0.01, 10


def _div(n, cap):
    if n <= cap:
        return n
    best = None
    for d in range(LANES, cap + 1, LANES):
        if n % d == 0:
            best = d
    return n if best is None else best


def _cp(*sem):
    return pltpu.CompilerParams(dimension_semantics=sem, vmem_limit_bytes=VMEM_LIMIT)


def _rms(x, n=None):
    d = x.shape[-1] if n is None else n
    return x * lax.rsqrt(jnp.sum(x * x, axis=-1, keepdims=True) / d + EPS)


@functools.lru_cache(maxsize=None)
def _lane_roll(shift):
    @jax.custom_vjp
    def roll(x):
        return pltpu.roll(x, shift, 1)

    def fwd(x):
        return roll(x), None

    def bwd(_, g):
        return (pltpu.roll(g, (LANES - shift) % LANES, 1),)

    roll.defvjp(fwd, bwd)
    return roll


def _mm(a, b, *, ta=False, tb=False, out_dtype=F32, res=None, gate=None, name):
    m, k = (a.shape[1], a.shape[0]) if ta else a.shape
    n = b.shape[0] if tb else b.shape[1]
    tm, tn, tk = _div(m, MM_TM), _div(n, MM_TN), _div(k, MM_TK)
    nk = k // tk
    a_spec = pl.BlockSpec((tk, tm), lambda i, j, kk: (kk, i)) if ta else pl.BlockSpec((tm, tk), lambda i, j, kk: (i, kk))
    b_spec = pl.BlockSpec((tn, tk), lambda i, j, kk: (j, kk)) if tb else pl.BlockSpec((tk, tn), lambda i, j, kk: (kk, j))
    o_spec = pl.BlockSpec((tm, tn), lambda i, j, kk: (i, j))
    dn = (((0 if ta else 1,), (1 if tb else 0,)), ((), ()))
    fused = res is not None

    def body(*refs):
        if fused:
            a_ref, b_ref, res_ref, gate_ref, o_ref, f_ref, acc = refs
        else:
            a_ref, b_ref, o_ref, acc = refs
        kk = pl.program_id(2)

        @pl.when(kk == 0)
        def _():
            acc[...] = jnp.zeros_like(acc)

        acc[...] += lax.dot_general(a_ref[...].astype(BF), b_ref[...].astype(BF), dn, preferred_element_type=F32)

        @pl.when(kk == nk - 1)
        def _():
            if fused:
                f_ref[...] = acc[...]
                o_ref[...] = res_ref[...] + gate_ref[...] * acc[...]
            else:
                o_ref[...] = acc[...].astype(out_dtype)

    in_specs, args = [a_spec, b_spec], [a, b]
    out_specs, out_shape = o_spec, jax.ShapeDtypeStruct((m, n), out_dtype)
    if fused:
        in_specs += [o_spec, pl.BlockSpec((1, tn), lambda i, j, kk: (0, j))]
        args += [res, gate]
        out_specs = [o_spec, o_spec]
        out_shape = [jax.ShapeDtypeStruct((m, n), F32), jax.ShapeDtypeStruct((m, n), F32)]
    return pl.pallas_call(
        body, grid=(m // tm, n // tn, nk), in_specs=in_specs, out_specs=out_specs, out_shape=out_shape,
        scratch_shapes=[pltpu.VMEM((tm, tn), F32)], compiler_params=_cp("parallel", "parallel", "arbitrary"), name=name,
    )(*args)


def _bmm(a, b, *, a_blk=None, b_blk=None, ta=False, tb=False, out_dtype=F32, res=None, gate=None, name):
    a2, b2 = a.shape[-2:], b.shape[-2:]
    m, k = (a2[1], a2[0]) if ta else a2
    n = b2[0] if tb else b2[1]
    nbo = max([x.shape[0] for x, blk in ((a, a_blk), (b, b_blk)) if blk == "o"], default=1)
    nbk = max([x.shape[0] for x, blk in ((a, a_blk), (b, b_blk)) if blk == "k"], default=1)
    tm, tn, tk = _div(m, MM_TM), _div(n, MM_TN), _div(k, MM_TK)
    nk = k // tk
    dn = (((0 if ta else 1,), (1 if tb else 0,)), ((), ()))
    fused = res is not None

    def spec(blk, shape2, idx2):
        if blk is None:
            return pl.BlockSpec(shape2, lambda bo, i, j, bk, kk: idx2(i, j, kk))
        if blk == "o":
            return pl.BlockSpec((None,) + shape2, lambda bo, i, j, bk, kk: (bo,) + idx2(i, j, kk))
        return pl.BlockSpec((None,) + shape2, lambda bo, i, j, bk, kk: (bk,) + idx2(i, j, kk))

    a_spec = spec(a_blk, (tk, tm), lambda i, j, kk: (kk, i)) if ta else spec(a_blk, (tm, tk), lambda i, j, kk: (i, kk))
    b_spec = spec(b_blk, (tn, tk), lambda i, j, kk: (j, kk)) if tb else spec(b_blk, (tk, tn), lambda i, j, kk: (kk, j))
    o_spec = spec("o" if nbo > 1 else None, (tm, tn), lambda i, j, kk: (i, j))

    def body(*refs):
        if fused:
            a_ref, b_ref, res_ref, gate_ref, o_ref, f_ref, acc = refs
        else:
            a_ref, b_ref, o_ref, acc = refs
        bk, kk = pl.program_id(3), pl.program_id(4)

        @pl.when((bk == 0) & (kk == 0))
        def _():
            acc[...] = jnp.zeros_like(acc)

        acc[...] += lax.dot_general(a_ref[...].astype(BF), b_ref[...].astype(BF), dn, preferred_element_type=F32)

        @pl.when((bk == nbk - 1) & (kk == nk - 1))
        def _():
            if fused:
                f_ref[...] = acc[...]
                o_ref[...] = res_ref[...] + gate_ref[...] * acc[...]
            else:
                o_ref[...] = acc[...].astype(out_dtype)

    out_shape2 = (m, n) if nbo == 1 else (nbo, m, n)
    in_specs, args = [a_spec, b_spec], [a, b]
    out_specs, out_shape = o_spec, jax.ShapeDtypeStruct(out_shape2, out_dtype)
    if fused:
        in_specs += [o_spec, pl.BlockSpec((1, tn), lambda bo, i, j, bk, kk: (0, j))]
        args += [res, gate]
        out_specs = [o_spec, o_spec]
        out_shape = [jax.ShapeDtypeStruct(out_shape2, F32)] * 2
    return pl.pallas_call(
        body, grid=(nbo, m // tm, n // tn, nbk, nk), in_specs=in_specs, out_specs=out_specs, out_shape=out_shape,
        scratch_shapes=[pltpu.VMEM((tm, tn), F32)],
        compiler_params=_cp("parallel", "parallel", "parallel", "arbitrary", "arbitrary"), name=name,
    )(*args)


def _row(tm, c, off=0):
    return pl.BlockSpec((tm, c), lambda i: (i + off, 0))


def _full(shape):
    return pl.BlockSpec(shape, lambda *_: (0,) * len(shape))


def _acc(ref, val, first):
    @pl.when(first)
    def _():
        ref[...] = jnp.zeros_like(ref)

    ref[...] += val


def _modnorm_f(x, g, sc, sh):
    return (_rms(x) * g) * (1.0 + sc) + sh


def _cls_spec(ncls, nctx, d):
    if ncls == 2:
        return pl.BlockSpec((1, 1, d), lambda i: (jnp.where(i < nctx, 0, 1), 0, 0))
    return pl.BlockSpec((1, 1, d), lambda i: (0, 0, 0))


def _modnorm(x, g, sc, sh, *, nctx, tm, name):
    r, d = x.shape
    cls = _cls_spec(sc.shape[0], nctx, d)

    def body(x_ref, g_ref, sc_ref, sh_ref, o_ref):
        o_ref[...] = _modnorm_f(x_ref[...], g_ref[...], sc_ref[0], sh_ref[0]).astype(BF)

    return pl.pallas_call(
        body, grid=(r // tm,), in_specs=[_row(tm, d), _full((1, d)), cls, cls], out_specs=_row(tm, d),
        out_shape=jax.ShapeDtypeStruct((r, d), BF), compiler_params=_cp("parallel"), name=name,
    )(x, g, sc, sh)


def _modnorm_bwd(x, dh, dres, g, sc, sh, *, nctx, tm, name):
    r, d = x.shape
    ncls = sc.shape[0]
    s = r - nctx * tm
    cls = _cls_spec(ncls, nctx, d)
    lat = pl.BlockSpec((tm, d), lambda i: (jnp.maximum(i - nctx, 0), 0))

    def body(x_ref, dh_ref, dres_ref, g_ref, sc_ref, sh_ref, dx_ref, dg_ref, dsc_ref, dsh_ref):
        i = pl.program_id(0)
        _, vjp = jax.vjp(_modnorm_f, x_ref[...], g_ref[...], sc_ref[0], sh_ref[0])
        dx, dg, dsc, dsh = vjp(dh_ref[...])
        _acc(dg_ref, dg, i == 0)
        first = (i == 0) | (i == nctx) if ncls == 2 else i == 0
        _acc(dsc_ref, dsc[None], first)
        _acc(dsh_ref, dsh[None], first)

        @pl.when(i >= nctx)
        def _():
            dx_ref[...] = dx + dres_ref[...]

    return pl.pallas_call(
        body, grid=(r // tm,), in_specs=[_row(tm, d), _row(tm, d), lat, _full((1, d)), cls, cls],
        out_specs=[lat, _full((1, d)), cls, cls],
        out_shape=[jax.ShapeDtypeStruct((s, d), F32), jax.ShapeDtypeStruct((1, d), F32),
                   jax.ShapeDtypeStruct((ncls, 1, d), F32), jax.ShapeDtypeStruct((ncls, 1, d), F32)],
        compiler_params=_cp("arbitrary"), name=name,
    )(x, dh, dres, g, sc, sh)


def _even_parts(p, c, ql, kvl):
    return p[:, :c], p[:, c:2 * c], p[:, 2 * c:2 * c + ql], p[:, 2 * c + ql:2 * c + ql + kvl]


def _even_mid_f(val, gate, cq, ckv, qa_g, kva_g):
    return val * jax.nn.sigmoid(gate), _rms(cq) * qa_g, _rms(ckv) * kva_g


def _even_mid(p, qa_g, kva_g, *, c, tm, name):
    r, w = p.shape
    ql, kvl = qa_g.shape[1], kva_g.shape[1]

    def body(p_ref, qg_ref, kg_ref, gl_ref, qn_ref, kvn_ref):
        gl, qn, kvn = _even_mid_f(*_even_parts(p_ref[...], c, ql, kvl), qg_ref[...], kg_ref[...])
        gl_ref[...] = gl
        qn_ref[...] = qn.astype(BF)
        kvn_ref[...] = kvn.astype(BF)

    return pl.pallas_call(
        body, grid=(r // tm,), in_specs=[_row(tm, w), _full((1, ql)), _full((1, kvl))],
        out_specs=[_row(tm, c), _row(tm, ql), _row(tm, kvl)],
        out_shape=[jax.ShapeDtypeStruct((r, c), F32), jax.ShapeDtypeStruct((r, ql), BF), jax.ShapeDtypeStruct((r, kvl), BF)],
        compiler_params=_cp("parallel"), name=name,
    )(p, qa_g, kva_g)


def _even_mid_bwd(p, dgl, dqn, dkvn, dkr, qa_g, kva_g, *, c, tm, name):
    r, w = p.shape
    ql, kvl = qa_g.shape[1], kva_g.shape[1]
    tail = w - (2 * c + ql + kvl + LANES)

    def body(p_ref, dgl_ref, dqn_ref, dkvn_ref, dkr_ref, qg_ref, kg_ref, dp_ref, dqg_ref, dkg_ref):
        i = pl.program_id(0)
        _, vjp = jax.vjp(_even_mid_f, *_even_parts(p_ref[...], c, ql, kvl), qg_ref[...], kg_ref[...])
        dval, dgate, dcq, dckv, dqg, dkg = vjp((dgl_ref[...], dqn_ref[...], dkvn_ref[...]))
        parts = [dval, dgate, dcq, dckv, dkr_ref[...]]
        if tail:
            parts.append(jnp.zeros((tm, tail), F32))
        dp_ref[...] = jnp.concatenate(parts, axis=1).astype(BF)
        _acc(dqg_ref, dqg, i == 0)
        _acc(dkg_ref, dkg, i == 0)

    return pl.pallas_call(
        body, grid=(r // tm,),
        in_specs=[_row(tm, w), _row(tm, c), _row(tm, ql), _row(tm, kvl), _row(tm, LANES), _full((1, ql)), _full((1, kvl))],
        out_specs=[_row(tm, w), _full((1, ql)), _full((1, kvl))],
        out_shape=[jax.ShapeDtypeStruct((r, w), BF), jax.ShapeDtypeStruct((1, ql), F32), jax.ShapeDtypeStruct((1, kvl), F32)],
        compiler_params=_cp("arbitrary"), name=name,
    )(p, dgl, dqn, dkvn, dkr, qa_g, kva_g)


def _halo_specs(tm, tc, total_rows, col_off=0, row_off=0):
    hb = HALO_ROWS
    nb = total_rows // hb
    cur = pl.BlockSpec((tm, tc), lambda j, i: (i + row_off // tm, j + col_off))
    prev = pl.BlockSpec((hb, tc), lambda j, i: (jnp.maximum((i * tm + row_off) // hb - 1, 0), j + col_off))
    nxt = pl.BlockSpec((hb, tc), lambda j, i: (jnp.minimum(((i + 1) * tm + row_off) // hb, nb - 1), j + col_off))
    return [prev, cur, nxt]


def _halo_specs_blk(tm, tc, total_rows, pair):
    hb = HALO_ROWS
    nb = total_rows // hb
    head = (2, None) if pair else (None,)
    idx = (lambda j, r: (0, j, r, 0)) if pair else (lambda j, r: (j, r, 0))
    cur = pl.BlockSpec(head + (tm, tc), lambda j, i: idx(j, i))
    prev = pl.BlockSpec(head + (hb, tc), lambda j, i: idx(j, jnp.maximum(i * tm // hb - 1, 0)))
    nxt = pl.BlockSpec(head + (hb, tc), lambda j, i: idx(j, jnp.minimum((i + 1) * tm // hb, nb - 1)))
    return [prev, cur, nxt]


def _ext(prev_ref, cur_ref, next_ref, i, n):
    prev = jnp.where(i == 0, 0.0, prev_ref[...].astype(F32))
    nxt = jnp.where(i == n - 1, 0.0, next_ref[...].astype(F32))
    return jnp.concatenate([prev, cur_ref[...].astype(F32), nxt], axis=-2)


def _sroll(x, shift):
    return pltpu.roll(x, shift % x.shape[0], 0)


def _conv_taps(e, w_ref, ksize, sign):
    pad = (ksize - 1) // 2
    out = None
    for k in range(ksize):
        t = w_ref[k:k + 1, :] * _sroll(e, -sign * (k - pad))
        out = t if out is None else out + t
    return out


def _core(e, tm):
    return e[HALO_ROWS:HALO_ROWS + tm]


def _ln_silu_f(u, g, b):
    mu = jnp.mean(u, axis=-1, keepdims=True)
    xc = u - mu
    y = xc * lax.rsqrt(jnp.mean(xc * xc, axis=-1, keepdims=True) + EPS)
    return jax.nn.silu(y * g + b)


def _conv_ln_silu(gl_all, w, b, ln_g, ln_b, *, s, row_off, tm, name):
    c = gl_all.shape[1]
    ksize = w.shape[0]
    n = s // tm

    def body(p_ref, c_ref, n_ref, w_ref, b_ref, g_ref, lb_ref, u_ref, a_ref):
        i = pl.program_id(1)
        e = _ext(p_ref, c_ref, n_ref, i, n)
        u = _core(_conv_taps(e, w_ref, ksize, 1), tm) + b_ref[...]
        u_ref[...] = u
        a_ref[...] = _ln_silu_f(u, g_ref[...], lb_ref[...]).astype(BF)

    out = pl.BlockSpec((tm, c), lambda j, i: (i, 0))
    return pl.pallas_call(
        body, grid=(1, n),
        in_specs=_halo_specs(tm, c, gl_all.shape[0], row_off=row_off) + [_full((ksize, c)), _full((1, c)), _full((1, c)), _full((1, c))],
        out_specs=[out, out], out_shape=[jax.ShapeDtypeStruct((s, c), F32), jax.ShapeDtypeStruct((s, c), BF)],
        compiler_params=_cp("parallel", "parallel"), name=name,
    )(gl_all, gl_all, gl_all, w, b, ln_g, ln_b)


def _ln_silu_bwd(u, dcat, ln_g, ln_b, *, tm, name):
    s, c = u.shape
    wc = dcat.shape[1]

    def body(u_ref, dc_ref, g_ref, b_ref, du_ref, dg_ref, db_ref):
        i = pl.program_id(0)
        _, vjp = jax.vjp(_ln_silu_f, u_ref[...], g_ref[...], b_ref[...])
        du, dg, db = vjp(dc_ref[:, :c])
        du_ref[...] = du
        _acc(dg_ref, dg, i == 0)
        _acc(db_ref, db, i == 0)

    return pl.pallas_call(
        body, grid=(s // tm,), in_specs=[_row(tm, c), _row(tm, wc), _full((1, c)), _full((1, c))],
        out_specs=[_row(tm, c), _full((1, c)), _full((1, c))],
        out_shape=[jax.ShapeDtypeStruct((s, c), F32), jax.ShapeDtypeStruct((1, c), F32), jax.ShapeDtypeStruct((1, c), F32)],
        compiler_params=_cp("arbitrary"), name=name,
    )(u, dcat, ln_g, ln_b)


def _dw_taps(dw_ref, g_core, e, ksize, tm, first):
    pad = (ksize - 1) // 2

    @pl.when(first)
    def _():
        dw_ref[...] = jnp.zeros_like(dw_ref)

    for k in range(ksize):
        dw_ref[k:k + 1, :] += jnp.sum(g_core * _core(_sroll(e, -(k - pad)), tm), axis=0, keepdims=True)


def _conv_bwd(du, gl_all, w, *, row_off, tm, name):
    s, c = du.shape
    ksize = w.shape[0]
    n = s // tm

    def body(dp_ref, dc_ref, dn_ref, gp_ref, gc_ref, gn_ref, w_ref, dgl_ref, dw_ref, db_ref):
        i = pl.program_id(1)
        de = _ext(dp_ref, dc_ref, dn_ref, i, n)
        ge = _ext(gp_ref, gc_ref, gn_ref, i, n)
        dgl_ref[...] = _core(_conv_taps(de, w_ref, ksize, -1), tm)
        du_core = dc_ref[...]
        _dw_taps(dw_ref, du_core, ge, ksize, tm, i == 0)
        _acc(db_ref, jnp.sum(du_core, axis=0, keepdims=True), i == 0)

    return pl.pallas_call(
        body, grid=(1, n),
        in_specs=_halo_specs(tm, c, s) + _halo_specs(tm, c, gl_all.shape[0], row_off=row_off) + [_full((ksize, c))],
        out_specs=[pl.BlockSpec((tm, c), lambda j, i: (i, 0)), _full((ksize, c)), _full((1, c))],
        out_shape=[jax.ShapeDtypeStruct((s, c), F32), jax.ShapeDtypeStruct((ksize, c), F32), jax.ShapeDtypeStruct((1, c), F32)],
        compiler_params=_cp("arbitrary", "arbitrary"), name=name,
    )(du, du, du, gl_all, gl_all, gl_all, w)


def _ffn_mid(u, w, b, *, tm, name):
    _, nb, s, tc = u.shape
    ksize = w.shape[1]
    n = s // tm

    def body(p_ref, c_ref, n_ref, w_ref, b_ref, z_ref):
        i = pl.program_id(1)
        e = _ext(p_ref, c_ref, n_ref, i, n)
        cg = _core(_conv_taps(e[0], w_ref, ksize, 1), tm) + b_ref[...]
        z_ref[...] = (jax.nn.silu(cg) * c_ref[1]).astype(BF)

    blk = lambda r: pl.BlockSpec((None, r, tc), lambda j, i: (j, 0, 0))
    return pl.pallas_call(
        body, grid=(nb, n), in_specs=_halo_specs_blk(tm, tc, s, True) + [blk(ksize), blk(1)],
        out_specs=pl.BlockSpec((None, tm, tc), lambda j, i: (j, i, 0)), out_shape=jax.ShapeDtypeStruct((nb, s, tc), BF),
        compiler_params=_cp("parallel", "parallel"), name=name,
    )(u, u, u, w, b)


def _ffn_mid_bwd(u, dz, w, b, *, tm, name):
    _, nb, s, tc = u.shape
    ksize = w.shape[1]
    n = s // tm

    def body(up_ref, uc_ref, un_ref, zp_ref, zc_ref, zn_ref, w_ref, b_ref, du_ref, dw_ref, db_ref):
        i = pl.program_id(1)
        ue = _ext(up_ref, uc_ref, un_ref, i, n)
        ge, ve = ue[0], ue[1]
        ze = _ext(zp_ref, zc_ref, zn_ref, i, n)
        cg = _conv_taps(ge, w_ref, ksize, 1) + b_ref[...]
        sg = jax.nn.sigmoid(cg)
        dcg = ze * ve * (sg * (1.0 + cg * (1.0 - sg)))
        du_ref[0] = _core(_conv_taps(dcg, w_ref, ksize, -1), tm).astype(BF)
        du_ref[1] = _core(ze * cg * sg, tm).astype(BF)
        dcg_core = _core(dcg, tm)
        _dw_taps(dw_ref, dcg_core, ge, ksize, tm, i == 0)
        _acc(db_ref, jnp.sum(dcg_core, axis=0, keepdims=True), i == 0)

    blk = lambda r: pl.BlockSpec((None, r, tc), lambda j, i: (j, 0, 0))
    return pl.pallas_call(
        body, grid=(nb, n), in_specs=_halo_specs_blk(tm, tc, s, True) + _halo_specs_blk(tm, tc, s, False) + [blk(ksize), blk(1)],
        out_specs=[pl.BlockSpec((2, None, tm, tc), lambda j, i: (0, j, i, 0)), blk(ksize), blk(1)],
        out_shape=[jax.ShapeDtypeStruct((2, nb, s, tc), BF), jax.ShapeDtypeStruct((nb, ksize, tc), F32), jax.ShapeDtypeStruct((nb, 1, tc), F32)],
        compiler_params=_cp("parallel", "arbitrary"), name=name,
    )(u, u, u, dz, dz, dz, w, b)


def _odd_mid(r3, w, b, *, tm, name):
    s, d3 = r3.shape
    d = d3 // 3
    ksize = w.shape[0]
    tc = _div(d, 1024)
    nc, n = d // tc, s // tm

    def body(bg_ref, cp_ref, cc_ref, cn_ref, up_ref, uc_ref, un_ref, w_ref, b_ref, m_ref):
        i = pl.program_id(1)
        pe = _ext(cp_ref, cc_ref, cn_ref, i, n) * _ext(up_ref, uc_ref, un_ref, i, n)
        sv = _core(_conv_taps(pe, w_ref, ksize, 1), tm) + b_ref[...]
        m_ref[...] = (bg_ref[...] * sv).astype(BF)

    cb = lambda j, i: (0, j)
    return pl.pallas_call(
        body, grid=(nc, n),
        in_specs=[pl.BlockSpec((tm, tc), lambda j, i: (i, j))] + _halo_specs(tm, tc, s, col_off=nc) + _halo_specs(tm, tc, s, col_off=2 * nc)
        + [pl.BlockSpec((ksize, tc), cb), pl.BlockSpec((1, tc), cb)],
        out_specs=pl.BlockSpec((tm, tc), lambda j, i: (i, j)), out_shape=jax.ShapeDtypeStruct((s, d), BF),
        compiler_params=_cp("parallel", "parallel"), name=name,
    )(r3, r3, r3, r3, r3, r3, r3, w, b)


def _odd_mid_bwd(r3, dm, w, b, *, tm, name):
    s, d3 = r3.shape
    d = d3 // 3
    ksize = w.shape[0]
    tc = _div(d, 1024)
    nc, n = d // tc, s // tm

    def body(bp_ref, bc_ref, bn_ref, cp_ref, cc_ref, cn_ref, up_ref, uc_ref, un_ref, mp_ref, mc_ref, mn_ref, w_ref, b_ref,
             dbg_ref, dcg_ref, du_ref, dw_ref, db_ref):
        i = pl.program_id(1)
        be = _ext(bp_ref, bc_ref, bn_ref, i, n)
        ce = _ext(cp_ref, cc_ref, cn_ref, i, n)
        ue = _ext(up_ref, uc_ref, un_ref, i, n)
        me = _ext(mp_ref, mc_ref, mn_ref, i, n)
        pe = ce * ue
        sv = _conv_taps(pe, w_ref, ksize, 1) + b_ref[...]
        ds = me * be
        dp = _core(_conv_taps(ds, w_ref, ksize, -1), tm)
        dbg_ref[...] = _core(me * sv, tm).astype(BF)
        dcg_ref[...] = (dp * uc_ref[...]).astype(BF)
        du_ref[...] = (dp * cc_ref[...]).astype(BF)
        ds_core = _core(ds, tm)
        _dw_taps(dw_ref, ds_core, pe, ksize, tm, i == 0)
        _acc(db_ref, jnp.sum(ds_core, axis=0, keepdims=True), i == 0)

    cb = lambda j, i: (0, j)
    col = pl.BlockSpec((tm, tc), lambda j, i: (i, j))
    return pl.pallas_call(
        body, grid=(nc, n),
        in_specs=_halo_specs(tm, tc, s) + _halo_specs(tm, tc, s, col_off=nc) + _halo_specs(tm, tc, s, col_off=2 * nc) + _halo_specs(tm, tc, s)
        + [pl.BlockSpec((ksize, tc), cb), pl.BlockSpec((1, tc), cb)],
        out_specs=[col, col, col, pl.BlockSpec((ksize, tc), cb), pl.BlockSpec((1, tc), cb)],
        out_shape=[jax.ShapeDtypeStruct((s, d), BF)] * 3 + [jax.ShapeDtypeStruct((ksize, d), F32), jax.ShapeDtypeStruct((1, d), F32)],
        compiler_params=_cp("parallel", "arbitrary"), name=name,
    )(r3, r3, r3, r3, r3, r3, r3, r3, r3, dm, dm, dm, w, b)


def _gate_bwd(dx, f, g, *, tm, name):
    s, d = dx.shape

    def body(dx_ref, f_ref, g_ref, df_ref, dg_ref):
        i = pl.program_id(0)
        dxv = dx_ref[...]
        df_ref[...] = (dxv * g_ref[...]).astype(BF)
        _acc(dg_ref, jnp.sum(dxv * f_ref[...], axis=0, keepdims=True), i == 0)

    return pl.pallas_call(
        body, grid=(s // tm,), in_specs=[_row(tm, d), _row(tm, d), _full((1, d))], out_specs=[_row(tm, d), _full((1, d))],
        out_shape=[jax.ShapeDtypeStruct((s, d), BF), jax.ShapeDtypeStruct((1, d), F32)], compiler_params=_cp("arbitrary"), name=name,
    )(dx, f, g)


def _loss_head(y, target, *, tm, name):
    s, d = y.shape

    def body(y_ref, t_ref, dy_ref, l_ref):
        i = pl.program_id(0)
        err = y_ref[...] - t_ref[...]
        dy_ref[...] = err * (1.0 / d)
        _acc(l_ref, jnp.full((1, LANES), 0.5 / d, F32) * jnp.sum(err * err), i == 0)

    return pl.pallas_call(
        body, grid=(s // tm,), in_specs=[_row(tm, d), _row(tm, d)], out_specs=[_row(tm, d), _full((1, LANES))],
        out_shape=[jax.ShapeDtypeStruct((s, d), F32), jax.ShapeDtypeStruct((1, LANES), F32)], compiler_params=_cp("arbitrary"), name=name,
    )(y, target)


def _rope(n, cos, sa, sb):
    return n * cos + _lane_roll(LANES - 8)(n) * sa + _lane_roll(8)(n) * sb


def _q_post_f(q, g, cos, sa, sb):
    outs = []
    for h in range(q.shape[1] // LANES):
        outs.append(_rope(_rms(q[:, h * LANES:(h + 1) * LANES], QK_DIM) * g, cos, sa, sb) * Q_PRESCALE)
    return jnp.concatenate(outs, axis=1)


def _kv_post_f(kv, kr, g, cos, sa, sb):
    lane = lax.broadcasted_iota(jnp.int32, (1, LANES), 1)
    krs = jnp.where((lane >= QK_NOPE) & (lane < QK_DIM), _lane_roll(QK_NOPE)(kr), 0.0)
    ks, vs = [], []
    for h in range(kv.shape[1] // LANES):
        kvh = kv[:, h * LANES:(h + 1) * LANES]
        ks.append(_rope(_rms(jnp.where(lane < QK_NOPE, kvh, krs), QK_DIM) * g, cos, sa, sb))
        vs.append(jnp.where(lane < V_DIM, _lane_roll(LANES - QK_NOPE)(kvh), 0.0))
    return jnp.concatenate(ks, axis=1), jnp.concatenate(vs, axis=1)


def _tab_specs(tm):
    return [_row(tm, LANES)] * 3


def _q_post(q, g, tabs, *, tm, name):
    r, w = q.shape

    def body(q_ref, g_ref, c_ref, sa_ref, sb_ref, o_ref):
        o_ref[...] = _q_post_f(q_ref[...], g_ref[...], c_ref[...], sa_ref[...], sb_ref[...]).astype(BF)

    return pl.pallas_call(
        body, grid=(r // tm,), in_specs=[_row(tm, w), _full((1, LANES))] + _tab_specs(tm), out_specs=_row(tm, w),
        out_shape=jax.ShapeDtypeStruct((r, w), BF), compiler_params=_cp("parallel"), name=name,
    )(q, g, *tabs)


def _q_post_bwd(q, dqh, g, tabs, *, tm, name):
    r, w = q.shape

    def body(q_ref, d_ref, g_ref, c_ref, sa_ref, sb_ref, dq_ref, dg_ref):
        i = pl.program_id(0)
        f = lambda qv, gv: _q_post_f(qv, gv, c_ref[...], sa_ref[...], sb_ref[...])
        _, vjp = jax.vjp(f, q_ref[...], g_ref[...])
        dq, dg = vjp(d_ref[...])
        dq_ref[...] = dq.astype(BF)
        _acc(dg_ref, dg, i == 0)

    return pl.pallas_call(
        body, grid=(r // tm,), in_specs=[_row(tm, w), _row(tm, w), _full((1, LANES))] + _tab_specs(tm),
        out_specs=[_row(tm, w), _full((1, LANES))],
        out_shape=[jax.ShapeDtypeStruct((r, w), BF), jax.ShapeDtypeStruct((1, LANES), F32)], compiler_params=_cp("arbitrary"), name=name,
    )(q, dqh, g, *tabs)


def _kr_spec(tm, w):
    return pl.BlockSpec((tm, LANES), lambda i: (i, w // LANES - 1))


def _kv_post(kv, p, g, tabs, *, tm, name):
    r, w = kv.shape

    def body(kv_ref, kr_ref, g_ref, c_ref, sa_ref, sb_ref, k_ref, v_ref):
        k, v = _kv_post_f(kv_ref[...], kr_ref[...], g_ref[...], c_ref[...], sa_ref[...], sb_ref[...])
        k_ref[...] = k.astype(BF)
        v_ref[...] = v.astype(BF)

    return pl.pallas_call(
        body, grid=(r // tm,), in_specs=[_row(tm, w), _kr_spec(tm, p.shape[1]), _full((1, LANES))] + _tab_specs(tm),
        out_specs=[_row(tm, w), _row(tm, w)], out_shape=[jax.ShapeDtypeStruct((r, w), BF)] * 2,
        compiler_params=_cp("parallel"), name=name,
    )(kv, p, g, *tabs)


def _kv_post_bwd(kv, p, dk, dv, g, tabs, *, tm, name):
    r, w = kv.shape

    def body(kv_ref, kr_ref, dk_ref, dv_ref, g_ref, c_ref, sa_ref, sb_ref, dkv_ref, dkr_ref, dg_ref):
        i = pl.program_id(0)
        f = lambda kvv, krv, gv: _kv_post_f(kvv, krv, gv, c_ref[...], sa_ref[...], sb_ref[...])
        _, vjp = jax.vjp(f, kv_ref[...], kr_ref[...], g_ref[...])
        dkv, dkr, dg = vjp((dk_ref[...], dv_ref[...]))
        dkv_ref[...] = dkv.astype(BF)
        dkr_ref[...] = dkr
        _acc(dg_ref, dg, i == 0)

    return pl.pallas_call(
        body, grid=(r // tm,),
        in_specs=[_row(tm, w), _kr_spec(tm, p.shape[1]), _row(tm, w), _row(tm, w), _full((1, LANES))] + _tab_specs(tm),
        out_specs=[_row(tm, w), _row(tm, LANES), _full((1, LANES))],
        out_shape=[jax.ShapeDtypeStruct((r, w), BF), jax.ShapeDtypeStruct((r, LANES), F32), jax.ShapeDtypeStruct((1, LANES), F32)],
        compiler_params=_cp("arbitrary"), name=name,
    )(kv, p, dk, dv, g, *tabs)


_NT = (((1,), (1,)), ((), ()))
_TN = (((0,), (0,)), ((), ()))
_NN = (((1,), (0,)), ((), ()))


ATTN_TQ, ATTN_TK = 512, 768
Q_PRESCALE = SM_SCALE * 1.4426950408889634
LN2 = 0.6931471805599453


def _attn_tiles(s, nk):
    return _div(s, ATTN_TQ), _div(nk, ATTN_TK)


def _chunk(ref, i, n):
    return ref[pl.ds(pl.multiple_of(i * n, n), n), :]


def _attn_fwd(q, k, v, *, name):
    s, w = q.shape
    nk = k.shape[0]
    tq, tk = _attn_tiles(s, nk)
    nj = nk // tk

    def body(q_ref, k_ref, v_ref, o_ref, lse_ref, m_s, l_s, acc_s):
        m_s[...] = jnp.full_like(m_s, -jnp.inf)
        l_s[...] = jnp.zeros_like(l_s)
        acc_s[...] = jnp.zeros_like(acc_s)
        qv = q_ref[...]

        def step(j, carry):
            sc = lax.dot_general(qv, _chunk(k_ref, j, tk), _NT, preferred_element_type=F32)
            m_prev = m_s[...]
            m_new = jnp.maximum(m_prev, jnp.max(sc, axis=1, keepdims=True))
            alpha = jnp.exp2(m_prev - m_new)
            pr = jnp.exp2(sc - m_new[:, :1])
            l_s[...] = alpha * l_s[...] + jnp.sum(pr, axis=1, keepdims=True)
            acc_s[...] = alpha * acc_s[...] + lax.dot_general(pr.astype(BF), _chunk(v_ref, j, tk), _NN, preferred_element_type=F32)
            m_s[...] = m_new
            return carry

        lax.fori_loop(0, nj, step, 0, unroll=True)
        o_ref[...] = (acc_s[...] / l_s[...]).astype(BF)
        lse_ref[...] = m_s[...] + jnp.log2(l_s[...])

    qs = pl.BlockSpec((tq, LANES), lambda h, i: (i, h))
    ks = pl.BlockSpec((nk, LANES), lambda h, i: (0, h))
    return pl.pallas_call(
        body, grid=(w // LANES, s // tq), in_specs=[qs, ks, ks], out_specs=[qs, qs],
        out_shape=[jax.ShapeDtypeStruct((s, w), BF), jax.ShapeDtypeStruct((s, w), F32)],
        scratch_shapes=[pltpu.VMEM((tq, LANES), F32)] * 3, compiler_params=_cp("parallel", "parallel"), name=name,
    )(q, k, v)


def _attn_delta(dcat, o, *, tm, name):
    s, w = o.shape
    wc = dcat.shape[1]

    def body(dc_ref, o_ref, do_ref, dl_ref):
        do = dc_ref[:, wc - w:]
        prod = do * o_ref[...].astype(F32)
        outs = []
        for h in range(w // LANES):
            outs.append(jnp.broadcast_to(jnp.sum(prod[:, h * LANES:(h + 1) * LANES], axis=1, keepdims=True), (tm, LANES)))
        do_ref[...] = do.astype(BF)
        dl_ref[...] = jnp.concatenate(outs, axis=1)

    return pl.pallas_call(
        body, grid=(s // tm,), in_specs=[_row(tm, wc), _row(tm, w)], out_specs=[_row(tm, w), _row(tm, w)],
        out_shape=[jax.ShapeDtypeStruct((s, w), BF), jax.ShapeDtypeStruct((s, w), F32)], compiler_params=_cp("parallel"), name=name,
    )(dcat, o)


def _attn_bwd(q, k, v, do, lse, delta, *, name):
    s, w = q.shape
    nk = k.shape[0]
    tq, tk = _attn_tiles(s, nk)
    ni, nj = s // tq, nk // tk

    def body(q_ref, k_ref, v_ref, do_ref, lse_ref, dl_ref, dq_ref, dk_ref, dv_ref, dk_s, dv_s):
        j = pl.program_id(1)

        @pl.when(j == 0)
        def _():
            dq_ref[...] = jnp.zeros_like(dq_ref)

        dk_s[...] = jnp.zeros_like(dk_s)
        dv_s[...] = jnp.zeros_like(dv_s)
        kv, vv = k_ref[...], v_ref[...]

        def step(i, carry):
            rows = pl.ds(pl.multiple_of(i * tq, tq), tq)
            qi, doi = q_ref[rows, :], do_ref[rows, :]
            sc = lax.dot_general(qi, kv, _NT, preferred_element_type=F32)
            pr = jnp.exp2(sc - lse_ref[rows, :][:, :1])
            dp = lax.dot_general(doi, vv, _NT, preferred_element_type=F32)
            ds = (pr * (dp - dl_ref[rows, :][:, :1])).astype(BF)
            dv_s[...] += lax.dot_general(pr.astype(BF), doi, _TN, preferred_element_type=F32)
            dk_s[...] += lax.dot_general(ds, qi, _TN, preferred_element_type=F32)
            dq_ref[rows, :] += lax.dot_general(ds, kv, _NN, preferred_element_type=F32)
            return carry

        lax.fori_loop(0, ni, step, 0, unroll=2 if ni % 2 == 0 else 1)
        dk_ref[...] = dk_s[...] * LN2
        dv_ref[...] = dv_s[...]

        @pl.when(j == nj - 1)
        def _():
            dq_ref[...] = dq_ref[...] * LN2

    qs = pl.BlockSpec((s, LANES), lambda h, j: (0, h))
    ks = pl.BlockSpec((tk, LANES), lambda h, j: (j, h))
    return pl.pallas_call(
        body, grid=(w // LANES, nj), in_specs=[qs, ks, ks, qs, qs, qs], out_specs=[qs, ks, ks],
        out_shape=[jax.ShapeDtypeStruct((s, w), F32), jax.ShapeDtypeStruct((nk, w), F32), jax.ShapeDtypeStruct((nk, w), F32)],
        scratch_shapes=[pltpu.VMEM((tk, LANES), F32)] * 2, compiler_params=_cp("parallel", "arbitrary"), name=name,
    )(q, k, v, do, lse, delta)


def _adamw(w, g, m, v, *, name):
    r, c = w.shape
    tr = r
    for cand in (512, 256, 128, 64, 32, 16, 8):
        if r % cand == 0 and cand * c * 4 <= (2 << 20):
            tr = cand
            break
    bc1 = 1.0 - ADAM_B1 ** ADAM_STEP
    bc2 = 1.0 - ADAM_B2 ** ADAM_STEP

    def body(w_ref, g_ref, m_ref, v_ref, d_ref, nm_ref, nv_ref):
        gv = g_ref[...]
        nm = ADAM_B1 * m_ref[...] + (1.0 - ADAM_B1) * gv
        nv = ADAM_B2 * v_ref[...] + (1.0 - ADAM_B2) * (gv * gv)
        d_ref[...] = -ADAM_LR * ((nm / bc1) / (jnp.sqrt(nv / bc2) + ADAM_EPS) + ADAM_WD * w_ref[...])
        nm_ref[...] = nm
        nv_ref[...] = nv

    spec = _row(tr, c)
    return pl.pallas_call(
        body, grid=(r // tr,), in_specs=[spec] * 4, out_specs=[spec] * 3,
        out_shape=[jax.ShapeDtypeStruct((r, c), F32)] * 3, compiler_params=_cp("parallel"), name=name,
    )(w, g, m, v)


def _mesh_pos():
    return lax.axis_index("x"), lax.axis_index("y"), lax.axis_index("c")


def _allgather(x, *, in_vmem, name):
    r, c = x.shape
    spec = pl.BlockSpec(memory_space=pltpu.VMEM if in_vmem else pl.ANY)

    def body(x_ref, out_ref, send_sems, recv_sems, local_sem):
        ix, iy, ic = _mesh_pos()
        me, sibling = (ix, iy, ic), (ix, iy, 1 - ic)
        chips = [(1 - ix, iy), (ix, 1 - iy), (1 - ix, 1 - iy)]

        def slab(px, py, pc):
            return out_ref.at[4 * px + 2 * py + pc]

        def copy(k, block, to, src=None):
            return pltpu.make_async_remote_copy(
                src_ref=slab(*block) if src is None else src, dst_ref=slab(*block),
                send_sem=send_sems.at[k], recv_sem=recv_sems.at[k], device_id=to, device_id_type=pl.DeviceIdType.MESH)

        mine = pltpu.make_async_copy(x_ref, slab(*me), local_sem)
        mine.start()
        first = [copy(0, me, sibling, src=x_ref)]
        first += [copy(1 + j, me, (*chip, ic), src=x_ref) for j, chip in enumerate(chips)]
        for cp in first:
            cp.start()
        passed = [copy(4 + j, (*chip, ic), sibling) for j, chip in enumerate(chips)]
        for j, chip in enumerate(chips):
            copy(1 + j, (*chip, ic), me).wait_recv()
            passed[j].start()
        copy(0, sibling, me).wait_recv()
        for j, chip in enumerate(chips):
            copy(4 + j, (*chip, 1 - ic), me).wait_recv()
        for cp in first + passed:
            cp.wait_send()
        mine.wait()

    return pl.pallas_call(
        body, out_shape=jax.ShapeDtypeStruct((N_DEV, r, c), x.dtype), in_specs=[spec], out_specs=spec,
        scratch_shapes=[pltpu.SemaphoreType.DMA((7,)), pltpu.SemaphoreType.DMA((7,)), pltpu.SemaphoreType.DMA], name=name,
    )(x)


def _exchange(g, *, name):
    _, r, c = g.shape
    spec = pl.BlockSpec(memory_space=pl.ANY)

    def body(g_ref, out_ref, send_sems, recv_sems, local_sem):
        ix, iy, ic = _mesh_pos()
        me = 4 * ix + 2 * iy + ic
        mine = pltpu.make_async_copy(g_ref.at[me], out_ref.at[me], local_sem)
        mine.start()
        sends, recvs = [], []
        for k in range(1, N_DEV):
            px = 1 - ix if k & 4 else ix
            py = 1 - iy if k & 2 else iy
            pc = 1 - ic if k & 1 else ic
            peer = 4 * px + 2 * py + pc
            mk = lambda src, dst: pltpu.make_async_remote_copy(
                src_ref=g_ref.at[src], dst_ref=out_ref.at[dst], send_sem=send_sems.at[k - 1], recv_sem=recv_sems.at[k - 1],
                device_id=(px, py, pc), device_id_type=pl.DeviceIdType.MESH)
            sends.append(mk(peer, me))
            recvs.append(mk(me, peer))
        for cp in sends:
            cp.start()
        for cp in recvs:
            cp.wait_recv()
        for cp in sends:
            cp.wait_send()
        mine.wait()

    return pl.pallas_call(
        body, out_shape=jax.ShapeDtypeStruct(g.shape, g.dtype), in_specs=[spec], out_specs=spec,
        scratch_shapes=[pltpu.SemaphoreType.DMA((7,)), pltpu.SemaphoreType.DMA((7,)), pltpu.SemaphoreType.DMA], name=name,
    )(g)


def _sum8(a, *, name):
    _, r, c = a.shape
    tr = r
    for cand in (512, 256, 128, 64, 32, 16):
        if r % cand == 0 and cand * c * 4 <= (1 << 20):
            tr = cand
            break

    def body(a_ref, o_ref):
        acc = a_ref[0].astype(F32)
        for d in range(1, N_DEV):
            acc = acc + a_ref[d].astype(F32)
        o_ref[...] = acc

    return pl.pallas_call(
        body, grid=(r // tr,), in_specs=[pl.BlockSpec((N_DEV, tr, c), lambda i: (0, i, 0))], out_specs=_row(tr, c),
        out_shape=jax.ShapeDtypeStruct((r, c), F32), compiler_params=_cp("parallel"), name=name,
    )(a)


ADA_ROWS = 16


def _silu_rows(c8, c_ctx):
    d = c8.shape[1]
    rows = jnp.concatenate([c8, c_ctx, jnp.zeros((ADA_ROWS - N_DEV - 1, d), F32)], axis=0)
    return jax.nn.silu(rows)


def _ada_fwd(c8, c_ctx, ada_w, ada_b_cols, *, name):
    nl, d, cols = ada_w.shape

    def body(c8_ref, cc_ref, w_ref, b_ref, o_ref):
        sc = _silu_rows(c8_ref[...], cc_ref[...]).astype(BF)
        for l in range(nl):
            o_ref[l] = lax.dot_general(sc, w_ref[l].astype(BF), _NN, preferred_element_type=F32) + b_ref[l:l + 1, :]

    return pl.pallas_call(
        body, out_shape=jax.ShapeDtypeStruct((nl, ADA_ROWS, cols), F32),
        compiler_params=pltpu.CompilerParams(vmem_limit_bytes=VMEM_LIMIT), name=name,
    )(c8, c_ctx, ada_w, ada_b_cols)


def _ada_bwd(c8, c_ctx, ada_w, g16, dctx_cols, tot_dm, dmodc_pad, *, name):
    nl, d, cols = ada_w.shape
    hi = lax.Precision.HIGHEST

    def body(c8_ref, cc_ref, w_ref, g_ref, dc_ref, tot_ref, dmc_ref, dw_ref, db_ref, part_ref):
        sc = _silu_rows(c8_ref[...], cc_ref[...])
        for l in range(nl):
            dw_ref[l] = lax.dot_general(sc, g_ref[l], _TN, precision=hi, preferred_element_type=F32)
        db_ref[...] = tot_ref[...]
        db_ref[0:1, :] += dmc_ref[...]
        ccv = cc_ref[...]
        sg = jax.nn.sigmoid(ccv)
        dsilu = sg * (1.0 + ccv * (1.0 - sg))
        part = lax.dot_general(dc_ref[...], w_ref[0], _NT, precision=hi, preferred_element_type=F32) * dsilu
        part_ref[...] = jnp.concatenate([part, jnp.zeros((SUBLANES - 1, d), F32)], axis=0)

    return pl.pallas_call(
        body, out_shape=[jax.ShapeDtypeStruct((nl, d, cols), F32), jax.ShapeDtypeStruct(tot_dm.shape, F32),
                         jax.ShapeDtypeStruct((SUBLANES, d), F32)],
        compiler_params=pltpu.CompilerParams(vmem_limit_bytes=VMEM_LIMIT), name=name,
    )(c8, c_ctx, ada_w, g16, dctx_cols, tot_dm, dmodc_pad)


def _rope_tables(s, lc):
    t = jnp.arange(s)
    half = QK_ROPE // 2
    inv = ROPE_THETA ** (-jnp.arange(0, half, 2, dtype=F32) / half)
    ang_r = (t // GRID_W).astype(F32)[:, None] * inv[None, :]
    ang_c = (t % GRID_W).astype(F32)[:, None] * inv[None, :]
    ang = jnp.concatenate([ang_r, ang_r, ang_c, ang_c], axis=-1)
    cos, sin = jnp.cos(ang), jnp.sin(ang)
    first = (jnp.arange(QK_ROPE) % half) < half // 2
    sa, sb = jnp.where(first, -sin, 0.0), jnp.where(first, 0.0, sin)

    def slot(mid, fill):
        body = jnp.concatenate([jnp.full((s, QK_NOPE), fill, F32), mid, jnp.full((s, LANES - QK_DIM), fill, F32)], axis=1)
        return jnp.concatenate([jnp.full((lc, LANES), fill, F32), body], axis=0)

    return slot(cos, 1.0), slot(sa, 0.0), slot(sb, 0.0)


def _pad_last(a, n):
    return jnp.pad(a, [(0, 0)] * (a.ndim - 1) + [(0, n - a.shape[-1])])


def _local_step(x, ctx, target, mods, modc, w):
    s, d = x.shape
    lc = ctx.shape[0]
    c = d // 2
    nh = (d - c) // V_DIM
    hw = nh * LANES
    ql, kvl = w["ev_qa_norm_g"].shape[-1], w["ev_kva_norm_g"].shape[-1]
    ei = 2 * c + ql + kvl + QK_ROPE
    eip = 2 * c + ql + kvl + LANES
    tm = 256 if (s % 256 == 0 and lc % 256 == 0) else 128
    nctx = lc // tm
    row = lambda v: v.reshape(1, -1)
    cls1 = lambda v: v.reshape(1, 1, -1)

    w_inp = _pad_last(w["ev_w_in"][0], eip)
    w_uqp = _pad_last(w["ev_w_uq"][0].reshape(ql, nh, QK_DIM), LANES).reshape(ql, hw)
    w_ukv = w["ev_w_ukv"][0]
    w_out = w["ev_w_out"][0]
    w_att = jnp.pad(w_out[c:].reshape(nh, V_DIM, d), [(0, 0), (0, LANES - V_DIM), (0, 0)]).reshape(hw, d)
    w_outp = jnp.concatenate([w_out[:c], w_att], axis=0)
    qg = _pad_last(row(w["ev_q_norm_g"]), LANES)
    kg = _pad_last(row(w["ev_k_norm_g"]), LANES)
    qa_g, kva_g = row(w["ev_qa_norm_g"]), row(w["ev_kva_norm_g"])
    ev_cw, ev_cb = w["ev_conv_w"][0], row(w["ev_conv_b"])
    ln_g, ln_b = row(w["ev_ln_g"]), row(w["ev_ln_b"])
    tabs = _rope_tables(s, lc)

    xall = jnp.concatenate([ctx, x], axis=0)
    sc0 = jnp.stack([modc[1], mods[0, 1]])[:, None, :]
    sh0 = jnp.stack([modc[0], mods[0, 0]])[:, None, :]
    g_mix0 = row(w["norm_mix_g"][0])
    hall = _modnorm(xall, g_mix0, sc0, sh0, nctx=nctx, tm=tm, name="l0_mix_norm")
    p_all = _mm(hall, w_inp, name="l0_w_in")
    gl_all, qn_all, kvn_all = _even_mid(p_all, qa_g, kva_g, c=c, tm=tm, name="l0_even_mid")
    u, a = _conv_ln_silu(gl_all, ev_cw, ev_cb, ln_g, ln_b, s=s, row_off=lc, tm=tm, name="l0_conformer")
    q_all = _mm(qn_all, w_uqp, name="l0_w_uq")
    kv_all = _mm(kvn_all, w_ukv, name="l0_w_ukv")
    qh_all = _q_post(q_all, qg, tabs, tm=tm, name="l0_q_post")
    k_all, v_all = _kv_post(kv_all, p_all, kg, tabs, tm=tm, name="l0_kv_post")
    qh = qh_all[lc:]
    o, lse = _attn_fwd(qh, k_all, v_all, name="l0_attn_fwd")
    cat = jnp.concatenate([a, o], axis=1)
    x1, f_mix0 = _mm(cat, w_outp, res=x, gate=row(mods[0, 2]), name="l0_w_out")

    nb = N_DEV // 2
    ffn_dim = w["ffn_conv_b"].shape[-1]
    tc = ffn_dim // nb
    ffn_k = w["ffn_conv_w"].shape[1]
    ffn_cw = [jnp.transpose(w["ffn_conv_w"][l].reshape(ffn_k, nb, tc), (1, 0, 2)) for l in range(2)]
    ffn_cb = [w["ffn_conv_b"][l].reshape(nb, 1, tc) for l in range(2)]

    def ffn_fwd(l, x_in):
        hf = _modnorm(x_in, row(w["norm_ffn_g"][l]), cls1(mods[l, 4]), cls1(mods[l, 3]), nctx=0, tm=tm, name=f"l{l}_ffn_norm")
        uu = _bmm(hf, w["ffn_w_up8"][l], b_blk="o", name=f"l{l}_w_up").reshape(2, nb, s, tc)
        z = _ffn_mid(uu, ffn_cw[l], ffn_cb[l], tm=tm, name=f"l{l}_ffn_mid")
        x_out, f = _bmm(z, w["ffn_w_down4"][l], a_blk="k", b_blk="k", res=x_in, gate=row(mods[l, 5]), name=f"l{l}_w_down")
        return x_out, (x_in, hf, uu, z, f)

    x2, ffn0 = ffn_fwd(0, x1)
    h1 = _modnorm(x2, row(w["norm_mix_g"][1]), cls1(mods[1, 1]), cls1(mods[1, 0]), nctx=0, tm=tm, name="l1_mix_norm")
    r3 = _mm(h1, w["od_w_in"][0], name="l1_w_in")
    od_cw, od_cb = w["od_conv_w"][0], row(w["od_conv_b"])
    m1 = _odd_mid(r3, od_cw, od_cb, tm=tm, name="l1_odd_mid")
    x3, f_mix1 = _mm(m1, w["od_w_out"][0], res=x2, gate=row(mods[1, 2]), name="l1_w_out")
    x4, ffn1 = ffn_fwd(1, x3)
    dx, loss_row = _loss_head(x4, target, tm=tm, name="loss_head")

    g = {}
    dmods = [[None] * N_MOD for _ in range(2)]

    def ffn_bwd(l, dx, saved):
        x_in, hf, uu, z, f = saved
        df, dmods[l][5] = _gate_bwd(dx, f, row(mods[l, 5]), tm=tm, name=f"l{l}_ffn_gate_bwd")
        dz = _bmm(df, w["ffn_w_down4"][l], b_blk="o", tb=True, name=f"l{l}_w_down_dx")
        dwd = _bmm(z, df, a_blk="o", ta=True, out_dtype=BF, name=f"l{l}_w_down_dw")
        du, dcw, dcb = _ffn_mid_bwd(uu, dz, ffn_cw[l], ffn_cb[l], tm=tm, name=f"l{l}_ffn_mid_bwd")
        du = du.reshape(N_DEV, s, tc)
        dhf = _bmm(du, w["ffn_w_up8"][l], a_blk="k", b_blk="k", tb=True, name=f"l{l}_w_up_dx")
        dwu = _bmm(hf, du, b_blk="o", ta=True, out_dtype=BF, name=f"l{l}_w_up_dw")
        dx, dgn, dsc, dsh = _modnorm_bwd(x_in, dhf, dx, row(w["norm_ffn_g"][l]), cls1(mods[l, 4]), cls1(mods[l, 3]),
                                         nctx=0, tm=tm, name=f"l{l}_ffn_norm_bwd")
        dmods[l][4], dmods[l][3] = dsc, dsh
        return dx, dict(ffn_w_up8=dwu, ffn_w_down8=dwd.reshape(N_DEV, tc // 2, d),
                        ffn_conv_w=jnp.transpose(dcw, (1, 0, 2)).reshape(ffn_k, ffn_dim), ffn_conv_b=dcb.reshape(1, ffn_dim), norm_ffn_g=dgn)

    dx, gf1 = ffn_bwd(1, dx, ffn1)
    df, dmods[1][2] = _gate_bwd(dx, f_mix1, row(mods[1, 2]), tm=tm, name="l1_mix_gate_bwd")
    dm1 = _mm(df, w["od_w_out"][0], tb=True, name="l1_w_out_dx")
    g["od_w_out"] = _mm(m1, df, ta=True, name="l1_w_out_dw")[None]
    dbg, dcg, duu, dcw, dcb = _odd_mid_bwd(r3, dm1, od_cw, od_cb, tm=tm, name="l1_odd_mid_bwd")
    g["od_conv_w"], g["od_conv_b"] = dcw[None], dcb
    dr3 = jnp.concatenate([dbg, dcg, duu], axis=1)
    dh1 = _mm(dr3, w["od_w_in"][0], tb=True, name="l1_w_in_dx")
    g["od_w_in"] = _mm(h1, dr3, ta=True, name="l1_w_in_dw")[None]
    dx, dgn1, dsc, dsh = _modnorm_bwd(x2, dh1, dx, row(w["norm_mix_g"][1]), cls1(mods[1, 1]), cls1(mods[1, 0]),
                                      nctx=0, tm=tm, name="l1_mix_norm_bwd")
    dmods[1][1], dmods[1][0] = dsc, dsh
    dx, gf0 = ffn_bwd(0, dx, ffn0)
    df, dmods[0][2] = _gate_bwd(dx, f_mix0, row(mods[0, 2]), tm=tm, name="l0_mix_gate_bwd")
    dcat = _mm(df, w_outp, tb=True, name="l0_w_out_dx")
    dw_outp = _mm(cat, df, ta=True, name="l0_w_out_dw")
    du0, g["ev_ln_g"], g["ev_ln_b"] = _ln_silu_bwd(u, dcat, ln_g, ln_b, tm=tm, name="l0_ln_silu_bwd")
    dgl, dcw, g["ev_conv_b"] = _conv_bwd(du0, gl_all, ev_cw, row_off=lc, tm=tm, name="l0_conformer_conv_bwd")
    g["ev_conv_w"] = dcw[None]
    do, delta = _attn_delta(dcat, o, tm=tm, name="l0_attn_delta")
    dq, dk, dv = _attn_bwd(qh, k_all, v_all, do, lse, delta, name="l0_attn_bwd")
    dq_all = jnp.concatenate([jnp.zeros((lc, hw), F32), dq], axis=0)
    dgl_all = jnp.concatenate([jnp.zeros((lc, c), F32), dgl], axis=0)
    dqp, dqg = _q_post_bwd(q_all, dq_all, qg, tabs, tm=tm, name="l0_q_post_bwd")
    dkvp, dkr, dkg = _kv_post_bwd(kv_all, p_all, dk, dv, kg, tabs, tm=tm, name="l0_kv_post_bwd")
    dqn = _mm(dqp, w_uqp, tb=True, name="l0_w_uq_dx")
    dw_uqp = _mm(qn_all, dqp, ta=True, name="l0_w_uq_dw")
    dkvn = _mm(dkvp, w_ukv, tb=True, name="l0_w_ukv_dx")
    g["ev_w_ukv"] = _mm(kvn_all, dkvp, ta=True, name="l0_w_ukv_dw")[None]
    dp, g["ev_qa_norm_g"], g["ev_kva_norm_g"] = _even_mid_bwd(p_all, dgl_all, dqn, dkvn, dkr, qa_g, kva_g, c=c, tm=tm, name="l0_even_mid_bwd")
    dhall = _mm(dp, w_inp, tb=True, name="l0_w_in_dx")
    dw_inp = _mm(hall, dp, ta=True, name="l0_w_in_dw")
    dx, dgn0, dsc2, dsh2 = _modnorm_bwd(xall, dhall, dx, g_mix0, sc0, sh0, nctx=nctx, tm=tm, name="l0_mix_norm_bwd")
    dmods[0][1], dmods[0][0] = dsc2[1], dsh2[1]
    dmodc = jnp.concatenate([dsh2[0], dsc2[0]], axis=0)

    g["ev_w_in"] = dw_inp[:, :ei][None]
    g["ev_w_uq"] = dw_uqp.reshape(ql, nh, LANES)[:, :, :QK_DIM].reshape(ql, nh * QK_DIM)[None]
    g["ev_w_out"] = jnp.concatenate([dw_outp[:c], dw_outp[c:].reshape(nh, LANES, d)[:, :V_DIM].reshape(nh * V_DIM, d)], axis=0)[None]
    g["ev_q_norm_g"], g["ev_k_norm_g"] = dqg[:, :QK_DIM], dkg[:, :QK_DIM]
    g["norm_mix_g"] = jnp.concatenate([dgn0, dgn1], axis=0)
    for name in ("ffn_w_up8", "ffn_w_down8"):
        g[name] = [gf0[name], gf1[name]]
    g["ffn_conv_w"] = jnp.stack([gf0["ffn_conv_w"], gf1["ffn_conv_w"]])
    for name in ("ffn_conv_b", "norm_ffn_g"):
        g[name] = jnp.concatenate([gf0[name], gf1[name]], axis=0)
    dmods_arr = jnp.stack([jnp.concatenate([v.reshape(1, d) for v in dmods[l]], axis=0) for l in range(2)])
    return loss_row, dx, g, dmods_arr, dmodc


WEIGHTS = ("c_ctx", "ada_w", "ada_b", "norm_mix_g", "norm_ffn_g", "ffn_w_up", "ffn_conv_w", "ffn_conv_b", "ffn_w_down", "ev_w_in",
           "ev_conv_w", "ev_conv_b", "ev_ln_g", "ev_ln_b", "ev_qa_norm_g", "ev_w_uq", "ev_kva_norm_g", "ev_w_ukv", "ev_q_norm_g",
           "ev_k_norm_g", "ev_w_out", "od_w_in", "od_conv_w", "od_conv_b", "od_w_out")
SHARD_DIM = dict(ada_w=2, ffn_w_up=2, ffn_conv_w=2, ffn_w_down=1, ev_w_in=2, ev_conv_w=2, ev_w_uq=2, ev_w_ukv=2, ev_w_out=1,
                 od_w_in=2, od_conv_w=2, od_conv_b=1, od_w_out=1)
BIG = ("ffn_w_up", "ffn_w_down", "ev_w_in", "ev_w_uq", "ev_w_ukv", "ev_w_out", "od_w_in", "od_w_out")
NATIVE = ("ffn_w_up", "od_w_in")
SMALL_SHARDED = ("ffn_conv_w", "ev_conv_w", "od_conv_w", "od_conv_b")
SMALL_GRADS = ("norm_mix_g", "norm_ffn_g", "ffn_conv_w", "ffn_conv_b", "ev_conv_w", "ev_conv_b", "ev_ln_g", "ev_ln_b",
               "ev_qa_norm_g", "ev_kva_norm_g", "ev_q_norm_g", "ev_k_norm_g", "od_conv_w", "od_conv_b")
SMALL_ADAM = ("c_ctx", "ada_b") + SMALL_GRADS


def _size(shape):
    n = 1
    for v in shape:
        n *= v
    return n


def _pack(parts, dtype, row_mult, lead=0):
    lead_shape = parts[0].shape[:lead]
    flat = jnp.concatenate([p.astype(dtype).reshape(lead_shape + (-1,)) for p in parts], axis=-1)
    per = PACK_W * row_mult
    total = -(-flat.shape[-1] // per) * per
    flat = jnp.pad(flat, [(0, 0)] * lead + [(0, total - flat.shape[-1])])
    return flat.reshape(lead_shape + (total // PACK_W, PACK_W))


def _unpack(buf, shapes):
    lead_shape = buf.shape[:-2]
    flat = buf.reshape(lead_shape + (-1,))
    out, off = [], 0
    for shp in shapes:
        n = _size(shp)
        out.append(flat[..., off:off + n].reshape(lead_shape + tuple(shp)))
        off += n
    return out


def _unshard(pieces, k):
    t = jnp.moveaxis(pieces, 0, k)
    return t.reshape(t.shape[:k] + (t.shape[k] * t.shape[k + 1],) + t.shape[k + 2:])


def _shard_major(full, k):
    t = full.reshape(full.shape[:k] + (N_DEV, full.shape[k] // N_DEV) + full.shape[k + 1:])
    return jnp.moveaxis(t, k, 0)


def _my_shard(full, k, me):
    n = full.shape[k] // N_DEV
    return lax.dynamic_slice_in_dim(full, me * n, n, axis=k)


def kernel(x, c, ctx, c_ctx, ada_w, ada_b, norm_mix_g, norm_ffn_g, ffn_w_up, ffn_conv_w, ffn_conv_b, ffn_w_down, ev_w_in, ev_conv_w, ev_conv_b, ev_ln_g, ev_ln_b, ev_qa_norm_g, ev_w_uq, ev_kva_norm_g, ev_w_ukv, ev_q_norm_g, ev_k_norm_g, ev_w_out, od_w_in, od_conv_w, od_conv_b, od_w_out, loss_target, m_c_ctx, m_ada_w, m_ada_b, m_norm_mix_g, m_norm_ffn_g, m_ffn_w_up, m_ffn_conv_w, m_ffn_conv_b, m_ffn_w_down, m_ev_w_in, m_ev_conv_w, m_ev_conv_b, m_ev_ln_g, m_ev_ln_b, m_ev_qa_norm_g, m_ev_w_uq, m_ev_kva_norm_g, m_ev_w_ukv, m_ev_q_norm_g, m_ev_k_norm_g, m_ev_w_out, m_od_w_in, m_od_conv_w, m_od_conv_b, m_od_w_out, v_c_ctx, v_ada_w, v_ada_b, v_norm_mix_g, v_norm_ffn_g, v_ffn_w_up, v_ffn_conv_w, v_ffn_conv_b, v_ffn_w_down, v_ev_w_in, v_ev_conv_w, v_ev_conv_b, v_ev_ln_g, v_ev_ln_b, v_ev_qa_norm_g, v_ev_w_uq, v_ev_kva_norm_g, v_ev_w_ukv, v_ev_q_norm_g, v_ev_k_norm_g, v_ev_w_out, v_od_w_in, v_od_conv_w, v_od_conv_b, v_od_w_out):
    a = dict(locals())
    ix, iy, ic = _mesh_pos()
    me = 4 * ix + 2 * iy + ic
    xs, ctxs, target = x[0], ctx[0], loss_target[0]
    d = xs.shape[1]
    nl, _, cols = ada_w.shape

    pieces = [(n, l) for n in BIG if n not in NATIVE for l in range(a[n].shape[0])]
    piece_shapes = [a[n].shape[1:] for n, _ in pieces]
    wall = _allgather(_pack([a[n][l] for n, l in pieces], BF, 16), in_vmem=False, name="gather_big_weights")
    w = {"ffn_w_down4": []}
    for (n, l), p in zip(pieces, _unpack(wall, piece_shapes)):
        if n == "ffn_w_down":
            w["ffn_w_down4"].append(p.reshape(N_DEV // 2, 2 * p.shape[1], p.shape[2]))
        else:
            w[n] = _unshard(p, SHARD_DIM[n] - 1)[None]
    native = {}
    for n in NATIVE:
        nlay, rows, width = a[n].shape
        got = _allgather(a[n].astype(BF).reshape(nlay * rows, width), in_vmem=False, name="gather_" + n)
        native[n] = [got[:, l * rows:(l + 1) * rows] for l in range(nlay)]
    w["ffn_w_up8"] = native["ffn_w_up"]
    w["od_w_in"] = _unshard(native["od_w_in"][0], SHARD_DIM["od_w_in"] - 1)[None]
    sall = _allgather(_pack([c] + [a[n] for n in SMALL_SHARDED], F32, SUBLANES), in_vmem=True, name="gather_cond")
    sp = _unpack(sall, [c.shape] + [a[n].shape for n in SMALL_SHARDED])
    c8 = sp[0].reshape(N_DEV, d)
    for n, p in zip(SMALL_SHARDED, sp[1:]):
        w[n] = _unshard(p, SHARD_DIM[n])
    for n in SMALL_GRADS:
        if n not in SMALL_SHARDED:
            w[n] = a[n]

    cc = c_ctx.reshape(1, d)
    mpart = _ada_fwd(c8, cc, ada_w, lax.dynamic_slice_in_dim(ada_b, me * cols, cols, axis=1), name="ada_fwd")
    mall = _allgather(mpart.reshape(nl * ADA_ROWS, cols), in_vmem=True, name="gather_mod").reshape(N_DEV, nl, ADA_ROWS, cols)
    mine = lax.dynamic_index_in_dim(mall, me, axis=2, keepdims=False)
    mods = jnp.transpose(mine, (1, 0, 2)).reshape(nl, N_MOD, d)
    modc = mall[:, 0, N_DEV, :].reshape(-1)[:2 * d].reshape(2, d)

    loss_row, dx, g, dmods, dmodc = _local_step(xs, ctxs, target, mods, modc, w)
    loss = lax.psum(loss_row[0, 0], MESH_AXES)

    gparts = [g["ffn_w_down8"][l] if n == "ffn_w_down" else _shard_major(g[n][l], SHARD_DIM[n] - 1) for n, l in pieces]
    gsum = _sum8(_exchange(_pack(gparts, BF, 16, lead=1), name="exchange_big_grads"), name="sum_big_grads")
    grads = {}
    for (n, l), t in zip(pieces, _unpack(gsum, piece_shapes)):
        grads.setdefault(n, []).append(t)
    grads = {n: jnp.stack(ts) for n, ts in grads.items()}
    native_grads = {"ffn_w_up": jnp.concatenate(g["ffn_w_up8"], axis=1),
                    "od_w_in": _shard_major(g["od_w_in"][0], SHARD_DIM["od_w_in"] - 1).astype(BF)}
    for n in NATIVE:
        t = _sum8(_exchange(native_grads[n], name="exchange_grad_" + n), name="sum_grad_" + n)
        grads[n] = t.reshape(a[n].shape)

    small_parts = [dmods, dmodc] + [g[n] for n in SMALL_GRADS]
    small_shapes = [p.shape for p in small_parts]
    small = _allgather(_pack(small_parts, F32, SUBLANES), in_vmem=True, name="gather_small_grads")
    tots = _unpack(_sum8(small, name="sum_small_grads"), small_shapes)
    for n, t in zip(SMALL_GRADS, tots[2:]):
        grads[n] = _my_shard(t, SHARD_DIM[n], me) if n in SMALL_SHARDED else t
    dm_all = _unpack(small, small_shapes[:1])[0].reshape(N_DEV, nl, N_MOD * d)
    tot_dm = tots[0].reshape(nl, N_MOD * d)
    dmodc_pad = _pad_last(tots[1].reshape(1, 2 * d), N_MOD * d)
    dm_cols = lax.dynamic_slice_in_dim(dm_all, me * cols, cols, axis=2)
    dctx_cols = lax.dynamic_slice_in_dim(dmodc_pad, me * cols, cols, axis=1)
    ctx_rows = jnp.concatenate([dctx_cols[None], jnp.zeros((nl - 1, 1, cols), F32)], axis=0)
    g16 = jnp.concatenate([jnp.transpose(dm_cols, (1, 0, 2)), ctx_rows, jnp.zeros((nl, ADA_ROWS - N_DEV - 1, cols), F32)], axis=1)
    grads["ada_w"], grads["ada_b"], cpart = _ada_bwd(c8, cc, ada_w, g16, dctx_cols, tot_dm, dmodc_pad, name="ada_bwd")
    grads["c_ctx"] = _sum8(_allgather(cpart, in_vmem=True, name="gather_c_ctx_grad"), name="sum_c_ctx_grad")[0]

    delta, new_m, new_v = {}, {}, {}
    for n in BIG + ("ada_w",):
        shp = a[n].shape
        two = lambda t: t.reshape(-1, shp[-1])
        outs = _adamw(two(a[n]), two(grads[n]), two(a["m_" + n]), two(a["v_" + n]), name="adamw_" + n)
        delta[n], new_m[n], new_v[n] = (o.reshape(shp) for o in outs)
    shapes = [a[n].shape for n in SMALL_ADAM]
    packs = [_pack([src[pre + n] for n in SMALL_ADAM], F32, SUBLANES) for src, pre in ((a, ""), (grads, ""), (a, "m_"), (a, "v_"))]
    outs = _adamw(*packs, name="adamw_small")
    for dst, o in zip((delta, new_m, new_v), outs):
        dst.update(zip(SMALL_ADAM, _unpack(o, shapes)))

    return (loss, dx[None], *[grads[n].reshape(a[n].shape) for n in WEIGHTS], *[delta[n] for n in WEIGHTS],
            *[new_m[n] for n in WEIGHTS], *[new_v[n] for n in WEIGHTS])
```

```python
import functools

import jax
import jax.numpy as jnp
from jax import lax
from jax.experimental import pallas as pl
from jax.experimental.pallas import tpu as pltpu

F32, BF = jnp.float32, jnp.bfloat16
N_DEV = 8
MESH_AXES = ("x", "y", "c")
LANES = 128
SUBLANES = 8
HALO_ROWS = 16
VMEM_LIMIT = 56 << 20
PACK_W = 1024
MM_TM, MM_TN, MM_TK = 1024, 1408, 2048
EPS = 1e-6
QK_NOPE, QK_ROPE, V_DIM, GRID_W = 64, 32, 64, 64
QK_DIM = QK_NOPE + QK_ROPE
ROPE_THETA = 10000.0
SM_SCALE = QK_DIM ** -0.5
N_MOD = 6
ADAM_LR, ADAM_B1, ADAM_B2, ADAM_EPS, ADAM_WD, ADAM_STEP = 0.001, 0.9, 0.999, 1e-08, 0.01, 10


def _div(n, cap):
    if n <= cap:
        return n
    best = None
    for d in range(LANES, cap + 1, LANES):
        if n % d == 0:
            best = d
    return n if best is None else best


def _cp(*sem):
    return pltpu.CompilerParams(dimension_semantics=sem, vmem_limit_bytes=VMEM_LIMIT)


def _rms(x, n=None):
    d = x.shape[-1] if n is None else n
    return x * lax.rsqrt(jnp.sum(x * x, axis=-1, keepdims=True) / d + EPS)


@functools.lru_cache(maxsize=None)
def _lane_roll(shift):
    @jax.custom_vjp
    def roll(x):
        return pltpu.roll(x, shift, 1)

    def fwd(x):
        return roll(x), None

    def bwd(_, g):
        return (pltpu.roll(g, (LANES - shift) % LANES, 1),)

    roll.defvjp(fwd, bwd)
    return roll


def _mm(a, b, *, ta=False, tb=False, out_dtype=F32, res=None, gate=None, name):
    m, k = (a.shape[1], a.shape[0]) if ta else a.shape
    n = b.shape[0] if tb else b.shape[1]
    tm, tn, tk = _div(m, MM_TM), _div(n, MM_TN), _div(k, MM_TK)
    nk = k // tk
    a_spec = pl.BlockSpec((tk, tm), lambda i, j, kk: (kk, i)) if ta else pl.BlockSpec((tm, tk), lambda i, j, kk: (i, kk))
    b_spec = pl.BlockSpec((tn, tk), lambda i, j, kk: (j, kk)) if tb else pl.BlockSpec((tk, tn), lambda i, j, kk: (kk, j))
    o_spec = pl.BlockSpec((tm, tn), lambda i, j, kk: (i, j))
    dn = (((0 if ta else 1,), (1 if tb else 0,)), ((), ()))
    fused = res is not None

    def body(*refs):
        if fused:
            a_ref, b_ref, res_ref, gate_ref, o_ref, f_ref, acc = refs
        else:
            a_ref, b_ref, o_ref, acc = refs
        kk = pl.program_id(2)

        @pl.when(kk == 0)
        def _():
            acc[...] = jnp.zeros_like(acc)

        acc[...] += lax.dot_general(a_ref[...].astype(BF), b_ref[...].astype(BF), dn, preferred_element_type=F32)

        @pl.when(kk == nk - 1)
        def _():
            if fused:
                f_ref[...] = acc[...]
                o_ref[...] = res_ref[...] + gate_ref[...] * acc[...]
            else:
                o_ref[...] = acc[...].astype(out_dtype)

    in_specs, args = [a_spec, b_spec], [a, b]
    out_specs, out_shape = o_spec, jax.ShapeDtypeStruct((m, n), out_dtype)
    if fused:
        in_specs += [o_spec, pl.BlockSpec((1, tn), lambda i, j, kk: (0, j))]
        args += [res, gate]
        out_specs = [o_spec, o_spec]
        out_shape = [jax.ShapeDtypeStruct((m, n), F32), jax.ShapeDtypeStruct((m, n), F32)]
    return pl.pallas_call(
        body, grid=(m // tm, n // tn, nk), in_specs=in_specs, out_specs=out_specs, out_shape=out_shape,
        scratch_shapes=[pltpu.VMEM((tm, tn), F32)], compiler_params=_cp("parallel", "parallel", "arbitrary"), name=name,
    )(*args)


def _bmm(a, b, *, a_blk=None, b_blk=None, ta=False, tb=False, out_dtype=F32, res=None, gate=None, name):
    a2, b2 = a.shape[-2:], b.shape[-2:]
    m, k = (a2[1], a2[0]) if ta else a2
    n = b2[0] if tb else b2[1]
    nbo = max([x.shape[0] for x, blk in ((a, a_blk), (b, b_blk)) if blk == "o"], default=1)
    nbk = max([x.shape[0] for x, blk in ((a, a_blk), (b, b_blk)) if blk == "k"], default=1)
    tm, tn, tk = _div(m, MM_TM), _div(n, MM_TN), _div(k, MM_TK)
    nk = k // tk
    dn = (((0 if ta else 1,), (1 if tb else 0,)), ((), ()))
    fused = res is not None

    def spec(blk, shape2, idx2):
        if blk is None:
            return pl.BlockSpec(shape2, lambda bo, i, j, bk, kk: idx2(i, j, kk))
        if blk == "o":
            return pl.BlockSpec((None,) + shape2, lambda bo, i, j, bk, kk: (bo,) + idx2(i, j, kk))
        return pl.BlockSpec((None,) + shape2, lambda bo, i, j, bk, kk: (bk,) + idx2(i, j, kk))

    a_spec = spec(a_blk, (tk, tm), lambda i, j, kk: (kk, i)) if ta else spec(a_blk, (tm, tk), lambda i, j, kk: (i, kk))
    b_spec = spec(b_blk, (tn, tk), lambda i, j, kk: (j, kk)) if tb else spec(b_blk, (tk, tn), lambda i, j, kk: (kk, j))
    o_spec = spec("o" if nbo > 1 else None, (tm, tn), lambda i, j, kk: (i, j))

    def body(*refs):
        if fused:
            a_ref, b_ref, res_ref, gate_ref, o_ref, f_ref, acc = refs
        else:
            a_ref, b_ref, o_ref, acc = refs
        bk, kk = pl.program_id(3), pl.program_id(4)

        @pl.when((bk == 0) & (kk == 0))
        def _():
            acc[...] = jnp.zeros_like(acc)

        acc[...] += lax.dot_general(a_ref[...].astype(BF), b_ref[...].astype(BF), dn, preferred_element_type=F32)

        @pl.when((bk == nbk - 1) & (kk == nk - 1))
        def _():
            if fused:
                f_ref[...] = acc[...]
                o_ref[...] = res_ref[...] + gate_ref[...] * acc[...]
            else:
                o_ref[...] = acc[...].astype(out_dtype)

    out_shape2 = (m, n) if nbo == 1 else (nbo, m, n)
    in_specs, args = [a_spec, b_spec], [a, b]
    out_specs, out_shape = o_spec, jax.ShapeDtypeStruct(out_shape2, out_dtype)
    if fused:
        in_specs += [o_spec, pl.BlockSpec((1, tn), lambda bo, i, j, bk, kk: (0, j))]
        args += [res, gate]
        out_specs = [o_spec, o_spec]
        out_shape = [jax.ShapeDtypeStruct(out_shape2, F32)] * 2
    return pl.pallas_call(
        body, grid=(nbo, m // tm, n // tn, nbk, nk), in_specs=in_specs, out_specs=out_specs, out_shape=out_shape,
        scratch_shapes=[pltpu.VMEM((tm, tn), F32)],
        compiler_params=_cp("parallel", "parallel", "parallel", "arbitrary", "arbitrary"), name=name,
    )(*args)


def _row(tm, c, off=0):
    return pl.BlockSpec((tm, c), lambda i: (i + off, 0))


def _full(shape):
    return pl.BlockSpec(shape, lambda *_: (0,) * len(shape))


def _acc(ref, val, first):
    @pl.when(first)
    def _():
        ref[...] = jnp.zeros_like(ref)

    ref[...] += val


def _modnorm_f(x, g, sc, sh):
    return (_rms(x) * g) * (1.0 + sc) + sh


def _cls_spec(ncls, nctx, d):
    if ncls == 2:
        return pl.BlockSpec((1, 1, d), lambda i: (jnp.where(i < nctx, 0, 1), 0, 0))
    return pl.BlockSpec((1, 1, d), lambda i: (0, 0, 0))


def _modnorm(x, g, sc, sh, *, nctx, tm, name):
    r, d = x.shape
    cls = _cls_spec(sc.shape[0], nctx, d)

    def body(x_ref, g_ref, sc_ref, sh_ref, o_ref):
        o_ref[...] = _modnorm_f(x_ref[...], g_ref[...], sc_ref[0], sh_ref[0]).astype(BF)

    return pl.pallas_call(
        body, grid=(r // tm,), in_specs=[_row(tm, d), _full((1, d)), cls, cls], out_specs=_row(tm, d),
        out_shape=jax.ShapeDtypeStruct((r, d), BF), compiler_params=_cp("parallel"), name=name,
    )(x, g, sc, sh)


def _modnorm_bwd(x, dh, dres, g, sc, sh, *, nctx, tm, name):
    r, d = x.shape
    ncls = sc.shape[0]
    s = r - nctx * tm
    cls = _cls_spec(ncls, nctx, d)
    lat = pl.BlockSpec((tm, d), lambda i: (jnp.maximum(i - nctx, 0), 0))

    def body(x_ref, dh_ref, dres_ref, g_ref, sc_ref, sh_ref, dx_ref, dg_ref, dsc_ref, dsh_ref):
        i = pl.program_id(0)
        _, vjp = jax.vjp(_modnorm_f, x_ref[...], g_ref[...], sc_ref[0], sh_ref[0])
        dx, dg, dsc, dsh = vjp(dh_ref[...])
        _acc(dg_ref, dg, i == 0)
        first = (i == 0) | (i == nctx) if ncls == 2 else i == 0
        _acc(dsc_ref, dsc[None], first)
        _acc(dsh_ref, dsh[None], first)

        @pl.when(i >= nctx)
        def _():
            dx_ref[...] = dx + dres_ref[...]

    return pl.pallas_call(
        body, grid=(r // tm,), in_specs=[_row(tm, d), _row(tm, d), lat, _full((1, d)), cls, cls],
        out_specs=[lat, _full((1, d)), cls, cls],
        out_shape=[jax.ShapeDtypeStruct((s, d), F32), jax.ShapeDtypeStruct((1, d), F32),
                   jax.ShapeDtypeStruct((ncls, 1, d), F32), jax.ShapeDtypeStruct((ncls, 1, d), F32)],
        compiler_params=_cp("arbitrary"), name=name,
    )(x, dh, dres, g, sc, sh)


def _even_parts(p, c, ql, kvl):
    return p[:, :c], p[:, c:2 * c], p[:, 2 * c:2 * c + ql], p[:, 2 * c + ql:2 * c + ql + kvl]


def _even_mid_f(val, gate, cq, ckv, qa_g, kva_g):
    return val * jax.nn.sigmoid(gate), _rms(cq) * qa_g, _rms(ckv) * kva_g


def _even_mid(p, qa_g, kva_g, *, c, tm, name):
    r, w = p.shape
    ql, kvl = qa_g.shape[1], kva_g.shape[1]

    def body(p_ref, qg_ref, kg_ref, gl_ref, qn_ref, kvn_ref):
        gl, qn, kvn = _even_mid_f(*_even_parts(p_ref[...], c, ql, kvl), qg_ref[...], kg_ref[...])
        gl_ref[...] = gl
        qn_ref[...] = qn.astype(BF)
        kvn_ref[...] = kvn.astype(BF)

    return pl.pallas_call(
        body, grid=(r // tm,), in_specs=[_row(tm, w), _full((1, ql)), _full((1, kvl))],
        out_specs=[_row(tm, c), _row(tm, ql), _row(tm, kvl)],
        out_shape=[jax.ShapeDtypeStruct((r, c), F32), jax.ShapeDtypeStruct((r, ql), BF), jax.ShapeDtypeStruct((r, kvl), BF)],
        compiler_params=_cp("parallel"), name=name,
    )(p, qa_g, kva_g)


def _even_mid_bwd(p, dgl, dqn, dkvn, dkr, qa_g, kva_g, *, c, tm, name):
    r, w = p.shape
    ql, kvl = qa_g.shape[1], kva_g.shape[1]
    tail = w - (2 * c + ql + kvl + LANES)

    def body(p_ref, dgl_ref, dqn_ref, dkvn_ref, dkr_ref, qg_ref, kg_ref, dp_ref, dqg_ref, dkg_ref):
        i = pl.program_id(0)
        _, vjp = jax.vjp(_even_mid_f, *_even_parts(p_ref[...], c, ql, kvl), qg_ref[...], kg_ref[...])
        dval, dgate, dcq, dckv, dqg, dkg = vjp((dgl_ref[...], dqn_ref[...], dkvn_ref[...]))
        parts = [dval, dgate, dcq, dckv, dkr_ref[...]]
        if tail:
            parts.append(jnp.zeros((tm, tail), F32))
        dp_ref[...] = jnp.concatenate(parts, axis=1).astype(BF)
        _acc(dqg_ref, dqg, i == 0)
        _acc(dkg_ref, dkg, i == 0)

    return pl.pallas_call(
        body, grid=(r // tm,),
        in_specs=[_row(tm, w), _row(tm, c), _row(tm, ql), _row(tm, kvl), _row(tm, LANES), _full((1, ql)), _full((1, kvl))],
        out_specs=[_row(tm, w), _full((1, ql)), _full((1, kvl))],
        out_shape=[jax.ShapeDtypeStruct((r, w), BF), jax.ShapeDtypeStruct((1, ql), F32), jax.ShapeDtypeStruct((1, kvl), F32)],
        compiler_params=_cp("arbitrary"), name=name,
    )(p, dgl, dqn, dkvn, dkr, qa_g, kva_g)


def _halo_specs(tm, tc, total_rows, col_off=0, row_off=0):
    hb = HALO_ROWS
    nb = total_rows // hb
    cur = pl.BlockSpec((tm, tc), lambda j, i: (i + row_off // tm, j + col_off))
    prev = pl.BlockSpec((hb, tc), lambda j, i: (jnp.maximum((i * tm + row_off) // hb - 1, 0), j + col_off))
    nxt = pl.BlockSpec((hb, tc), lambda j, i: (jnp.minimum(((i + 1) * tm + row_off) // hb, nb - 1), j + col_off))
    return [prev, cur, nxt]


def _halo_specs_blk(tm, tc, total_rows, pair):
    hb = HALO_ROWS
    nb = total_rows // hb
    head = (2, None) if pair else (None,)
    idx = (lambda j, r: (0, j, r, 0)) if pair else (lambda j, r: (j, r, 0))
    cur = pl.BlockSpec(head + (tm, tc), lambda j, i: idx(j, i))
    prev = pl.BlockSpec(head + (hb, tc), lambda j, i: idx(j, jnp.maximum(i * tm // hb - 1, 0)))
    nxt = pl.BlockSpec(head + (hb, tc), lambda j, i: idx(j, jnp.minimum((i + 1) * tm // hb, nb - 1)))
    return [prev, cur, nxt]


def _ext(prev_ref, cur_ref, next_ref, i, n):
    prev = jnp.where(i == 0, 0.0, prev_ref[...].astype(F32))
    nxt = jnp.where(i == n - 1, 0.0, next_ref[...].astype(F32))
    return jnp.concatenate([prev, cur_ref[...].astype(F32), nxt], axis=-2)


def _sroll(x, shift):
    return pltpu.roll(x, shift % x.shape[0], 0)


def _conv_taps(e, w_ref, ksize, sign):
    pad = (ksize - 1) // 2
    out = None
    for k in range(ksize):
        t = w_ref[k:k + 1, :] * _sroll(e, -sign * (k - pad))
        out = t if out is None else out + t
    return out


def _core(e, tm):
    return e[HALO_ROWS:HALO_ROWS + tm]


def _ln_silu_f(u, g, b):
    mu = jnp.mean(u, axis=-1, keepdims=True)
    xc = u - mu
    y = xc * lax.rsqrt(jnp.mean(xc * xc, axis=-1, keepdims=True) + EPS)
    return jax.nn.silu(y * g + b)


def _conv_ln_silu(gl_all, w, b, ln_g, ln_b, *, s, row_off, tm, name):
    c = gl_all.shape[1]
    ksize = w.shape[0]
    n = s // tm

    def body(p_ref, c_ref, n_ref, w_ref, b_ref, g_ref, lb_ref, u_ref, a_ref):
        i = pl.program_id(1)
        e = _ext(p_ref, c_ref, n_ref, i, n)
        u = _core(_conv_taps(e, w_ref, ksize, 1), tm) + b_ref[...]
        u_ref[...] = u
        a_ref[...] = _ln_silu_f(u, g_ref[...], lb_ref[...]).astype(BF)

    out = pl.BlockSpec((tm, c), lambda j, i: (i, 0))
    return pl.pallas_call(
        body, grid=(1, n),
        in_specs=_halo_specs(tm, c, gl_all.shape[0], row_off=row_off) + [_full((ksize, c)), _full((1, c)), _full((1, c)), _full((1, c))],
        out_specs=[out, out], out_shape=[jax.ShapeDtypeStruct((s, c), F32), jax.ShapeDtypeStruct((s, c), BF)],
        compiler_params=_cp("parallel", "parallel"), name=name,
    )(gl_all, gl_all, gl_all, w, b, ln_g, ln_b)


def _ln_silu_bwd(u, dcat, ln_g, ln_b, *, tm, name):
    s, c = u.shape
    wc = dcat.shape[1]

    def body(u_ref, dc_ref, g_ref, b_ref, du_ref, dg_ref, db_ref):
        i = pl.program_id(0)
        _, vjp = jax.vjp(_ln_silu_f, u_ref[...], g_ref[...], b_ref[...])
        du, dg, db = vjp(dc_ref[:, :c])
        du_ref[...] = du
        _acc(dg_ref, dg, i == 0)
        _acc(db_ref, db, i == 0)

    return pl.pallas_call(
        body, grid=(s // tm,), in_specs=[_row(tm, c), _row(tm, wc), _full((1, c)), _full((1, c))],
        out_specs=[_row(tm, c), _full((1, c)), _full((1, c))],
        out_shape=[jax.ShapeDtypeStruct((s, c), F32), jax.ShapeDtypeStruct((1, c), F32), jax.ShapeDtypeStruct((1, c), F32)],
        compiler_params=_cp("arbitrary"), name=name,
    )(u, dcat, ln_g, ln_b)


def _dw_taps(dw_ref, g_core, e, ksize, tm, first):
    pad = (ksize - 1) // 2

    @pl.when(first)
    def _():
        dw_ref[...] = jnp.zeros_like(dw_ref)

    for k in range(ksize):
        dw_ref[k:k + 1, :] += jnp.sum(g_core * _core(_sroll(e, -(k - pad)), tm), axis=0, keepdims=True)


def _conv_bwd(du, gl_all, w, *, row_off, tm, name):
    s, c = du.shape
    ksize = w.shape[0]
    n = s // tm

    def body(dp_ref, dc_ref, dn_ref, gp_ref, gc_ref, gn_ref, w_ref, dgl_ref, dw_ref, db_ref):
        i = pl.program_id(1)
        de = _ext(dp_ref, dc_ref, dn_ref, i, n)
        ge = _ext(gp_ref, gc_ref, gn_ref, i, n)
        dgl_ref[...] = _core(_conv_taps(de, w_ref, ksize, -1), tm)
        du_core = dc_ref[...]
        _dw_taps(dw_ref, du_core, ge, ksize, tm, i == 0)
        _acc(db_ref, jnp.sum(du_core, axis=0, keepdims=True), i == 0)

    return pl.pallas_call(
        body, grid=(1, n),
        in_specs=_halo_specs(tm, c, s) + _halo_specs(tm, c, gl_all.shape[0], row_off=row_off) + [_full((ksize, c))],
        out_specs=[pl.BlockSpec((tm, c), lambda j, i: (i, 0)), _full((ksize, c)), _full((1, c))],
        out_shape=[jax.ShapeDtypeStruct((s, c), F32), jax.ShapeDtypeStruct((ksize, c), F32), jax.ShapeDtypeStruct((1, c), F32)],
        compiler_params=_cp("arbitrary", "arbitrary"), name=name,
    )(du, du, du, gl_all, gl_all, gl_all, w)


def _ffn_mid(u, w, b, *, tm, name):
    _, nb, s, tc = u.shape
    ksize = w.shape[1]
    n = s // tm

    def body(p_ref, c_ref, n_ref, w_ref, b_ref, z_ref):
        i = pl.program_id(1)
        e = _ext(p_ref, c_ref, n_ref, i, n)
        cg = _core(_conv_taps(e[0], w_ref, ksize, 1), tm) + b_ref[...]
        z_ref[...] = (jax.nn.silu(cg) * c_ref[1]).astype(BF)

    blk = lambda r: pl.BlockSpec((None, r, tc), lambda j, i: (j, 0, 0))
    return pl.pallas_call(
        body, grid=(nb, n), in_specs=_halo_specs_blk(tm, tc, s, True) + [blk(ksize), blk(1)],
        out_specs=pl.BlockSpec((None, tm, tc), lambda j, i: (j, i, 0)), out_shape=jax.ShapeDtypeStruct((nb, s, tc), BF),
        compiler_params=_cp("parallel", "parallel"), name=name,
    )(u, u, u, w, b)


def _ffn_mid_bwd(u, dz, w, b, *, tm, name):
    _, nb, s, tc = u.shape
    ksize = w.shape[1]
    n = s // tm

    def body(up_ref, uc_ref, un_ref, zp_ref, zc_ref, zn_ref, w_ref, b_ref, du_ref, dw_ref, db_ref):
        i = pl.program_id(1)
        ue = _ext(up_ref, uc_ref, un_ref, i, n)
        ge, ve = ue[0], ue[1]
        ze = _ext(zp_ref, zc_ref, zn_ref, i, n)
        cg = _conv_taps(ge, w_ref, ksize, 1) + b_ref[...]
        sg = jax.nn.sigmoid(cg)
        dcg = ze * ve * (sg * (1.0 + cg * (1.0 - sg)))
        du_ref[0] = _core(_conv_taps(dcg, w_ref, ksize, -1), tm).astype(BF)
        du_ref[1] = _core(ze * cg * sg, tm).astype(BF)
        dcg_core = _core(dcg, tm)
        _dw_taps(dw_ref, dcg_core, ge, ksize, tm, i == 0)
        _acc(db_ref, jnp.sum(dcg_core, axis=0, keepdims=True), i == 0)

    blk = lambda r: pl.BlockSpec((None, r, tc), lambda j, i: (j, 0, 0))
    return pl.pallas_call(
        body, grid=(nb, n), in_specs=_halo_specs_blk(tm, tc, s, True) + _halo_specs_blk(tm, tc, s, False) + [blk(ksize), blk(1)],
        out_specs=[pl.BlockSpec((2, None, tm, tc), lambda j, i: (0, j, i, 0)), blk(ksize), blk(1)],
        out_shape=[jax.ShapeDtypeStruct((2, nb, s, tc), BF), jax.ShapeDtypeStruct((nb, ksize, tc), F32), jax.ShapeDtypeStruct((nb, 1, tc), F32)],
        compiler_params=_cp("parallel", "arbitrary"), name=name,
    )(u, u, u, dz, dz, dz, w, b)


def _odd_mid(r3, w, b, *, tm, name):
    s, d3 = r3.shape
    d = d3 // 3
    ksize = w.shape[0]
    tc = _div(d, 1024)
    nc, n = d // tc, s // tm

    def body(bg_ref, cp_ref, cc_ref, cn_ref, up_ref, uc_ref, un_ref, w_ref, b_ref, m_ref):
        i = pl.program_id(1)
        pe = _ext(cp_ref, cc_ref, cn_ref, i, n) * _ext(up_ref, uc_ref, un_ref, i, n)
        sv = _core(_conv_taps(pe, w_ref, ksize, 1), tm) + b_ref[...]
        m_ref[...] = (bg_ref[...] * sv).astype(BF)

    cb = lambda j, i: (0, j)
    return pl.pallas_call(
        body, grid=(nc, n),
        in_specs=[pl.BlockSpec((tm, tc), lambda j, i: (i, j))] + _halo_specs(tm, tc, s, col_off=nc) + _halo_specs(tm, tc, s, col_off=2 * nc)
        + [pl.BlockSpec((ksize, tc), cb), pl.BlockSpec((1, tc), cb)],
        out_specs=pl.BlockSpec((tm, tc), lambda j, i: (i, j)), out_shape=jax.ShapeDtypeStruct((s, d), BF),
        compiler_params=_cp("parallel", "parallel"), name=name,
    )(r3, r3, r3, r3, r3, r3, r3, w, b)


def _odd_mid_bwd(r3, dm, w, b, *, tm, name):
    s, d3 = r3.shape
    d = d3 // 3
    ksize = w.shape[0]
    tc = _div(d, 1024)
    nc, n = d // tc, s // tm

    def body(bp_ref, bc_ref, bn_ref, cp_ref, cc_ref, cn_ref, up_ref, uc_ref, un_ref, mp_ref, mc_ref, mn_ref, w_ref, b_ref,
             dbg_ref, dcg_ref, du_ref, dw_ref, db_ref):
        i = pl.program_id(1)
        be = _ext(bp_ref, bc_ref, bn_ref, i, n)
        ce = _ext(cp_ref, cc_ref, cn_ref, i, n)
        ue = _ext(up_ref, uc_ref, un_ref, i, n)
        me = _ext(mp_ref, mc_ref, mn_ref, i, n)
        pe = ce * ue
        sv = _conv_taps(pe, w_ref, ksize, 1) + b_ref[...]
        ds = me * be
        dp = _core(_conv_taps(ds, w_ref, ksize, -1), tm)
        dbg_ref[...] = _core(me * sv, tm).astype(BF)
        dcg_ref[...] = (dp * uc_ref[...]).astype(BF)
        du_ref[...] = (dp * cc_ref[...]).astype(BF)
        ds_core = _core(ds, tm)
        _dw_taps(dw_ref, ds_core, pe, ksize, tm, i == 0)
        _acc(db_ref, jnp.sum(ds_core, axis=0, keepdims=True), i == 0)

    cb = lambda j, i: (0, j)
    col = pl.BlockSpec((tm, tc), lambda j, i: (i, j))
    return pl.pallas_call(
        body, grid=(nc, n),
        in_specs=_halo_specs(tm, tc, s) + _halo_specs(tm, tc, s, col_off=nc) + _halo_specs(tm, tc, s, col_off=2 * nc) + _halo_specs(tm, tc, s)
        + [pl.BlockSpec((ksize, tc), cb), pl.BlockSpec((1, tc), cb)],
        out_specs=[col, col, col, pl.BlockSpec((ksize, tc), cb), pl.BlockSpec((1, tc), cb)],
        out_shape=[jax.ShapeDtypeStruct((s, d), BF)] * 3 + [jax.ShapeDtypeStruct((ksize, d), F32), jax.ShapeDtypeStruct((1, d), F32)],
        compiler_params=_cp("parallel", "arbitrary"), name=name,
    )(r3, r3, r3, r3, r3, r3, r3, r3, r3, dm, dm, dm, w, b)


def _gate_bwd(dx, f, g, *, tm, name):
    s, d = dx.shape

    def body(dx_ref, f_ref, g_ref, df_ref, dg_ref):
        i = pl.program_id(0)
        dxv = dx_ref[...]
        df_ref[...] = (dxv * g_ref[...]).astype(BF)
        _acc(dg_ref, jnp.sum(dxv * f_ref[...], axis=0, keepdims=True), i == 0)

    return pl.pallas_call(
        body, grid=(s // tm,), in_specs=[_row(tm, d), _row(tm, d), _full((1, d))], out_specs=[_row(tm, d), _full((1, d))],
        out_shape=[jax.ShapeDtypeStruct((s, d), BF), jax.ShapeDtypeStruct((1, d), F32)], compiler_params=_cp("arbitrary"), name=name,
    )(dx, f, g)


def _loss_head(y, target, *, tm, name):
    s, d = y.shape

    def body(y_ref, t_ref, dy_ref, l_ref):
        i = pl.program_id(0)
        err = y_ref[...] - t_ref[...]
        dy_ref[...] = err * (1.0 / d)
        _acc(l_ref, jnp.full((1, LANES), 0.5 / d, F32) * jnp.sum(err * err), i == 0)

    return pl.pallas_call(
        body, grid=(s // tm,), in_specs=[_row(tm, d), _row(tm, d)], out_specs=[_row(tm, d), _full((1, LANES))],
        out_shape=[jax.ShapeDtypeStruct((s, d), F32), jax.ShapeDtypeStruct((1, LANES), F32)], compiler_params=_cp("arbitrary"), name=name,
    )(y, target)


def _rope(n, cos, sa, sb):
    return n * cos + _lane_roll(LANES - 8)(n) * sa + _lane_roll(8)(n) * sb


def _q_post_f(q, g, cos, sa, sb):
    outs = []
    for h in range(q.shape[1] // LANES):
        outs.append(_rope(_rms(q[:, h * LANES:(h + 1) * LANES], QK_DIM) * g, cos, sa, sb) * Q_PRESCALE)
    return jnp.concatenate(outs, axis=1)


def _kv_post_f(kv, kr, g, cos, sa, sb):
    lane = lax.broadcasted_iota(jnp.int32, (1, LANES), 1)
    krs = jnp.where((lane >= QK_NOPE) & (lane < QK_DIM), _lane_roll(QK_NOPE)(kr), 0.0)
    ks, vs = [], []
    for h in range(kv.shape[1] // LANES):
        kvh = kv[:, h * LANES:(h + 1) * LANES]
        ks.append(_rope(_rms(jnp.where(lane < QK_NOPE, kvh, krs), QK_DIM) * g, cos, sa, sb))
        vs.append(jnp.where(lane < V_DIM, _lane_roll(LANES - QK_NOPE)(kvh), 0.0))
    return jnp.concatenate(ks, axis=1), jnp.concatenate(vs, axis=1)


def _tab_specs(tm):
    return [_row(tm, LANES)] * 3


def _q_post(q, g, tabs, *, tm, name):
    r, w = q.shape

    def body(q_ref, g_ref, c_ref, sa_ref, sb_ref, o_ref):
        o_ref[...] = _q_post_f(q_ref[...], g_ref[...], c_ref[...], sa_ref[...], sb_ref[...]).astype(BF)

    return pl.pallas_call(
        body, grid=(r // tm,), in_specs=[_row(tm, w), _full((1, LANES))] + _tab_specs(tm), out_specs=_row(tm, w),
        out_shape=jax.ShapeDtypeStruct((r, w), BF), compiler_params=_cp("parallel"), name=name,
    )(q, g, *tabs)


def _q_post_bwd(q, dqh, g, tabs, *, tm, name):
    r, w = q.shape

    def body(q_ref, d_ref, g_ref, c_ref, sa_ref, sb_ref, dq_ref, dg_ref):
        i = pl.program_id(0)
        f = lambda qv, gv: _q_post_f(qv, gv, c_ref[...], sa_ref[...], sb_ref[...])
        _, vjp = jax.vjp(f, q_ref[...], g_ref[...])
        dq, dg = vjp(d_ref[...])
        dq_ref[...] = dq.astype(BF)
        _acc(dg_ref, dg, i == 0)

    return pl.pallas_call(
        body, grid=(r // tm,), in_specs=[_row(tm, w), _row(tm, w), _full((1, LANES))] + _tab_specs(tm),
        out_specs=[_row(tm, w), _full((1, LANES))],
        out_shape=[jax.ShapeDtypeStruct((r, w), BF), jax.ShapeDtypeStruct((1, LANES), F32)], compiler_params=_cp("arbitrary"), name=name,
    )(q, dqh, g, *tabs)


def _kr_spec(tm, w):
    return pl.BlockSpec((tm, LANES), lambda i: (i, w // LANES - 1))


def _kv_post(kv, p, g, tabs, *, tm, name):
    r, w = kv.shape

    def body(kv_ref, kr_ref, g_ref, c_ref, sa_ref, sb_ref, k_ref, v_ref):
        k, v = _kv_post_f(kv_ref[...], kr_ref[...], g_ref[...], c_ref[...], sa_ref[...], sb_ref[...])
        k_ref[...] = k.astype(BF)
        v_ref[...] = v.astype(BF)

    return pl.pallas_call(
        body, grid=(r // tm,), in_specs=[_row(tm, w), _kr_spec(tm, p.shape[1]), _full((1, LANES))] + _tab_specs(tm),
        out_specs=[_row(tm, w), _row(tm, w)], out_shape=[jax.ShapeDtypeStruct((r, w), BF)] * 2,
        compiler_params=_cp("parallel"), name=name,
    )(kv, p, g, *tabs)


def _kv_post_bwd(kv, p, dk, dv, g, tabs, *, tm, name):
    r, w = kv.shape

    def body(kv_ref, kr_ref, dk_ref, dv_ref, g_ref, c_ref, sa_ref, sb_ref, dkv_ref, dkr_ref, dg_ref):
        i = pl.program_id(0)
        f = lambda kvv, krv, gv: _kv_post_f(kvv, krv, gv, c_ref[...], sa_ref[...], sb_ref[...])
        _, vjp = jax.vjp(f, kv_ref[...], kr_ref[...], g_ref[...])
        dkv, dkr, dg = vjp((dk_ref[...], dv_ref[...]))
        dkv_ref[...] = dkv.astype(BF)
        dkr_ref[...] = dkr
        _acc(dg_ref, dg, i == 0)

    return pl.pallas_call(
        body, grid=(r // tm,),
        in_specs=[_row(tm, w), _kr_spec(tm, p.shape[1]), _row(tm, w), _row(tm, w), _full((1, LANES))] + _tab_specs(tm),
        out_specs=[_row(tm, w), _row(tm, LANES), _full((1, LANES))],
        out_shape=[jax.ShapeDtypeStruct((r, w), BF), jax.ShapeDtypeStruct((r, LANES), F32), jax.ShapeDtypeStruct((1, LANES), F32)],
        compiler_params=_cp("arbitrary"), name=name,
    )(kv, p, dk, dv, g, *tabs)


_NT = (((1,), (1,)), ((), ()))
_TN = (((0,), (0,)), ((), ()))
_NN = (((1,), (0,)), ((), ()))


ATTN_TQ, ATTN_TK = 512, 768
ATTN_FWD_TQ, ATTN_FWD_TK = 1024, 768
Q_PRESCALE = SM_SCALE * 1.4426950408889634
LN2 = 0.6931471805599453


def _attn_tiles(s, nk):
    return _div(s, ATTN_TQ), _div(nk, ATTN_TK)


def _chunk(ref, i, n):
    return ref[pl.ds(pl.multiple_of(i * n, n), n), :]


def _attn_fwd(q, k, v, *, name):
    s, w = q.shape
    nk = k.shape[0]
    tq, tk = _div(s, ATTN_FWD_TQ), _div(nk, ATTN_FWD_TK)
    nj = nk // tk

    def body(q_ref, k_ref, v_ref, o_ref, lse_ref, m_s, l_s, acc_s):
        m_s[...] = jnp.full_like(m_s, -jnp.inf)
        l_s[...] = jnp.zeros_like(l_s)
        acc_s[...] = jnp.zeros_like(acc_s)
        qv = q_ref[...]

        def step(j, carry):
            sc = lax.dot_general(qv, _chunk(k_ref, j, tk), _NT, preferred_element_type=F32)
            m_prev = m_s[...]
            m_new = jnp.maximum(m_prev, jnp.max(sc, axis=1, keepdims=True))
            alpha = jnp.exp2(m_prev - m_new)
            pr = jnp.exp2(sc - m_new[:, :1])
            l_s[...] = alpha * l_s[...] + jnp.sum(pr, axis=1, keepdims=True)
            acc_s[...] = alpha * acc_s[...] + lax.dot_general(pr.astype(BF), _chunk(v_ref, j, tk), _NN, preferred_element_type=F32)
            m_s[...] = m_new
            return carry

        lax.fori_loop(0, nj, step, 0, unroll=True)
        o_ref[...] = (acc_s[...] / l_s[...]).astype(BF)
        lse_ref[...] = m_s[...] + jnp.log2(l_s[...])

    qs = pl.BlockSpec((tq, LANES), lambda h, i: (i, h))
    ks = pl.BlockSpec((nk, LANES), lambda h, i: (0, h))
    return pl.pallas_call(
        body, grid=(w // LANES, s // tq), in_specs=[qs, ks, ks], out_specs=[qs, qs],
        out_shape=[jax.ShapeDtypeStruct((s, w), BF), jax.ShapeDtypeStruct((s, w), F32)],
        scratch_shapes=[pltpu.VMEM((tq, LANES), F32)] * 3, compiler_params=_cp("parallel", "parallel"), name=name,
    )(q, k, v)


def _attn_delta(dcat, o, *, tm, name):
    s, w = o.shape
    wc = dcat.shape[1]

    def body(dc_ref, o_ref, do_ref, dl_ref):
        do = dc_ref[:, wc - w:]
        prod = do * o_ref[...].astype(F32)
        outs = []
        for h in range(w // LANES):
            outs.append(jnp.broadcast_to(jnp.sum(prod[:, h * LANES:(h + 1) * LANES], axis=1, keepdims=True), (tm, LANES)))
        do_ref[...] = do.astype(BF)
        dl_ref[...] = jnp.concatenate(outs, axis=1)

    return pl.pallas_call(
        body, grid=(s // tm,), in_specs=[_row(tm, wc), _row(tm, w)], out_specs=[_row(tm, w), _row(tm, w)],
        out_shape=[jax.ShapeDtypeStruct((s, w), BF), jax.ShapeDtypeStruct((s, w), F32)], compiler_params=_cp("parallel"), name=name,
    )(dcat, o)


def _attn_bwd(q, k, v, do, lse, delta, *, name):
    s, w = q.shape
    nk = k.shape[0]
    tq, tk = _attn_tiles(s, nk)
    ni, nj = s // tq, nk // tk

    def body(q_ref, k_ref, v_ref, do_ref, lse_ref, dl_ref, dq_ref, dk_ref, dv_ref, dk_s, dv_s):
        j = pl.program_id(1)

        @pl.when(j == 0)
        def _():
            dq_ref[...] = jnp.zeros_like(dq_ref)

        dk_s[...] = jnp.zeros_like(dk_s)
        dv_s[...] = jnp.zeros_like(dv_s)
        kv, vv = k_ref[...], v_ref[...]

        def step(i, carry):
            rows = pl.ds(pl.multiple_of(i * tq, tq), tq)
            qi, doi = q_ref[rows, :], do_ref[rows, :]
            sc = lax.dot_general(qi, kv, _NT, preferred_element_type=F32)
            pr = jnp.exp2(sc - lse_ref[rows, :][:, :1])
            dp = lax.dot_general(doi, vv, _NT, preferred_element_type=F32)
            ds = (pr * (dp - dl_ref[rows, :][:, :1])).astype(BF)
            dv_s[...] += lax.dot_general(pr.astype(BF), doi, _TN, preferred_element_type=F32)
            dk_s[...] += lax.dot_general(ds, qi, _TN, preferred_element_type=F32)
            dq_ref[rows, :] += lax.dot_general(ds, kv, _NN, preferred_element_type=F32)
            return carry

        lax.fori_loop(0, ni, step, 0, unroll=2 if ni % 2 == 0 else 1)
        dk_ref[...] = dk_s[...] * LN2
        dv_ref[...] = dv_s[...]

        @pl.when(j == nj - 1)
        def _():
            dq_ref[...] = dq_ref[...] * LN2

    qs = pl.BlockSpec((s, LANES), lambda h, j: (0, h))
    ks = pl.BlockSpec((tk, LANES), lambda h, j: (j, h))
    return pl.pallas_call(
        body, grid=(w // LANES, nj), in_specs=[qs, ks, ks, qs, qs, qs], out_specs=[qs, ks, ks],
        out_shape=[jax.ShapeDtypeStruct((s, w), F32), jax.ShapeDtypeStruct((nk, w), F32), jax.ShapeDtypeStruct((nk, w), F32)],
        scratch_shapes=[pltpu.VMEM((tk, LANES), F32)] * 2, compiler_params=_cp("parallel", "arbitrary"), name=name,
    )(q, k, v, do, lse, delta)


def _adamw(w, g, m, v, *, name):
    r, c = w.shape
    tr = r
    for cand in (512, 256, 128, 64, 32, 16, 8):
        if r % cand == 0 and cand * c * 4 <= (2 << 20):
            tr = cand
            break
    bc1 = 1.0 - ADAM_B1 ** ADAM_STEP
    bc2 = 1.0 - ADAM_B2 ** ADAM_STEP

    def body(w_ref, g_ref, m_ref, v_ref, d_ref, nm_ref, nv_ref):
        gv = g_ref[...]
        nm = ADAM_B1 * m_ref[...] + (1.0 - ADAM_B1) * gv
        nv = ADAM_B2 * v_ref[...] + (1.0 - ADAM_B2) * (gv * gv)
        d_ref[...] = -ADAM_LR * ((nm / bc1) / (jnp.sqrt(nv / bc2) + ADAM_EPS) + ADAM_WD * w_ref[...])
        nm_ref[...] = nm
        nv_ref[...] = nv

    spec = _row(tr, c)
    return pl.pallas_call(
        body, grid=(r // tr,), in_specs=[spec] * 4, out_specs=[spec] * 3,
        out_shape=[jax.ShapeDtypeStruct((r, c), F32)] * 3, compiler_params=_cp("parallel"), name=name,
    )(w, g, m, v)


def _mesh_pos():
    return lax.axis_index("x"), lax.axis_index("y"), lax.axis_index("c")


def _allgather(x, *, in_vmem, name):
    r, c = x.shape
    spec = pl.BlockSpec(memory_space=pltpu.VMEM if in_vmem else pl.ANY)

    def body(x_ref, out_ref, send_sems, recv_sems, local_sem):
        ix, iy, ic = _mesh_pos()
        me, sibling = (ix, iy, ic), (ix, iy, 1 - ic)
        chips = [(1 - ix, iy), (ix, 1 - iy), (1 - ix, 1 - iy)]

        def slab(px, py, pc):
            return out_ref.at[4 * px + 2 * py + pc]

        def copy(k, block, to, src=None):
            return pltpu.make_async_remote_copy(
                src_ref=slab(*block) if src is None else src, dst_ref=slab(*block),
                send_sem=send_sems.at[k], recv_sem=recv_sems.at[k], device_id=to, device_id_type=pl.DeviceIdType.MESH)

        mine = pltpu.make_async_copy(x_ref, slab(*me), local_sem)
        mine.start()
        first = [copy(0, me, sibling, src=x_ref)]
        first += [copy(1 + j, me, (*chip, ic), src=x_ref) for j, chip in enumerate(chips)]
        for cp in first:
            cp.start()
        passed = [copy(4 + j, (*chip, ic), sibling) for j, chip in enumerate(chips)]
        for j, chip in enumerate(chips):
            copy(1 + j, (*chip, ic), me).wait_recv()
            passed[j].start()
        copy(0, sibling, me).wait_recv()
        for j, chip in enumerate(chips):
            copy(4 + j, (*chip, 1 - ic), me).wait_recv()
        for cp in first + passed:
            cp.wait_send()
        mine.wait()

    return pl.pallas_call(
        body, out_shape=jax.ShapeDtypeStruct((N_DEV, r, c), x.dtype), in_specs=[spec], out_specs=spec,
        scratch_shapes=[pltpu.SemaphoreType.DMA((7,)), pltpu.SemaphoreType.DMA((7,)), pltpu.SemaphoreType.DMA], name=name,
    )(x)


def _exchange(g, *, name):
    _, r, c = g.shape
    spec = pl.BlockSpec(memory_space=pl.ANY)

    def body(g_ref, out_ref, send_sems, recv_sems, local_sem):
        ix, iy, ic = _mesh_pos()
        me = 4 * ix + 2 * iy + ic
        mine = pltpu.make_async_copy(g_ref.at[me], out_ref.at[me], local_sem)
        mine.start()
        sends, recvs = [], []
        for k in range(1, N_DEV):
            px = 1 - ix if k & 4 else ix
            py = 1 - iy if k & 2 else iy
            pc = 1 - ic if k & 1 else ic
            peer = 4 * px + 2 * py + pc
            mk = lambda src, dst: pltpu.make_async_remote_copy(
                src_ref=g_ref.at[src], dst_ref=out_ref.at[dst], send_sem=send_sems.at[k - 1], recv_sem=recv_sems.at[k - 1],
                device_id=(px, py, pc), device_id_type=pl.DeviceIdType.MESH)
            sends.append(mk(peer, me))
            recvs.append(mk(me, peer))
        for cp in sends:
            cp.start()
        for cp in recvs:
            cp.wait_recv()
        for cp in sends:
            cp.wait_send()
        mine.wait()

    return pl.pallas_call(
        body, out_shape=jax.ShapeDtypeStruct(g.shape, g.dtype), in_specs=[spec], out_specs=spec,
        scratch_shapes=[pltpu.SemaphoreType.DMA((7,)), pltpu.SemaphoreType.DMA((7,)), pltpu.SemaphoreType.DMA], name=name,
    )(g)


def _sum8(a, *, name):
    _, r, c = a.shape
    tr = r
    for cand in (512, 256, 128, 64, 32, 16):
        if r % cand == 0 and cand * c * 4 <= (1 << 20):
            tr = cand
            break

    def body(a_ref, o_ref):
        acc = a_ref[0].astype(F32)
        for d in range(1, N_DEV):
            acc = acc + a_ref[d].astype(F32)
        o_ref[...] = acc

    return pl.pallas_call(
        body, grid=(r // tr,), in_specs=[pl.BlockSpec((N_DEV, tr, c), lambda i: (0, i, 0))], out_specs=_row(tr, c),
        out_shape=jax.ShapeDtypeStruct((r, c), F32), compiler_params=_cp("parallel"), name=name,
    )(a)


ADA_ROWS = 16


def _silu_rows(c8, c_ctx):
    d = c8.shape[1]
    rows = jnp.concatenate([c8, c_ctx, jnp.zeros((ADA_ROWS - N_DEV - 1, d), F32)], axis=0)
    return jax.nn.silu(rows)


def _ada_fwd(c8, c_ctx, ada_w, ada_b_cols, *, name):
    nl, d, cols = ada_w.shape

    def body(c8_ref, cc_ref, w_ref, b_ref, o_ref):
        sc = _silu_rows(c8_ref[...], cc_ref[...]).astype(BF)
        for l in range(nl):
            o_ref[l] = lax.dot_general(sc, w_ref[l].astype(BF), _NN, preferred_element_type=F32) + b_ref[l:l + 1, :]

    return pl.pallas_call(
        body, out_shape=jax.ShapeDtypeStruct((nl, ADA_ROWS, cols), F32),
        compiler_params=pltpu.CompilerParams(vmem_limit_bytes=VMEM_LIMIT), name=name,
    )(c8, c_ctx, ada_w, ada_b_cols)


def _ada_bwd(c8, c_ctx, ada_w, g16, dctx_cols, tot_dm, dmodc_pad, *, name):
    nl, d, cols = ada_w.shape
    hi = lax.Precision.HIGHEST

    def body(c8_ref, cc_ref, w_ref, g_ref, dc_ref, tot_ref, dmc_ref, dw_ref, db_ref, part_ref):
        sc = _silu_rows(c8_ref[...], cc_ref[...])
        for l in range(nl):
            dw_ref[l] = lax.dot_general(sc, g_ref[l], _TN, precision=hi, preferred_element_type=F32)
        db_ref[...] = tot_ref[...]
        db_ref[0:1, :] += dmc_ref[...]
        ccv = cc_ref[...]
        sg = jax.nn.sigmoid(ccv)
        dsilu = sg * (1.0 + ccv * (1.0 - sg))
        part = lax.dot_general(dc_ref[...], w_ref[0], _NT, precision=hi, preferred_element_type=F32) * dsilu
        part_ref[...] = jnp.concatenate([part, jnp.zeros((SUBLANES - 1, d), F32)], axis=0)

    return pl.pallas_call(
        body, out_shape=[jax.ShapeDtypeStruct((nl, d, cols), F32), jax.ShapeDtypeStruct(tot_dm.shape, F32),
                         jax.ShapeDtypeStruct((SUBLANES, d), F32)],
        compiler_params=pltpu.CompilerParams(vmem_limit_bytes=VMEM_LIMIT), name=name,
    )(c8, c_ctx, ada_w, g16, dctx_cols, tot_dm, dmodc_pad)


def _rope_tables(s, lc):
    t = jnp.arange(s)
    half = QK_ROPE // 2
    inv = ROPE_THETA ** (-jnp.arange(0, half, 2, dtype=F32) / half)
    ang_r = (t // GRID_W).astype(F32)[:, None] * inv[None, :]
    ang_c = (t % GRID_W).astype(F32)[:, None] * inv[None, :]
    ang = jnp.concatenate([ang_r, ang_r, ang_c, ang_c], axis=-1)
    cos, sin = jnp.cos(ang), jnp.sin(ang)
    first = (jnp.arange(QK_ROPE) % half) < half // 2
    sa, sb = jnp.where(first, -sin, 0.0), jnp.where(first, 0.0, sin)

    def slot(mid, fill):
        body = jnp.concatenate([jnp.full((s, QK_NOPE), fill, F32), mid, jnp.full((s, LANES - QK_DIM), fill, F32)], axis=1)
        return jnp.concatenate([jnp.full((lc, LANES), fill, F32), body], axis=0)

    return slot(cos, 1.0), slot(sa, 0.0), slot(sb, 0.0)


def _pad_last(a, n):
    return jnp.pad(a, [(0, 0)] * (a.ndim - 1) + [(0, n - a.shape[-1])])


def _local_step(x, ctx, target, mods, modc, w):
    s, d = x.shape
    lc = ctx.shape[0]
    c = d // 2
    nh = (d - c) // V_DIM
    hw = nh * LANES
    ql, kvl = w["ev_qa_norm_g"].shape[-1], w["ev_kva_norm_g"].shape[-1]
    ei = 2 * c + ql + kvl + QK_ROPE
    eip = 2 * c + ql + kvl + LANES
    tm = 256 if (s % 256 == 0 and lc % 256 == 0) else 128
    nctx = lc // tm
    row = lambda v: v.reshape(1, -1)
    cls1 = lambda v: v.reshape(1, 1, -1)

    w_inp = _pad_last(w["ev_w_in"][0], eip)
    w_uqp = _pad_last(w["ev_w_uq"][0].reshape(ql, nh, QK_DIM), LANES).reshape(ql, hw)
    w_ukv = w["ev_w_ukv"][0]
    w_out = w["ev_w_out"][0]
    w_att = jnp.pad(w_out[c:].reshape(nh, V_DIM, d), [(0, 0), (0, LANES - V_DIM), (0, 0)]).reshape(hw, d)
    w_outp = jnp.concatenate([w_out[:c], w_att], axis=0)
    qg = _pad_last(row(w["ev_q_norm_g"]), LANES)
    kg = _pad_last(row(w["ev_k_norm_g"]), LANES)
    qa_g, kva_g = row(w["ev_qa_norm_g"]), row(w["ev_kva_norm_g"])
    ev_cw, ev_cb = w["ev_conv_w"][0], row(w["ev_conv_b"])
    ln_g, ln_b = row(w["ev_ln_g"]), row(w["ev_ln_b"])
    tabs = _rope_tables(s, lc)

    xall = jnp.concatenate([ctx, x], axis=0)
    sc0 = jnp.stack([modc[1], mods[0, 1]])[:, None, :]
    sh0 = jnp.stack([modc[0], mods[0, 0]])[:, None, :]
    g_mix0 = row(w["norm_mix_g"][0])
    hall = _modnorm(xall, g_mix0, sc0, sh0, nctx=nctx, tm=tm, name="l0_mix_norm")
    p_all = _mm(hall, w_inp, name="l0_w_in")
    gl_all, qn_all, kvn_all = _even_mid(p_all, qa_g, kva_g, c=c, tm=tm, name="l0_even_mid")
    u, a = _conv_ln_silu(gl_all, ev_cw, ev_cb, ln_g, ln_b, s=s, row_off=lc, tm=tm, name="l0_conformer")
    q_all = _mm(qn_all, w_uqp, name="l0_w_uq")
    kv_all = _mm(kvn_all, w_ukv, name="l0_w_ukv")
    qh_all = _q_post(q_all, qg, tabs, tm=tm, name="l0_q_post")
    k_all, v_all = _kv_post(kv_all, p_all, kg, tabs, tm=tm, name="l0_kv_post")
    qh = qh_all[lc:]
    o, lse = _attn_fwd(qh, k_all, v_all, name="l0_attn_fwd")
    cat = jnp.concatenate([a, o], axis=1)
    x1, f_mix0 = _mm(cat, w_outp, res=x, gate=row(mods[0, 2]), name="l0_w_out")

    nb = N_DEV // 2
    ffn_dim = w["ffn_conv_b"].shape[-1]
    tc = ffn_dim // nb
    ffn_k = w["ffn_conv_w"].shape[1]
    ffn_cw = [jnp.transpose(w["ffn_conv_w"][l].reshape(ffn_k, nb, tc), (1, 0, 2)) for l in range(2)]
    ffn_cb = [w["ffn_conv_b"][l].reshape(nb, 1, tc) for l in range(2)]

    def ffn_fwd(l, x_in):
        hf = _modnorm(x_in, row(w["norm_ffn_g"][l]), cls1(mods[l, 4]), cls1(mods[l, 3]), nctx=0, tm=tm, name=f"l{l}_ffn_norm")
        uu = _bmm(hf, w["ffn_w_up8"][l], b_blk="o", name=f"l{l}_w_up").reshape(2, nb, s, tc)
        z = _ffn_mid(uu, ffn_cw[l], ffn_cb[l], tm=tm, name=f"l{l}_ffn_mid")
        x_out, f = _bmm(z, w["ffn_w_down4"][l], a_blk="k", b_blk="k", res=x_in, gate=row(mods[l, 5]), name=f"l{l}_w_down")
        return x_out, (x_in, hf, uu, z, f)

    x2, ffn0 = ffn_fwd(0, x1)
    h1 = _modnorm(x2, row(w["norm_mix_g"][1]), cls1(mods[1, 1]), cls1(mods[1, 0]), nctx=0, tm=tm, name="l1_mix_norm")
    r3 = _mm(h1, w["od_w_in"][0], name="l1_w_in")
    od_cw, od_cb = w["od_conv_w"][0], row(w["od_conv_b"])
    m1 = _odd_mid(r3, od_cw, od_cb, tm=tm, name="l1_odd_mid")
    x3, f_mix1 = _mm(m1, w["od_w_out"][0], res=x2, gate=row(mods[1, 2]), name="l1_w_out")
    x4, ffn1 = ffn_fwd(1, x3)
    dx, loss_row = _loss_head(x4, target, tm=tm, name="loss_head")

    g = {}
    dmods = [[None] * N_MOD for _ in range(2)]

    def ffn_bwd(l, dx, saved):
        x_in, hf, uu, z, f = saved
        df, dmods[l][5] = _gate_bwd(dx, f, row(mods[l, 5]), tm=tm, name=f"l{l}_ffn_gate_bwd")
        dz = _bmm(df, w["ffn_w_down4"][l], b_blk="o", tb=True, name=f"l{l}_w_down_dx")
        dwd = _bmm(z, df, a_blk="o", ta=True, out_dtype=BF, name=f"l{l}_w_down_dw")
        du, dcw, dcb = _ffn_mid_bwd(uu, dz, ffn_cw[l], ffn_cb[l], tm=tm, name=f"l{l}_ffn_mid_bwd")
        du = du.reshape(N_DEV, s, tc)
        dhf = _bmm(du, w["ffn_w_up8"][l], a_blk="k", b_blk="k", tb=True, name=f"l{l}_w_up_dx")
        dwu = _bmm(hf, du, b_blk="o", ta=True, out_dtype=BF, name=f"l{l}_w_up_dw")
        dx, dgn, dsc, dsh = _modnorm_bwd(x_in, dhf, dx, row(w["norm_ffn_g"][l]), cls1(mods[l, 4]), cls1(mods[l, 3]),
                                         nctx=0, tm=tm, name=f"l{l}_ffn_norm_bwd")
        dmods[l][4], dmods[l][3] = dsc, dsh
        return dx, dict(ffn_w_up8=dwu, ffn_w_down8=dwd.reshape(N_DEV, tc // 2, d),
                        ffn_conv_w=jnp.transpose(dcw, (1, 0, 2)).reshape(ffn_k, ffn_dim), ffn_conv_b=dcb.reshape(1, ffn_dim), norm_ffn_g=dgn)

    dx, gf1 = ffn_bwd(1, dx, ffn1)
    df, dmods[1][2] = _gate_bwd(dx, f_mix1, row(mods[1, 2]), tm=tm, name="l1_mix_gate_bwd")
    dm1 = _mm(df, w["od_w_out"][0], tb=True, name="l1_w_out_dx")
    g["od_w_out"] = _mm(m1, df, ta=True, name="l1_w_out_dw")[None]
    dbg, dcg, duu, dcw, dcb = _odd_mid_bwd(r3, dm1, od_cw, od_cb, tm=tm, name="l1_odd_mid_bwd")
    g["od_conv_w"], g["od_conv_b"] = dcw[None], dcb
    dr3 = jnp.concatenate([dbg, dcg, duu], axis=1)
    dh1 = _mm(dr3, w["od_w_in"][0], tb=True, name="l1_w_in_dx")
    g["od_w_in"] = _mm(h1, dr3, ta=True, name="l1_w_in_dw")[None]
    dx, dgn1, dsc, dsh = _modnorm_bwd(x2, dh1, dx, row(w["norm_mix_g"][1]), cls1(mods[1, 1]), cls1(mods[1, 0]),
                                      nctx=0, tm=tm, name="l1_mix_norm_bwd")
    dmods[1][1], dmods[1][0] = dsc, dsh
    dx, gf0 = ffn_bwd(0, dx, ffn0)
    df, dmods[0][2] = _gate_bwd(dx, f_mix0, row(mods[0, 2]), tm=tm, name="l0_mix_gate_bwd")
    dcat = _mm(df, w_outp, tb=True, name="l0_w_out_dx")
    dw_outp = _mm(cat, df, ta=True, name="l0_w_out_dw")
    du0, g["ev_ln_g"], g["ev_ln_b"] = _ln_silu_bwd(u, dcat, ln_g, ln_b, tm=tm, name="l0_ln_silu_bwd")
    dgl, dcw, g["ev_conv_b"] = _conv_bwd(du0, gl_all, ev_cw, row_off=lc, tm=tm, name="l0_conformer_conv_bwd")
    g["ev_conv_w"] = dcw[None]
    do, delta = _attn_delta(dcat, o, tm=tm, name="l0_attn_delta")
    dq, dk, dv = _attn_bwd(qh, k_all, v_all, do, lse, delta, name="l0_attn_bwd")
    dq_all = jnp.concatenate([jnp.zeros((lc, hw), F32), dq], axis=0)
    dgl_all = jnp.concatenate([jnp.zeros((lc, c), F32), dgl], axis=0)
    dqp, dqg = _q_post_bwd(q_all, dq_all, qg, tabs, tm=tm, name="l0_q_post_bwd")
    dkvp, dkr, dkg = _kv_post_bwd(kv_all, p_all, dk, dv, kg, tabs, tm=tm, name="l0_kv_post_bwd")
    dqn = _mm(dqp, w_uqp, tb=True, name="l0_w_uq_dx")
    dw_uqp = _mm(qn_all, dqp, ta=True, name="l0_w_uq_dw")
    dkvn = _mm(dkvp, w_ukv, tb=True, name="l0_w_ukv_dx")
    g["ev_w_ukv"] = _mm(kvn_all, dkvp, ta=True, name="l0_w_ukv_dw")[None]
    dp, g["ev_qa_norm_g"], g["ev_kva_norm_g"] = _even_mid_bwd(p_all, dgl_all, dqn, dkvn, dkr, qa_g, kva_g, c=c, tm=tm, name="l0_even_mid_bwd")
    dhall = _mm(dp, w_inp, tb=True, name="l0_w_in_dx")
    dw_inp = _mm(hall, dp, ta=True, name="l0_w_in_dw")
    dx, dgn0, dsc2, dsh2 = _modnorm_bwd(xall, dhall, dx, g_mix0, sc0, sh0, nctx=nctx, tm=tm, name="l0_mix_norm_bwd")
    dmods[0][1], dmods[0][0] = dsc2[1], dsh2[1]
    dmodc = jnp.concatenate([dsh2[0], dsc2[0]], axis=0)

    g["ev_w_in"] = dw_inp[:, :ei][None]
    g["ev_w_uq"] = dw_uqp.reshape(ql, nh, LANES)[:, :, :QK_DIM].reshape(ql, nh * QK_DIM)[None]
    g["ev_w_out"] = jnp.concatenate([dw_outp[:c], dw_outp[c:].reshape(nh, LANES, d)[:, :V_DIM].reshape(nh * V_DIM, d)], axis=0)[None]
    g["ev_q_norm_g"], g["ev_k_norm_g"] = dqg[:, :QK_DIM], dkg[:, :QK_DIM]
    g["norm_mix_g"] = jnp.concatenate([dgn0, dgn1], axis=0)
    for name in ("ffn_w_up8", "ffn_w_down8"):
        g[name] = [gf0[name], gf1[name]]
    g["ffn_conv_w"] = jnp.stack([gf0["ffn_conv_w"], gf1["ffn_conv_w"]])
    for name in ("ffn_conv_b", "norm_ffn_g"):
        g[name] = jnp.concatenate([gf0[name], gf1[name]], axis=0)
    dmods_arr = jnp.stack([jnp.concatenate([v.reshape(1, d) for v in dmods[l]], axis=0) for l in range(2)])
    return loss_row, dx, g, dmods_arr, dmodc


WEIGHTS = ("c_ctx", "ada_w", "ada_b", "norm_mix_g", "norm_ffn_g", "ffn_w_up", "ffn_conv_w", "ffn_conv_b", "ffn_w_down", "ev_w_in",
           "ev_conv_w", "ev_conv_b", "ev_ln_g", "ev_ln_b", "ev_qa_norm_g", "ev_w_uq", "ev_kva_norm_g", "ev_w_ukv", "ev_q_norm_g",
           "ev_k_norm_g", "ev_w_out", "od_w_in", "od_conv_w", "od_conv_b", "od_w_out")
SHARD_DIM = dict(ada_w=2, ffn_w_up=2, ffn_conv_w=2, ffn_w_down=1, ev_w_in=2, ev_conv_w=2, ev_w_uq=2, ev_w_ukv=2, ev_w_out=1,
                 od_w_in=2, od_conv_w=2, od_conv_b=1, od_w_out=1)
BIG = ("ffn_w_up", "ffn_w_down", "ev_w_in", "ev_w_uq", "ev_w_ukv", "ev_w_out", "od_w_in", "od_w_out")
NATIVE = ("ffn_w_up", "od_w_in")
SMALL_SHARDED = ("ffn_conv_w", "ev_conv_w", "od_conv_w", "od_conv_b")
SMALL_GRADS = ("norm_mix_g", "norm_ffn_g", "ffn_conv_w", "ffn_conv_b", "ev_conv_w", "ev_conv_b", "ev_ln_g", "ev_ln_b",
               "ev_qa_norm_g", "ev_kva_norm_g", "ev_q_norm_g", "ev_k_norm_g", "od_conv_w", "od_conv_b")
SMALL_ADAM = ("c_ctx", "ada_b") + SMALL_GRADS


def _size(shape):
    n = 1
    for v in shape:
        n *= v
    return n


def _pack(parts, dtype, row_mult, lead=0):
    lead_shape = parts[0].shape[:lead]
    flat = jnp.concatenate([p.astype(dtype).reshape(lead_shape + (-1,)) for p in parts], axis=-1)
    per = PACK_W * row_mult
    total = -(-flat.shape[-1] // per) * per
    flat = jnp.pad(flat, [(0, 0)] * lead + [(0, total - flat.shape[-1])])
    return flat.reshape(lead_shape + (total // PACK_W, PACK_W))


def _unpack(buf, shapes):
    lead_shape = buf.shape[:-2]
    flat = buf.reshape(lead_shape + (-1,))
    out, off = [], 0
    for shp in shapes:
        n = _size(shp)
        out.append(flat[..., off:off + n].reshape(lead_shape + tuple(shp)))
        off += n
    return out


def _unshard(pieces, k):
    t = jnp.moveaxis(pieces, 0, k)
    return t.reshape(t.shape[:k] + (t.shape[k] * t.shape[k + 1],) + t.shape[k + 2:])


def _shard_major(full, k):
    t = full.reshape(full.shape[:k] + (N_DEV, full.shape[k] // N_DEV) + full.shape[k + 1:])
    return jnp.moveaxis(t, k, 0)


def _my_shard(full, k, me):
    n = full.shape[k] // N_DEV
    return lax.dynamic_slice_in_dim(full, me * n, n, axis=k)


def kernel(x, c, ctx, c_ctx, ada_w, ada_b, norm_mix_g, norm_ffn_g, ffn_w_up, ffn_conv_w, ffn_conv_b, ffn_w_down, ev_w_in, ev_conv_w, ev_conv_b, ev_ln_g, ev_ln_b, ev_qa_norm_g, ev_w_uq, ev_kva_norm_g, ev_w_ukv, ev_q_norm_g, ev_k_norm_g, ev_w_out, od_w_in, od_conv_w, od_conv_b, od_w_out, loss_target, m_c_ctx, m_ada_w, m_ada_b, m_norm_mix_g, m_norm_ffn_g, m_ffn_w_up, m_ffn_conv_w, m_ffn_conv_b, m_ffn_w_down, m_ev_w_in, m_ev_conv_w, m_ev_conv_b, m_ev_ln_g, m_ev_ln_b, m_ev_qa_norm_g, m_ev_w_uq, m_ev_kva_norm_g, m_ev_w_ukv, m_ev_q_norm_g, m_ev_k_norm_g, m_ev_w_out, m_od_w_in, m_od_conv_w, m_od_conv_b, m_od_w_out, v_c_ctx, v_ada_w, v_ada_b, v_norm_mix_g, v_norm_ffn_g, v_ffn_w_up, v_ffn_conv_w, v_ffn_conv_b, v_ffn_w_down, v_ev_w_in, v_ev_conv_w, v_ev_conv_b, v_ev_ln_g, v_ev_ln_b, v_ev_qa_norm_g, v_ev_w_uq, v_ev_kva_norm_g, v_ev_w_ukv, v_ev_q_norm_g, v_ev_k_norm_g, v_ev_w_out, v_od_w_in, v_od_conv_w, v_od_conv_b, v_od_w_out):
    a = dict(locals())
    ix, iy, ic = _mesh_pos()
    me = 4 * ix + 2 * iy + ic
    xs, ctxs, target = x[0], ctx[0], loss_target[0]
    d = xs.shape[1]
    nl, _, cols = ada_w.shape

    pieces = [(n, l) for n in BIG if n not in NATIVE for l in range(a[n].shape[0])]
    piece_shapes = [a[n].shape[1:] for n, _ in pieces]
    wall = _allgather(_pack([a[n][l] for n, l in pieces], BF, 16), in_vmem=False, name="gather_big_weights")
    w = {"ffn_w_down4": []}
    for (n, l), p in zip(pieces, _unpack(wall, piece_shapes)):
        if n == "ffn_w_down":
            w["ffn_w_down4"].append(p.reshape(N_DEV // 2, 2 * p.shape[1], p.shape[2]))
        else:
            w[n] = _unshard(p, SHARD_DIM[n] - 1)[None]
    native = {}
    for n in NATIVE:
        nlay, rows, width = a[n].shape
        got = _allgather(a[n].astype(BF).reshape(nlay * rows, width), in_vmem=False, name="gather_" + n)
        native[n] = [got[:, l * rows:(l + 1) * rows] for l in range(nlay)]
    w["ffn_w_up8"] = native["ffn_w_up"]
    w["od_w_in"] = _unshard(native["od_w_in"][0], SHARD_DIM["od_w_in"] - 1)[None]
    sall = _allgather(_pack([c] + [a[n] for n in SMALL_SHARDED], F32, SUBLANES), in_vmem=True, name="gather_cond")
    sp = _unpack(sall, [c.shape] + [a[n].shape for n in SMALL_SHARDED])
    c8 = sp[0].reshape(N_DEV, d)
    for n, p in zip(SMALL_SHARDED, sp[1:]):
        w[n] = _unshard(p, SHARD_DIM[n])
    for n in SMALL_GRADS:
        if n not in SMALL_SHARDED:
            w[n] = a[n]

    cc = c_ctx.reshape(1, d)
    mpart = _ada_fwd(c8, cc, ada_w, lax.dynamic_slice_in_dim(ada_b, me * cols, cols, axis=1), name="ada_fwd")
    mall = _allgather(mpart.reshape(nl * ADA_ROWS, cols), in_vmem=True, name="gather_mod").reshape(N_DEV, nl, ADA_ROWS, cols)
    mine = lax.dynamic_index_in_dim(mall, me, axis=2, keepdims=False)
    mods = jnp.transpose(mine, (1, 0, 2)).reshape(nl, N_MOD, d)
    modc = mall[:, 0, N_DEV, :].reshape(-1)[:2 * d].reshape(2, d)

    loss_row, dx, g, dmods, dmodc = _local_step(xs, ctxs, target, mods, modc, w)
    loss = lax.psum(loss_row[0, 0], MESH_AXES)

    gparts = [g["ffn_w_down8"][l] if n == "ffn_w_down" else _shard_major(g[n][l], SHARD_DIM[n] - 1) for n, l in pieces]
    gsum = _sum8(_exchange(_pack(gparts, BF, 16, lead=1), name="exchange_big_grads"), name="sum_big_grads")
    grads = {}
    for (n, l), t in zip(pieces, _unpack(gsum, piece_shapes)):
        grads.setdefault(n, []).append(t)
    grads = {n: jnp.stack(ts) for n, ts in grads.items()}
    native_grads = {"ffn_w_up": jnp.concatenate(g["ffn_w_up8"], axis=1),
                    "od_w_in": _shard_major(g["od_w_in"][0], SHARD_DIM["od_w_in"] - 1).astype(BF)}
    for n in NATIVE:
        t = _sum8(_exchange(native_grads[n], name="exchange_grad_" + n), name="sum_grad_" + n)
        grads[n] = t.reshape(a[n].shape)

    small_parts = [dmods, dmodc] + [g[n] for n in SMALL_GRADS]
    small_shapes = [p.shape for p in small_parts]
    small = _allgather(_pack(small_parts, F32, SUBLANES), in_vmem=True, name="gather_small_grads")
    tots = _unpack(_sum8(small, name="sum_small_grads"), small_shapes)
    for n, t in zip(SMALL_GRADS, tots[2:]):
        grads[n] = _my_shard(t, SHARD_DIM[n], me) if n in SMALL_SHARDED else t
    dm_all = _unpack(small, small_shapes[:1])[0].reshape(N_DEV, nl, N_MOD * d)
    tot_dm = tots[0].reshape(nl, N_MOD * d)
    dmodc_pad = _pad_last(tots[1].reshape(1, 2 * d), N_MOD * d)
    dm_cols = lax.dynamic_slice_in_dim(dm_all, me * cols, cols, axis=2)
    dctx_cols = lax.dynamic_slice_in_dim(dmodc_pad, me * cols, cols, axis=1)
    ctx_rows = jnp.concatenate([dctx_cols[None], jnp.zeros((nl - 1, 1, cols), F32)], axis=0)
    g16 = jnp.concatenate([jnp.transpose(dm_cols, (1, 0, 2)), ctx_rows, jnp.zeros((nl, ADA_ROWS - N_DEV - 1, cols), F32)], axis=1)
    grads["ada_w"], grads["ada_b"], cpart = _ada_bwd(c8, cc, ada_w, g16, dctx_cols, tot_dm, dmodc_pad, name="ada_bwd")
    grads["c_ctx"] = _sum8(_allgather(cpart, in_vmem=True, name="gather_c_ctx_grad"), name="sum_c_ctx_grad")[0]

    delta, new_m, new_v = {}, {}, {}
    for n in BIG + ("ada_w",):
        shp = a[n].shape
        two = lambda t: t.reshape(-1, shp[-1])
        outs = _adamw(two(a[n]), two(grads[n]), two(a["m_" + n]), two(a["v_" + n]), name="adamw_" + n)
        delta[n], new_m[n], new_v[n] = (o.reshape(shp) for o in outs)
    shapes = [a[n].shape for n in SMALL_ADAM]
    packs = [_pack([src[pre + n] for n in SMALL_ADAM], F32, SUBLANES) for src, pre in ((a, ""), (grads, ""), (a, "m_"), (a, "v_"))]
    outs = _adamw(*packs, name="adamw_small")
    for dst, o in zip((delta, new_m, new_v), outs):
        dst.update(zip(SMALL_ADAM, _unpack(o, shapes)))

    return (loss, dx[None], *[grads[n].reshape(a[n].shape) for n in WEIGHTS], *[delta[n] for n in WEIGHTS],
            *[new_m[n] for n in WEIGHTS], *[new_v[n] for n in WEIGHTS])
```
